```python
import math
import jax
import jax.numpy as jnp
from jax import lax
import numpy as np

D_MODEL = 1024
BATCH = 4
SEQ = 8192
DEPTH = 2

CTX_LEN = 256
GRID_W = 64
EPS = 1e-6
D_MIX = D_MODEL
GLA_V = 3 * D_MIX // 8
GLA_DV = 64
GLA_HEADS = GLA_V // GLA_DV
GLA_DK = GLA_DV // 2
GLA_QK = GLA_HEADS * GLA_DK
GLA_LOWRANK = 16
GLA_TAU = 16.0
GLA_CHUNK = 64
HY_WIDTH = D_MIX // 4
HY_SHORT = 3
HY_BANDS = 16
HY_EMB = 1 + 2 * HY_BANDS
HY_FILTER_ORDER = 64
HY_TARGET = 1e-2
HY_FAST_PCT = 0.3
HY_SLOW_PCT = 1.5
SSD_INNER = D_MIX - GLA_V - HY_WIDTH
SSD_HEAD_DIM = 64
SSD_HEADS = SSD_INNER // SSD_HEAD_DIM
SSD_GROUPS = 2
SSD_HG = SSD_HEADS // SSD_GROUPS
SSD_STATE = 128
SSD_CONV = 3
SSD_CONV_DIM = SSD_INNER + 2 * SSD_GROUPS * SSD_STATE
SSD_CHUNK = 128
D_FF = -(-8 * D_MODEL // (3 * 256)) * 256
IN_SIZES = (GLA_QK, GLA_QK, GLA_V, GLA_V, GLA_LOWRANK, GLA_LOWRANK, 3 * HY_WIDTH,
            SSD_INNER, SSD_CONV_DIM, SSD_HEADS, SSD_HEADS)
IN_OFFSETS = tuple(int(v) for v in np.cumsum(IN_SIZES)[:-1])
D_IN = int(sum(IN_SIZES))

kernel_name = 'hybrid_gla_hyena_ssd_dit_block'


def rmsnorm(x, g):
    xf = x.astype(jnp.float32)
    y = xf * lax.rsqrt(jnp.mean(xf * xf, axis=-1, keepdims=True) + EPS)
    return (y * g.astype(jnp.float32)).astype(x.dtype)


def modulate(h, shift, scale):
    return h * (1.0 + scale) + shift


def flip(t):
    return jnp.flip(t, axis=1)


def dwconv(x, w, b):
    K = w.shape[0]
    y = lax.conv_general_dilated(x, w.astype(x.dtype)[:, None, :], window_strides=(1,),
                                 padding=[(K // 2, K // 2)],
                                 dimension_numbers=('NWC', 'WIO', 'NWC'),
                                 feature_group_count=x.shape[-1])
    return y + b


def to_col_major(t):
    bsz, L, C = t.shape
    rows = L // GRID_W
    return t.reshape(bsz, rows, GRID_W, C).transpose(0, 2, 1, 3).reshape(bsz, L, C)


def from_col_major(t):
    bsz, L, C = t.shape
    rows = L // GRID_W
    return t.reshape(bsz, GRID_W, rows, C).transpose(0, 2, 1, 3).reshape(bsz, L, C)


def gla_scan(q, k, v, log_a, s0):
    bsz, L, H, _ = q.shape
    dv = v.shape[-1]
    nc = L // GLA_CHUNK

    def chunks(t):
        return t.astype(jnp.float32).reshape(bsz, nc, GLA_CHUNK, H, t.shape[-1])

    q, k, v, log_a = chunks(q), chunks(k), chunks(v), chunks(log_a)
    b = jnp.cumsum(log_a, axis=2)
    b_last = b[:, :, -1:]
    q_dec = q * jnp.exp(b)
    scores = jnp.einsum('bcihd,bcjhd->bchij', q_dec, k * jnp.exp(-b))
    lower = jnp.tril(jnp.ones((GLA_CHUNK, GLA_CHUNK), dtype=bool))
    scores = jnp.where(lower, scores, 0.0)
    o = jnp.einsum('bchij,bcjhe->bcihe', scores, v)
    ds = jnp.einsum('bcjhd,bcjhe->bchde', k * jnp.exp(b_last - b), v)
    chunk_decay = jnp.exp(b_last[:, :, 0])

    def step(s, inp):
        dec, d = inp
        return dec[..., None] * s + d, s

    s_final, s_start = lax.scan(step, s0.astype(jnp.float32),
                                (jnp.moveaxis(chunk_decay, 1, 0), jnp.moveaxis(ds, 1, 0)))
    o = o + jnp.einsum('bcihd,bchde->bcihe', q_dec, jnp.moveaxis(s_start, 0, 1))
    return o.reshape(bsz, L, H, dv), s_final


def ssd_scan(x, log_a, bm, cm, s0):
    bsz, L, G, HG, P = x.shape
    N = bm.shape[-1]
    nc = L // SSD_CHUNK
    Q = SSD_CHUNK
    x = x.astype(jnp.float32).reshape(bsz, nc, Q, G, HG, P)
    bm = bm.astype(jnp.float32).reshape(bsz, nc, Q, G, N)
    cm = cm.astype(jnp.float32).reshape(bsz, nc, Q, G, N)
    cs = jnp.cumsum(log_a.astype(jnp.float32).reshape(bsz, nc, Q, G, HG), axis=2)
    cs = jnp.moveaxis(cs, 2, -1)
    lower = jnp.tril(jnp.ones((Q, Q), dtype=bool))
    seg = cs[..., :, None] - cs[..., None, :]
    decay = jnp.exp(jnp.where(lower, seg, -jnp.inf))
    cb = jnp.einsum('bcign,bcjgn->bcgij', cm, bm)
    y = jnp.einsum('bcgij,bcghij,bcjghp->bcighp', cb, decay, x)
    ds = jnp.einsum('bcjgn,bcghj,bcjghp->bcghpn', bm, jnp.exp(cs[..., -1:] - cs), x)
    chunk_decay = jnp.exp(cs[..., -1])

    def step(s, inp):
        dec, d = inp
        return dec[..., None, None] * s + d, s

    s_final, s_start = lax.scan(step, s0.astype(jnp.float32),
                                (jnp.moveaxis(chunk_decay, 1, 0), jnp.moveaxis(ds, 1, 0)))
    y = y + jnp.einsum('bcign,bcghi,bcghpn->bcighp', cm, jnp.exp(cs), jnp.moveaxis(s_start, 0, 1))
    return y.reshape(bsz, L, G, HG, P), s_final


def gla_mixer(q, k, v, g, gkf, gkb, lp, s_f, s_b):
    bsz, L, _ = q.shape
    q = q.reshape(bsz, L, GLA_HEADS, GLA_DK) * (GLA_DK ** -0.5)
    k = k.reshape(bsz, L, GLA_HEADS, GLA_DK)
    v = v.reshape(bsz, L, GLA_HEADS, GLA_DV)
    pre_f = (gkf @ lp['gla_gk_w_f'] + lp['gla_gk_b_f']).astype(jnp.float32)
    pre_b = (gkb @ lp['gla_gk_w_b'] + lp['gla_gk_b_b']).astype(jnp.float32)
    log_f = (jax.nn.log_sigmoid(pre_f) / GLA_TAU).reshape(bsz, L, GLA_HEADS, GLA_DK)
    log_b = (jax.nn.log_sigmoid(pre_b) / GLA_TAU).reshape(bsz, L, GLA_HEADS, GLA_DK)
    o_f, s_f = gla_scan(q, k, v, log_f, s_f)
    o_b, s_b = gla_scan(flip(q), flip(k), flip(v), flip(log_b), s_b)
    o = (o_f + flip(o_b)).astype(v.dtype)
    o = rmsnorm(o, lp['gla_norm_w']).reshape(bsz, L, GLA_V) * jax.nn.silu(g)
    return o, (s_f, s_b)


def hyena_filter(L, lp):
    t = jnp.arange(L, dtype=jnp.float32)
    rel = (t - (L // 2)) / L
    bands = jnp.linspace(1e-4, HY_BANDS - 1, HY_BANDS, dtype=jnp.float32)
    ang = 2.0 * math.pi * rel[:, None] * bands
    z = jnp.concatenate([rel[:, None], jnp.cos(ang), -jnp.sin(ang)], axis=-1)
    freq = lp['hy_freq']
    hdn = jnp.sin(freq * (z @ lp['hy_w1'] + lp['hy_b1']))
    hdn = jnp.sin(freq * (hdn @ lp['hy_w2'] + lp['hy_b2']))
    h = hdn @ lp['hy_w3']
    window = jnp.exp(-2.0 * jnp.abs(rel)[:, None] * lp['hy_decay'])
    return h * window


def fft_conv_centred(u, h):
    L = u.shape[1]
    n = 2 * L
    uf = jnp.fft.rfft(u.astype(jnp.float32), n=n, axis=1)
    hf = jnp.fft.rfft(h.astype(jnp.float32), n=n, axis=0)
    y = jnp.fft.irfft(uf * hf[None], n=n, axis=1)[:, L // 2: L // 2 + L]
    return y.astype(u.dtype)


def hyena_mixer(u, lp):
    L = u.shape[1]
    u = dwconv(u, lp['hy_short_w'], lp['hy_short_b'])
    x0, x1, v = jnp.split(u, 3, axis=-1)
    h = hyena_filter(L, lp)
    v = v * x1
    v = fft_conv_centred(v, h) + v * lp['hy_bias']
    return v * x0


def ssd_mixer(z, xbc, dtf, dtb, lp, s_f, s_b, col_major):
    bsz, L, _ = z.shape
    if col_major:
        xbc, dtf, dtb = to_col_major(xbc), to_col_major(dtf), to_col_major(dtb)
    xbc = jax.nn.silu(dwconv(xbc, lp['ssd_conv_w'], lp['ssd_conv_b']))
    xs, bm, cm = jnp.split(xbc, [SSD_INNER, SSD_INNER + SSD_GROUPS * SSD_STATE], axis=-1)
    xs = xs.reshape(bsz, L, SSD_GROUPS, SSD_HG, SSD_HEAD_DIM)
    bm = bm.reshape(bsz, L, SSD_GROUPS, SSD_STATE)
    cm = cm.reshape(bsz, L, SSD_GROUPS, SSD_STATE)
    dt_f = jax.nn.softplus(dtf.astype(jnp.float32) + lp['ssd_dt_bias_f']).reshape(bsz, L, SSD_GROUPS, SSD_HG)
    dt_b = jax.nn.softplus(dtb.astype(jnp.float32) + lp['ssd_dt_bias_b']).reshape(bsz, L, SSD_GROUPS, SSD_HG)
    a_f = -jnp.exp(lp['ssd_a_log_f'].astype(jnp.float32)).reshape(SSD_GROUPS, SSD_HG) * dt_f
    a_b = -jnp.exp(lp['ssd_a_log_b'].astype(jnp.float32)).reshape(SSD_GROUPS, SSD_HG) * dt_b
    y_f, s_f = ssd_scan(xs * dt_f[..., None], a_f, bm, cm, s_f)
    y_b, s_b = ssd_scan(flip(xs * dt_b[..., None]), flip(a_b), flip(bm), flip(cm), s_b)
    y = y_f + flip(y_b) + lp['ssd_d'].reshape(SSD_GROUPS, SSD_HG, 1) * xs
    y = y.reshape(bsz, L, SSD_INNER).astype(z.dtype)
    if col_major:
        y = from_col_major(y)
    yz = (y * jax.nn.silu(z)).reshape(bsz, L, SSD_GROUPS, SSD_INNER // SSD_GROUPS)
    y = rmsnorm(yz, lp['ssd_norm_w'].reshape(SSD_GROUPS, SSD_INNER // SSD_GROUPS))
    return y.reshape(bsz, L, SSD_INNER), (s_f, s_b)


def mixers(p, lp, states, col_major):
    q, k, v, g, gkf, gkb, hy_u, z, xbc, dtf, dtb = jnp.split(p, IN_OFFSETS, axis=-1)
    gla_y, (g_f, g_b) = gla_mixer(q, k, v, g, gkf, gkb, lp, states[0], states[1])
    hy_y = hyena_mixer(hy_u, lp)
    ssd_y, (m_f, m_b) = ssd_mixer(z, xbc, dtf, dtb, lp, states[2], states[3], col_major)
    return jnp.concatenate([gla_y, hy_y, ssd_y], axis=-1), (g_f, g_b, m_f, m_b)


def zero_states(bsz):
    gla = jnp.zeros((bsz, GLA_HEADS, GLA_DK, GLA_DV), jnp.float32)
    ssd = jnp.zeros((bsz, SSD_GROUPS, SSD_HG, SSD_HEAD_DIM, SSD_STATE), jnp.float32)
    return (gla, gla, ssd, ssd)


def swiglu(h, w1, w3, w2):
    return (jax.nn.silu(h @ w1) * (h @ w3)) @ w2


def setup_inputs(seed: int = 0) -> dict:
    key = jax.random.key(seed)
    ks = iter(jax.random.split(key, 48))

    def nrm(shape, scale):
        return scale * jax.random.normal(next(ks), shape, jnp.float32)

    def unif(shape, lo, hi):
        return jax.random.uniform(next(ks), shape, jnp.float32, lo, hi)

    x = nrm((BATCH, SEQ, D_MODEL), 1.0)
    c = nrm((BATCH, D_MODEL), 1.0)
    ctx = nrm((BATCH, CTX_LEN, D_MODEL), 1.0)
    c_ctx = nrm((D_MODEL,), 1.0)
    mod_w = nrm((DEPTH, D_MODEL, 6 * D_MODEL), 0.2 * D_MODEL ** -0.5)
    mod_b = nrm((DEPTH, 6 * D_MODEL), 0.02)
    norm1_g = 1.0 + nrm((DEPTH, D_MODEL), 0.02)
    norm2_g = 1.0 + nrm((DEPTH, D_MODEL), 0.02)
    w_in = nrm((DEPTH, D_MODEL, D_IN), D_MODEL ** -0.5)
    gla_gk_w_f = nrm((DEPTH, GLA_LOWRANK, GLA_QK), GLA_LOWRANK ** -0.5)
    gla_gk_b_f = nrm((DEPTH, GLA_QK), 0.1)
    gla_gk_w_b = nrm((DEPTH, GLA_LOWRANK, GLA_QK), GLA_LOWRANK ** -0.5)
    gla_gk_b_b = nrm((DEPTH, GLA_QK), 0.1)
    gla_norm_w = 1.0 + nrm((DEPTH, GLA_DV), 0.02)
    hy_short_w = nrm((DEPTH, HY_SHORT, 3 * HY_WIDTH), HY_SHORT ** -0.5)
    hy_short_b = nrm((DEPTH, 3 * HY_WIDTH), 0.02)
    hy_w1 = nrm((DEPTH, HY_EMB, HY_FILTER_ORDER), HY_EMB ** -0.5)
    hy_b1 = nrm((DEPTH, HY_FILTER_ORDER), 0.1)
    hy_w2 = nrm((DEPTH, HY_FILTER_ORDER, HY_FILTER_ORDER), HY_FILTER_ORDER ** -0.5)
    hy_b2 = nrm((DEPTH, HY_FILTER_ORDER), 0.1)
    hy_w3 = nrm((DEPTH, HY_FILTER_ORDER, HY_WIDTH), 0.005)
    hy_freq = 1.0 + nrm((DEPTH, HY_FILTER_ORDER), 0.05)
    base_decay = jnp.abs(jnp.linspace(math.log(HY_TARGET) / HY_FAST_PCT,
                                      math.log(HY_TARGET) / HY_SLOW_PCT, HY_WIDTH, dtype=jnp.float32))
    hy_decay = base_decay[None] * (1.0 + nrm((DEPTH, HY_WIDTH), 0.05))
    hy_bias = nrm((DEPTH, HY_WIDTH), 1.0)
    ssd_conv_w = nrm((DEPTH, SSD_CONV, SSD_CONV_DIM), SSD_CONV ** -0.5)
    ssd_conv_b = nrm((DEPTH, SSD_CONV_DIM), 0.02)
    dt_f0 = jnp.exp(unif((DEPTH, SSD_HEADS), math.log(1e-3), math.log(1e-1)))
    dt_b0 = jnp.exp(unif((DEPTH, SSD_HEADS), math.log(1e-3), math.log(1e-1)))
    ssd_dt_bias_f = dt_f0 + jnp.log(-jnp.expm1(-dt_f0))
    ssd_dt_bias_b = dt_b0 + jnp.log(-jnp.expm1(-dt_b0))
    ssd_a_log_f = jnp.log(unif((DEPTH, SSD_HEADS), 1.0, 16.0))
    ssd_a_log_b = jnp.log(unif((DEPTH, SSD_HEADS), 1.0, 16.0))
    ssd_d = 1.0 + nrm((DEPTH, SSD_HEADS), 0.02)
    ssd_norm_w = 1.0 + nrm((DEPTH, SSD_INNER), 0.02)
    w_out = nrm((DEPTH, D_MIX, D_MODEL), D_MIX ** -0.5)
    ffn_w1 = nrm((DEPTH, D_MODEL, D_FF), D_MODEL ** -0.5)
    ffn_w3 = nrm((DEPTH, D_MODEL, D_FF), D_MODEL ** -0.5)
    ffn_w2 = nrm((DEPTH, D_FF, D_MODEL), D_FF ** -0.5)
    final_g = 1.0 + nrm((D_MODEL,), 0.02)
    return {'x': x, 'c': c, 'ctx': ctx, 'c_ctx': c_ctx, 'mod_w': mod_w, 'mod_b': mod_b,
            'norm1_g': norm1_g, 'norm2_g': norm2_g, 'w_in': w_in,
            'gla_gk_w_f': gla_gk_w_f, 'gla_gk_b_f': gla_gk_b_f, 'gla_gk_w_b': gla_gk_w_b,
            'gla_gk_b_b': gla_gk_b_b, 'gla_norm_w': gla_norm_w,
            'hy_short_w': hy_short_w, 'hy_short_b': hy_short_b, 'hy_w1': hy_w1, 'hy_b1': hy_b1,
            'hy_w2': hy_w2, 'hy_b2': hy_b2, 'hy_w3': hy_w3, 'hy_freq': hy_freq,
            'hy_decay': hy_decay, 'hy_bias': hy_bias,
            'ssd_conv_w': ssd_conv_w, 'ssd_conv_b': ssd_conv_b, 'ssd_dt_bias_f': ssd_dt_bias_f,
            'ssd_dt_bias_b': ssd_dt_bias_b, 'ssd_a_log_f': ssd_a_log_f, 'ssd_a_log_b': ssd_a_log_b,
            'ssd_d': ssd_d, 'ssd_norm_w': ssd_norm_w, 'w_out': w_out,
            'ffn_w1': ffn_w1, 'ffn_w3': ffn_w3, 'ffn_w2': ffn_w2, 'final_g': final_g}


def reference(x, c, ctx, c_ctx, mod_w, mod_b, norm1_g, norm2_g, w_in,
              gla_gk_w_f, gla_gk_b_f, gla_gk_w_b, gla_gk_b_b, gla_norm_w,
              hy_short_w, hy_short_b, hy_w1, hy_b1, hy_w2, hy_b2, hy_w3, hy_freq, hy_decay, hy_bias,
              ssd_conv_w, ssd_conv_b, ssd_dt_bias_f, ssd_dt_bias_b, ssd_a_log_f, ssd_a_log_b,
              ssd_d, ssd_norm_w, w_out, ffn_w1, ffn_w3, ffn_w2, final_g):
    bsz = x.shape[0]
    c_act = jax.nn.silu(c)
    cc_act = jax.nn.silu(c_ctx)
    for l in range(DEPTH):
        lp = {'gla_gk_w_f': gla_gk_w_f[l], 'gla_gk_b_f': gla_gk_b_f[l],
              'gla_gk_w_b': gla_gk_w_b[l], 'gla_gk_b_b': gla_gk_b_b[l], 'gla_norm_w': gla_norm_w[l],
              'hy_short_w': hy_short_w[l], 'hy_short_b': hy_short_b[l], 'hy_w1': hy_w1[l],
              'hy_b1': hy_b1[l], 'hy_w2': hy_w2[l], 'hy_b2': hy_b2[l], 'hy_w3': hy_w3[l],
              'hy_freq': hy_freq[l], 'hy_decay': hy_decay[l], 'hy_bias': hy_bias[l],
              'ssd_conv_w': ssd_conv_w[l], 'ssd_conv_b': ssd_conv_b[l],
              'ssd_dt_bias_f': ssd_dt_bias_f[l], 'ssd_dt_bias_b': ssd_dt_bias_b[l],
              'ssd_a_log_f': ssd_a_log_f[l], 'ssd_a_log_b': ssd_a_log_b[l],
              'ssd_d': ssd_d[l], 'ssd_norm_w': ssd_norm_w[l]}
        mx = jnp.split(c_act @ mod_w[l] + mod_b[l], 6, axis=-1)
        mc = jnp.split(cc_act @ mod_w[l] + mod_b[l], 6, axis=-1)
        hx = modulate(rmsnorm(x, norm1_g[l]), mx[0][:, None], mx[1][:, None])
        hc = modulate(rmsnorm(ctx, norm1_g[l]), mc[0], mc[1])
        yc, ctx_states = mixers(hc @ w_in[l], lp, zero_states(bsz), False)
        yx, _ = mixers(hx @ w_in[l], lp, ctx_states, True)
        x = x + mx[2][:, None] * (yx @ w_out[l])
        hx2 = modulate(rmsnorm(x, norm2_g[l]), mx[3][:, None], mx[4][:, None])
        x = x + mx[5][:, None] * swiglu(hx2, ffn_w1[l], ffn_w3[l], ffn_w2[l])
        if l < DEPTH - 1:
            ctx = ctx + mc[2] * (yc @ w_out[l])
            hc2 = modulate(rmsnorm(ctx, norm2_g[l]), mc[3], mc[4])
            ctx = ctx + mc[5] * swiglu(hc2, ffn_w1[l], ffn_w3[l], ffn_w2[l])
    return rmsnorm(x, final_g)
```

```python
import functools
import math

import numpy as np
import jax
import jax.numpy as jnp
from jax import lax
from jax.experimental import pallas as pl
from jax.experimental.pallas import tpu as pltpu

F32 = jnp.float32
BF16 = jnp.bfloat16

D_MODEL = 1024
DEPTH = 2
GRID_W = 64
EPS = 1e-6
GLA_V = 384
GLA_DV = 64
GLA_HEADS = 6
GLA_DK = 32
GLA_QK = 192
GLA_LOWRANK = 16
GLA_TAU = 16.0
GLA_CHUNK = 64
GLA_BLOCK = 256
HY_WIDTH = 256
HY_BANDS = 16
HY_EMB = 1 + 2 * HY_BANDS
HY_ORDER = 64
SSD_INNER = 384
SSD_HEADS = 6
SSD_GROUPS = 2
SSD_HG = 3
SSD_P = 64
SSD_STATE = 128
SSD_CONV_DIM = SSD_INNER + 2 * SSD_GROUPS * SSD_STATE
SSD_CHUNK = 128
D_FF = 2816
FF_TILE = 1408
PG_W = 1408
PH_W = 768
PS_W = 1408
LANE = 128
FFT_N1 = 128
FFT_N = FFT_N1 * FFT_N1

VMEM_LIMIT = 56 * 1024 * 1024


def _cp(*sem):
    return pltpu.CompilerParams(dimension_semantics=sem, vmem_limit_bytes=VMEM_LIMIT)


def _bf(x):
    return x.astype(BF16)


def _dot(a, b):
    return jnp.dot(a, b, preferred_element_type=F32)


def _dot_nt(a, b):
    return lax.dot_general(a, b, (((1,), (1,)), ((), ())), preferred_element_type=F32)


def _dot_tn(a, b):
    return lax.dot_general(a, b, (((0,), (0,)), ((), ())), preferred_element_type=F32)


def _split3(x):
    hi = _bf(x)
    r1 = x - hi.astype(F32)
    mid = _bf(r1)
    lo = _bf(r1 - mid.astype(F32))
    return hi, mid, lo


def _dot_sel(sel_bf, x):
    hi, mid, lo = _split3(x)
    return _dot(sel_bf, hi) + _dot(sel_bf, mid) + _dot(sel_bf, lo)


def _dot_hp(a, b):
    ah = _bf(a)
    al = _bf(a - ah.astype(F32))
    bh = _bf(b)
    bl = _bf(b - bh.astype(F32))
    return _dot(ah, bh) + _dot(ah, bl) + _dot(al, bh)


def _silu(x):
    return x * jax.nn.sigmoid(x)


def _softplus(x):
    return jnp.maximum(x, 0.0) + jnp.log1p(jnp.exp(-jnp.abs(x)))


def _log_sigmoid(x):
    return jnp.minimum(x, 0.0) - jnp.log1p(jnp.exp(-jnp.abs(x)))


def _mod_kernel(c_ref, w_ref, b_ref, o_ref):
    act = _silu(c_ref[...])
    o_ref[0] = _dot(_bf(act), _bf(w_ref[0])) + b_ref[0]


def _mod_call(c8, mod_w, mod_b):
    nt = 1536
    n = mod_w.shape[-1]
    return pl.pallas_call(
        _mod_kernel,
        grid=(DEPTH, n // nt),
        in_specs=[pl.BlockSpec((8, D_MODEL), lambda l, j: (0, 0)),
                  pl.BlockSpec((1, D_MODEL, nt), lambda l, j: (l, 0, j)),
                  pl.BlockSpec((1, 1, nt), lambda l, j: (l, 0, j))],
        out_specs=pl.BlockSpec((1, 8, nt), lambda l, j: (l, 0, j)),
        out_shape=jax.ShapeDtypeStruct((DEPTH, 8, n), F32),
        compiler_params=_cp("arbitrary", "arbitrary"),
        name="adaln_mod",
    )(c8, mod_w, mod_b.reshape(DEPTH, 1, n))


def _in_kernel(x_ref, mod_ref, g_ref, w_ref, og_ref, oh_ref, os_ref):
    x = x_ref[...]
    ms = jnp.mean(x * x, axis=-1, keepdims=True)
    y = x * lax.rsqrt(ms + EPS) * g_ref[...]
    m = mod_ref[0]
    h = _bf(y * (1.0 + m[1:2]) + m[0:1])
    og_ref[...] = _dot(h, w_ref[:, 0:PG_W])
    oh_ref[...] = _dot(h, w_ref[:, PG_W:PG_W + PH_W])
    os_ref[...] = _dot(h, w_ref[:, PG_W + PH_W:])


def _in_call(x2, mods, mod_row, g, w):
    t = x2.shape[0]
    tm = 512
    wtot = PG_W + PH_W + PS_W
    return pl.pallas_call(
        _in_kernel,
        grid=(t // tm,),
        in_specs=[pl.BlockSpec((tm, D_MODEL), lambda i: (i, 0)),
                  pl.BlockSpec((1, 6, D_MODEL), lambda i: (mod_row(i), 0, 0)),
                  pl.BlockSpec((1, D_MODEL), lambda i: (0, 0)),
                  pl.BlockSpec((D_MODEL, wtot), lambda i: (0, 0))],
        out_specs=[pl.BlockSpec((tm, PG_W), lambda i: (i, 0)),
                   pl.BlockSpec((tm, PH_W), lambda i: (i, 0)),
                   pl.BlockSpec((tm, PS_W), lambda i: (i, 0))],
        out_shape=[jax.ShapeDtypeStruct((t, PG_W), F32),
                   jax.ShapeDtypeStruct((t, PH_W), F32),
                   jax.ShapeDtypeStruct((t, PS_W), F32)],
        compiler_params=_cp("arbitrary"),
        name="in_proj",
    )(x2, mods, g.reshape(1, D_MODEL), w)


def _gla_kernel(*refs, reverse, nblk):
    if reverse:
        (p_ref, of_ref, wgk_ref, bgk_ref, nw_ref, s0_ref, y_ref, sf_ref, st_scr) = refs
    else:
        (p_ref, wgk_ref, bgk_ref, s0_ref, of_ref, sf_ref, st_scr) = refs
    i = pl.program_id(1)
    tb = GLA_BLOCK
    nch = tb // GLA_CHUNK

    @pl.when(i == 0)
    def _():
        st_scr[...] = s0_ref[0]

    p = p_ref[0]
    q = p[:, 0:256]
    k = p[:, 256:512]
    v = p[:, 512:896]
    gk = p[:, 1280:1408]
    pre = _dot(_bf(gk), wgk_ref[...]) + bgk_ref[...]
    la = _log_sigmoid(pre) * (1.0 / GLA_TAU)

    ri = lax.broadcasted_iota(jnp.int32, (tb, tb), 0)
    ci = lax.broadcasted_iota(jnp.int32, (tb, tb), 1)
    same = (ri // GLA_CHUNK) == (ci // GLA_CHUNK)
    tri = same & ((ci >= ri) if reverse else (ci <= ri))
    b = _dot_sel(jnp.where(tri, 1.0, 0.0).astype(BF16), la)

    qd = _bf(q * ((GLA_DK ** -0.5) * jnp.exp(b)))
    ki = _bf(k * jnp.exp(-b))
    vb = _bf(v)

    lane = lax.broadcasted_iota(jnp.int32, (1, 256), 1)
    hms = [(lane // GLA_DK) == h for h in range(GLA_HEADS)]

    lhs = jnp.concatenate([jnp.where(hm, qd, jnp.zeros_like(qd)) for hm in hms], axis=0)
    s = _dot_nt(lhs, ki)
    pm = jnp.concatenate([_bf(jnp.where(tri, s[h * tb:(h + 1) * tb], 0.0)) for h in range(GLA_HEADS)], axis=0)
    ra = _dot(pm[:4 * tb], vb[:, 0:256])
    rb = _dot(pm[4 * tb:], vb[:, 256:384])
    hl = lane // GLA_DV
    oa = jnp.zeros((tb, 256), F32)
    for h in range(4):
        oa = oa + jnp.where(hl == h, ra[h * tb:(h + 1) * tb], 0.0)
    ob = jnp.where(hl[:, :LANE] == 0, rb[:tb], rb[tb:])
    o_intra = jnp.concatenate([oa, ob], axis=1)

    st = st_scr[...]
    outs = [None] * nch
    order = range(nch - 1, -1, -1) if reverse else range(nch)
    for c in order:
        r0 = c * GLA_CHUNK
        bc = b[r0:r0 + GLA_CHUNK]
        edge = r0 if reverse else r0 + GLA_CHUNK - 1
        bl = b[edge:edge + 1]
        kd = _bf(k[r0:r0 + GLA_CHUNK] * jnp.exp(bl - bc))
        stbd = _bf(jnp.concatenate([jnp.where(hm, st, 0.0) for hm in hms], axis=0))
        outs[c] = _dot_nt(qd[r0:r0 + GLA_CHUNK], stbd)
        full = _dot_tn(vb[r0:r0 + GLA_CHUNK], kd)
        ds = jnp.zeros((GLA_DV, 256), F32)
        for h in range(GLA_HEADS):
            ds = ds + jnp.where(hms[h], full[h * GLA_DV:(h + 1) * GLA_DV], 0.0)
        st = jnp.exp(bl) * st + ds
    st_scr[...] = st
    o = o_intra + jnp.concatenate(outs, axis=0)

    @pl.when(i == nblk - 1)
    def _():
        sf_ref[0] = st

    if not reverse:
        of_ref[0] = o
    else:
        ot = of_ref[0] + o
        r2 = lax.broadcasted_iota(jnp.int32, (GLA_V, GLA_V), 0) // GLA_DV
        c2 = lax.broadcasted_iota(jnp.int32, (GLA_V, GLA_V), 1) // GLA_DV
        ind = jnp.where(r2 == c2, 1.0, 0.0).astype(BF16)
        sq = ot * ot
        sh = _bf(sq)
        sl = _bf(sq - sh.astype(F32))
        ms = (_dot(sh, ind) + _dot(sl, ind)) * (1.0 / GLA_DV)
        g = p[:, 896:1280]
        y_ref[0] = ot * lax.rsqrt(ms + EPS) * nw_ref[...] * _silu(g)


def _gla_call(pg, o_f, wgk, bgk, nw, s0, reverse):
    bsz, L, _ = pg.shape
    nblk = L // GLA_BLOCK
    blk = (lambda b, i: (b, nblk - 1 - i, 0)) if reverse else (lambda b, i: (b, i, 0))
    const2 = lambda b, i: (0, 0)
    st_spec = pl.BlockSpec((1, GLA_DV, 256), lambda b, i: (b, 0, 0))
    p_spec = pl.BlockSpec((1, GLA_BLOCK, PG_W), blk)
    o_spec = pl.BlockSpec((1, GLA_BLOCK, GLA_V), blk)
    if reverse:
        in_specs = [p_spec, o_spec, pl.BlockSpec((LANE, 256), const2), pl.BlockSpec((1, 256), const2),
                    pl.BlockSpec((1, GLA_V), const2), st_spec]
        args = (pg, o_f, wgk, bgk, nw, s0)
    else:
        in_specs = [p_spec, pl.BlockSpec((LANE, 256), const2), pl.BlockSpec((1, 256), const2), st_spec]
        args = (pg, wgk, bgk, s0)
    return pl.pallas_call(
        functools.partial(_gla_kernel, reverse=reverse, nblk=nblk),
        grid=(bsz, nblk),
        in_specs=in_specs,
        out_specs=[o_spec, st_spec],
        out_shape=[jax.ShapeDtypeStruct((bsz, L, GLA_V), F32),
                   jax.ShapeDtypeStruct((bsz, GLA_DV, 256), F32)],
        scratch_shapes=[pltpu.VMEM((GLA_DV, 256), F32)],
        compiler_params=_cp("arbitrary", "arbitrary"),
        name="gla_bwd" if reverse else "gla_fwd",
    )(*args)


def _expand_heads(t, lo):
    r = t.shape[0]
    lane = lax.broadcasted_iota(jnp.int32, (1, LANE), 1)
    tiles = []
    for j in range(SSD_HEADS // 2):
        a = jnp.broadcast_to(t[:, lo + 2 * j:lo + 2 * j + 1], (r, LANE))
        b = jnp.broadcast_to(t[:, lo + 2 * j + 1:lo + 2 * j + 2], (r, LANE))
        tiles.append(jnp.where(lane < SSD_P, a, b))
    return jnp.concatenate(tiles, axis=1)


def _ssd_kernel(*refs, reverse, nchunk):
    if reverse:
        (cur_ref, prev_ref, next_ref, yf_ref, cw_ref, cb_ref, dtb_ref, alog_ref, dx_ref, nw_ref, s0_ref,
         y_ref, sf_ref, st_scr) = refs
    else:
        (cur_ref, prev_ref, next_ref, cw_ref, cb_ref, dtb_ref, alog_ref, s0_ref,
         yf_ref, sf_ref, st_scr) = refs
    i = pl.program_id(1)
    c = (nchunk - 1 - i) if reverse else i
    q_ = SSD_CHUNK
    lo = SSD_HEADS if reverse else 0

    @pl.when(i == 0)
    def _():
        st_scr[...] = s0_ref[0]

    cur = cur_ref[0]
    x = cur[:, 384:1280]
    has_prev = (c > 0).astype(F32)
    has_next = (c < nchunk - 1).astype(F32)
    prow = prev_ref[0][7:8, 384:1280] * has_prev
    nrow = next_ref[0][0:1, 384:1280] * has_next
    ridx = lax.broadcasted_iota(jnp.int32, (q_, 1), 0)
    xm = jnp.where(ridx == 0, prow, pltpu.roll(x, 1, 0))
    xp = jnp.where(ridx == q_ - 1, nrow, pltpu.roll(x, q_ - 1, 0))
    conv = cw_ref[0:1] * xm + cw_ref[1:2] * x + cw_ref[2:3] * xp + cb_ref[...]
    act = _silu(conv)
    xs = act[:, 0:384]
    bm = act[:, 384:640]
    cm = act[:, 640:896]
    dtt = _softplus(cur[:, 1280:1408] + dtb_ref[...])
    a = -jnp.exp(alog_ref[...]) * dtt

    ri = lax.broadcasted_iota(jnp.int32, (q_, q_), 0)
    ci = lax.broadcasted_iota(jnp.int32, (q_, q_), 1)
    tri = (ci >= ri) if reverse else (ci <= ri)
    cs = _dot_sel(jnp.where(tri, 1.0, 0.0).astype(BF16), a)
    cst = cs.T
    dtT = dtt.T
    edge = 0 if reverse else q_ - 1
    cs_last = cs[edge:edge + 1]

    bmb = _bf(bm)
    cmb = _bf(cm)
    cbs = [_dot_nt(cmb[:, g * SSD_STATE:(g + 1) * SSD_STATE], bmb[:, g * SSD_STATE:(g + 1) * SSD_STATE])
           for g in range(SSD_GROUPS)]
    ms = []
    for h in range(SSD_HEADS):
        l = lo + h
        seg = cs[:, l:l + 1] - cst[l:l + 1, :]
        dec = jnp.exp(jnp.where(tri, seg, -jnp.inf))
        ms.append(_bf(cbs[h // SSD_HG] * dec * dtT[l:l + 1, :]))
    mst = jnp.concatenate(ms, axis=0)
    xsb = _bf(xs)
    ra = _dot(mst[:4 * q_], xsb[:, 0:256])
    rb = _dot(mst[4 * q_:], xsb[:, 256:384])
    lane = lax.broadcasted_iota(jnp.int32, (1, 256), 1)
    hl = lane // SSD_P
    ya = jnp.zeros((q_, 256), F32)
    for h in range(4):
        ya = ya + jnp.where(hl == h, ra[h * q_:(h + 1) * q_], 0.0)
    yb = jnp.where(hl[:, :LANE] == 0, rb[:q_], rb[q_:])
    y = jnp.concatenate([ya, yb], axis=1)

    st = st_scr[...]
    cs_x = _expand_heads(cs, lo)
    csl_x = _expand_heads(cs_last, lo)
    y = y + jnp.exp(cs_x) * _dot(cmb, _bf(st))
    xw = _bf(xs * (jnp.exp(csl_x - cs_x) * _expand_heads(dtt, lo)))
    full = _dot_tn(bmb, xw)
    r2 = lax.broadcasted_iota(jnp.int32, (2 * SSD_STATE, SSD_INNER), 0) // SSD_STATE
    c2 = lax.broadcasted_iota(jnp.int32, (2 * SSD_STATE, SSD_INNER), 1) // (SSD_HG * SSD_P)
    st = jnp.exp(csl_x) * st + jnp.where(r2 == c2, full, 0.0)
    st_scr[...] = st

    @pl.when(i == nchunk - 1)
    def _():
        sf_ref[0] = st

    if not reverse:
        yf_ref[0, 0] = y
    else:
        yt = yf_ref[0, 0] + y + dx_ref[...] * xs
        yz = yt * _silu(cur[:, 0:384])
        l384 = lax.broadcasted_iota(jnp.int32, (1, SSD_INNER), 1)
        g0 = l384 < (SSD_INNER // SSD_GROUPS)
        sq = yz * yz
        m0 = jnp.sum(jnp.where(g0, sq, 0.0), axis=-1, keepdims=True)
        m1 = jnp.sum(jnp.where(g0, 0.0, sq), axis=-1, keepdims=True)
        msq = jnp.where(g0, m0, m1) * (1.0 / (SSD_INNER // SSD_GROUPS))
        y_ref[0] = yz * lax.rsqrt(msq + EPS) * nw_ref[...]


def _ssd_call(ps, y_f, cw, cb, dtb, alog, dx, nw, s0, reverse, col_major):
    bsz, L, _ = ps.shape
    nchunk = L // SSD_CHUNK
    if col_major:
        assert nchunk == GRID_W
        src = ps.reshape(bsz, SSD_CHUNK, GRID_W * PS_W)
        cidx = (lambda i: nchunk - 1 - i) if reverse else (lambda i: i)
        cur_map = lambda b, i: (b, 0, cidx(i))
        prev_map = lambda b, i: (b, SSD_CHUNK // 8 - 1, jnp.maximum(cidx(i) - 1, 0))
        next_map = lambda b, i: (b, 0, jnp.minimum(cidx(i) + 1, nchunk - 1))
        y_shape = (bsz, SSD_CHUNK, GRID_W * SSD_INNER)
    else:
        src = ps
        cidx = (lambda i: nchunk - 1 - i) if reverse else (lambda i: i)
        cur_map = lambda b, i: (b, cidx(i), 0)
        prev_map = lambda b, i: (b, jnp.maximum(cidx(i) * (SSD_CHUNK // 8) - 1, 0), 0)
        next_map = lambda b, i: (b, jnp.minimum((cidx(i) + 1) * (SSD_CHUNK // 8), L // 8 - 1), 0)
        y_shape = (bsz, L, SSD_INNER)
    const2 = lambda b, i: (0, 0)
    cur_spec = pl.BlockSpec((1, SSD_CHUNK, PS_W), cur_map)
    prev_spec = pl.BlockSpec((1, 8, PS_W), prev_map)
    next_spec = pl.BlockSpec((1, 8, PS_W), next_map)
    yf_spec = pl.BlockSpec((1, 1, SSD_CHUNK, SSD_INNER), lambda b, i: (b, cidx(i), 0, 0))
    st_spec = pl.BlockSpec((1, 2 * SSD_STATE, SSD_INNER), lambda b, i: (b, 0, 0))
    w_specs = [pl.BlockSpec((8, SSD_CONV_DIM), const2), pl.BlockSpec((1, SSD_CONV_DIM), const2),
               pl.BlockSpec((1, LANE), const2), pl.BlockSpec((1, LANE), const2)]
    if reverse:
        in_specs = [cur_spec, prev_spec, next_spec, yf_spec] + w_specs + [
            pl.BlockSpec((1, SSD_INNER), const2), pl.BlockSpec((1, SSD_INNER), const2), st_spec]
        args = (src, src, src, y_f, cw, cb, dtb, alog, dx, nw, s0)
        out_specs = [pl.BlockSpec((1, SSD_CHUNK, SSD_INNER), cur_map), st_spec]
        out_shape = [jax.ShapeDtypeStruct(y_shape, F32),
                     jax.ShapeDtypeStruct((bsz, 2 * SSD_STATE, SSD_INNER), F32)]
    else:
        in_specs = [cur_spec, prev_spec, next_spec] + w_specs + [st_spec]
        args = (src, src, src, cw, cb, dtb, alog, s0)
        out_specs = [yf_spec, st_spec]
        out_shape = [jax.ShapeDtypeStruct((bsz, nchunk, SSD_CHUNK, SSD_INNER), F32),
                     jax.ShapeDtypeStruct((bsz, 2 * SSD_STATE, SSD_INNER), F32)]
    y, sf = pl.pallas_call(
        functools.partial(_ssd_kernel, reverse=reverse, nchunk=nchunk),
        grid=(bsz, nchunk),
        in_specs=in_specs,
        out_specs=out_specs,
        out_shape=out_shape,
        scratch_shapes=[pltpu.VMEM((2 * SSD_STATE, SSD_INNER), F32)],
        compiler_params=_cp("arbitrary", "arbitrary"),
        name="ssd_bwd" if reverse else "ssd_fwd",
    )(*args)
    if reverse:
        y = y.reshape(bsz, L, SSD_INNER)
    return y, sf


def _hy_pre_kernel(cur_ref, prev_ref, next_ref, w_ref, b_ref, vg_ref, x0_ref, *, nblk):
    i = pl.program_id(1)
    x = cur_ref[0]
    tb = x.shape[0]
    has_prev = (i > 0).astype(F32)
    has_next = (i < nblk - 1).astype(F32)
    prow = prev_ref[0][7:8] * has_prev
    nrow = next_ref[0][0:1] * has_next
    ridx = lax.broadcasted_iota(jnp.int32, (tb, 1), 0)
    xm = jnp.where(ridx == 0, prow, pltpu.roll(x, 1, 0))
    xp = jnp.where(ridx == tb - 1, nrow, pltpu.roll(x, tb - 1, 0))
    u = w_ref[0:1] * xm + w_ref[1:2] * x + w_ref[2:3] * xp + b_ref[...]
    x0_ref[0] = u[:, 0:HY_WIDTH]
    vg_ref[0] = u[:, 2 * HY_WIDTH:] * u[:, HY_WIDTH:2 * HY_WIDTH]


def _hy_pre_call(ph, w8, b):
    bsz, L, _ = ph.shape
    tb = min(512, L)
    nblk = L // tb
    r8 = tb // 8
    return pl.pallas_call(
        functools.partial(_hy_pre_kernel, nblk=nblk),
        grid=(bsz, nblk),
        in_specs=[pl.BlockSpec((1, tb, PH_W), lambda b_, i: (b_, i, 0)),
                  pl.BlockSpec((1, 8, PH_W), lambda b_, i: (b_, jnp.maximum(i * r8 - 1, 0), 0)),
                  pl.BlockSpec((1, 8, PH_W), lambda b_, i: (b_, jnp.minimum((i + 1) * r8, L // 8 - 1), 0)),
                  pl.BlockSpec((8, PH_W), lambda b_, i: (0, 0)),
                  pl.BlockSpec((1, PH_W), lambda b_, i: (0, 0))],
        out_specs=[pl.BlockSpec((1, tb, HY_WIDTH), lambda b_, i: (b_, i, 0)),
                   pl.BlockSpec((1, tb, HY_WIDTH), lambda b_, i: (b_, i, 0))],
        out_shape=[jax.ShapeDtypeStruct((bsz, L, HY_WIDTH), F32),
                   jax.ShapeDtypeStruct((bsz, L, HY_WIDTH), F32)],
        compiler_params=_cp("arbitrary", "arbitrary"),
        name="hy_pre",
    )(ph, ph, ph, w8, b)


def _hy_filter_kernel(z_ref, w1_ref, b1_ref, w2_ref, b2_ref, w3_ref, fr_ref, dec_ref, h_ref):
    z = z_ref[...]
    fr = fr_ref[...]
    h1 = jnp.sin(fr * (_dot_hp(z, w1_ref[...]) + b1_ref[...]))
    h2 = jnp.sin(fr * (_dot_hp(h1, w2_ref[...]) + b2_ref[...]))
    h = _dot_hp(h2, w3_ref[...])
    win = jnp.exp(-2.0 * jnp.abs(z[:, 0:1]) * dec_ref[...])
    h_ref[...] = h * win


def _hy_features(L):
    t = jnp.arange(L, dtype=F32)
    rel = (t - (L // 2)) / L
    bands = jnp.linspace(1e-4, HY_BANDS - 1, HY_BANDS, dtype=F32)
    ang = 2.0 * math.pi * rel[:, None] * bands
    z = jnp.concatenate([rel[:, None], jnp.cos(ang), -jnp.sin(ang)], axis=-1)
    return jnp.pad(z, ((0, 0), (0, LANE - HY_EMB)))


def _hy_filter_call(L, w1p, b1p, w2p, b2p, w3p, frp, dec):
    z = _hy_features(L)
    tl = min(1024, L)
    c2 = lambda i: (0, 0)
    return pl.pallas_call(
        _hy_filter_kernel,
        grid=(L // tl,),
        in_specs=[pl.BlockSpec((tl, LANE), lambda i: (i, 0)),
                  pl.BlockSpec((LANE, LANE), c2), pl.BlockSpec((1, LANE), c2),
                  pl.BlockSpec((LANE, LANE), c2), pl.BlockSpec((1, LANE), c2),
                  pl.BlockSpec((LANE, HY_WIDTH), c2), pl.BlockSpec((1, LANE), c2),
                  pl.BlockSpec((1, HY_WIDTH), c2)],
        out_specs=pl.BlockSpec((tl, HY_WIDTH), lambda i: (i, 0)),
        out_shape=jax.ShapeDtypeStruct((L, HY_WIDTH), F32),
        compiler_params=_cp("arbitrary"),
        name="hy_filter",
    )(z, w1p, b1p, w2p, b2p, w3p, frp, dec)


@functools.lru_cache(maxsize=None)
def _fft_consts():
    n1 = FFT_N1
    half = n1 // 2
    k = np.arange(n1, dtype=np.float64)
    n2 = k[:, None, None]
    k1 = k[None, :, None]
    nn = np.arange(half, dtype=np.float64)[None, None, :]
    ang = -2.0 * np.pi * (n2 * k1 / FFT_N + nn * k1 / n1)
    mr, mi = np.cos(ang), np.sin(ang)
    m1 = np.concatenate([np.concatenate([mr, -mi], axis=2), np.concatenate([mi, mr], axis=2)], axis=1)
    ang2 = -2.0 * np.pi * np.outer(k, k) / n1
    fr, fi = np.cos(ang2), np.sin(ang2)
    f2 = np.block([[fr, -fi], [fi, fr]])
    f2c = np.block([[fr, fi], [-fi, fr]])
    no = (np.arange(half, dtype=np.float64) + n1 // 4)[None, :, None]
    kk = k[None, None, :]
    ang3 = 2.0 * np.pi * (n2 * kk / FFT_N + no * kk / n1)
    ir, ii = np.cos(ang3) / FFT_N, np.sin(ang3) / FFT_N
    m3 = np.concatenate([np.concatenate([ir, -ii], axis=2), np.concatenate([ii, ir], axis=2)], axis=1)
    return tuple(np.asarray(m, dtype=np.float32) for m in (m1, f2, f2c, m3))


FFT_NB = 8


def _fft1_kernel(u_ref, m_ref, a_ref):
    for t in range(FFT_NB):
        ua = u_ref[0][:, t * HY_WIDTH:(t + 1) * HY_WIDTH]
        ub = u_ref[1][:, t * HY_WIDTH:(t + 1) * HY_WIDTH]
        rhs = _bf(jnp.concatenate([ua, ub], axis=0))
        a_ref[0, t] = _bf(_dot(m_ref[t], rhs))


def _fft1_call(u, m1):
    npair = u.shape[0] // 2
    uv = u.reshape(2 * npair, FFT_N1 // 2, FFT_N1 * HY_WIDTH)
    return pl.pallas_call(
        _fft1_kernel,
        grid=(npair, FFT_N1 // FFT_NB),
        in_specs=[pl.BlockSpec((2, FFT_N1 // 2, FFT_NB * HY_WIDTH), lambda p, j: (p, 0, j)),
                  pl.BlockSpec((FFT_NB, 2 * FFT_N1, FFT_N1), lambda p, j: (j, 0, 0))],
        out_specs=pl.BlockSpec((1, FFT_NB, 2 * FFT_N1, HY_WIDTH), lambda p, j: (p, j, 0, 0)),
        out_shape=jax.ShapeDtypeStruct((npair, FFT_N1, 2 * FFT_N1, HY_WIDTH), BF16),
        compiler_params=_cp("arbitrary", "arbitrary"),
        name="hy_fft1",
    )(uv, m1)


def _fft2_kernel(*refs, spectrum):
    if spectrum:
        ar_ref, ai_ref, f_ref, o_ref = refs
    else:
        ar_ref, ai_ref, f_ref, fc_ref, h_ref, o_ref = refs
    for t in range(FFT_NB):
        sl = slice(t * HY_WIDTH, (t + 1) * HY_WIDTH)
        rhs = jnp.concatenate([ar_ref[0][:, sl], ai_ref[0][:, sl]], axis=0)
        x = _dot(f_ref[...], rhs)
        if spectrum:
            o_ref[t] = x
        else:
            xr, xi = x[:FFT_N1], x[FFT_N1:]
            hr, hi = h_ref[t, :FFT_N1], h_ref[t, FFT_N1:]
            y = jnp.concatenate([xr * hr - xi * hi, xr * hi + xi * hr], axis=0)
            o_ref[0, t] = _bf(_dot(fc_ref[...], _bf(y)))


def _fft2_call(a, f2, f2c, hspec):
    npair = a.shape[0]
    av = a.reshape(npair, FFT_N1, 2 * FFT_N1 * HY_WIDTH)
    nj = FFT_N1 // FFT_NB
    w = FFT_NB * HY_WIDTH
    c2 = lambda p, j: (0, 0)
    in_specs = [pl.BlockSpec((1, FFT_N1, w), lambda p, j: (p, 0, j)),
                pl.BlockSpec((1, FFT_N1, w), lambda p, j: (p, 0, nj + j)),
                pl.BlockSpec((2 * FFT_N1, 2 * FFT_N1), c2)]
    if hspec is None:
        return pl.pallas_call(
            functools.partial(_fft2_kernel, spectrum=True),
            grid=(1, nj),
            in_specs=in_specs,
            out_specs=pl.BlockSpec((FFT_NB, 2 * FFT_N1, HY_WIDTH), lambda p, j: (j, 0, 0)),
            out_shape=jax.ShapeDtypeStruct((FFT_N1, 2 * FFT_N1, HY_WIDTH), F32),
            compiler_params=_cp("arbitrary", "arbitrary"),
            name="hy_fft2_spec",
        )(av, av, f2)
    in_specs += [pl.BlockSpec((2 * FFT_N1, 2 * FFT_N1), c2),
                 pl.BlockSpec((FFT_NB, 2 * FFT_N1, HY_WIDTH), lambda p, j: (j, 0, 0))]
    return pl.pallas_call(
        functools.partial(_fft2_kernel, spectrum=False),
        grid=(npair, nj),
        in_specs=in_specs,
        out_specs=pl.BlockSpec((1, FFT_NB, 2 * FFT_N1, HY_WIDTH), lambda p, j: (p, j, 0, 0)),
        out_shape=jax.ShapeDtypeStruct((npair, FFT_N1, 2 * FFT_N1, HY_WIDTH), BF16),
        compiler_params=_cp("arbitrary", "arbitrary"),
        name="hy_fft2",
    )(av, av, f2, f2c, hspec)


def _fft3_kernel(br_ref, bi_ref, m_ref, vg_ref, x0_ref, bias_ref, y_ref):
    half = FFT_N1 // 2
    for t in range(FFT_NB):
        sl = slice(t * HY_WIDTH, (t + 1) * HY_WIDTH)
        rhs = jnp.concatenate([br_ref[0][:, sl], bi_ref[0][:, sl]], axis=0)
        out = _dot(m_ref[t], rhs)
        for s in range(2):
            conv = out[s * half:(s + 1) * half]
            y_ref[s, :, sl] = (conv + vg_ref[s][:, sl] * bias_ref[...]) * x0_ref[s][:, sl]


def _fft3_call(bmat, m3, vg, x0, bias):
    npair = bmat.shape[0]
    bv = bmat.reshape(npair, FFT_N1, 2 * FFT_N1 * HY_WIDTH)
    nj = FFT_N1 // FFT_NB
    w = FFT_NB * HY_WIDTH
    half = FFT_N1 // 2
    bsz, L, _ = vg.shape
    view = lambda t: t.reshape(bsz, half, FFT_N1 * HY_WIDTH)
    tok_spec = pl.BlockSpec((2, half, w), lambda p, j: (p, 0, j))
    y = pl.pallas_call(
        _fft3_kernel,
        grid=(npair, nj),
        in_specs=[pl.BlockSpec((1, FFT_N1, w), lambda p, j: (p, 0, j)),
                  pl.BlockSpec((1, FFT_N1, w), lambda p, j: (p, 0, nj + j)),
                  pl.BlockSpec((FFT_NB, FFT_N1, 2 * FFT_N1), lambda p, j: (j, 0, 0)),
                  tok_spec, tok_spec,
                  pl.BlockSpec((1, HY_WIDTH), lambda p, j: (0, 0))],
        out_specs=tok_spec,
        out_shape=jax.ShapeDtypeStruct((bsz, half, FFT_N1 * HY_WIDTH), F32),
        compiler_params=_cp("arbitrary", "arbitrary"),
        name="hy_fft3",
    )(bv, bv, m3, view(vg), view(x0), bias)
    return y.reshape(bsz, L, HY_WIDTH)


def _hy_direct_kernel(vg_ref, x0_ref, h_ref, bias_ref, y_ref, pad_scr, sh_scr):
    L = vg_ref.shape[1]
    u = vg_ref[0]
    pad_scr[...] = jnp.zeros_like(pad_scr)
    pad_scr[L:2 * L, :] = u
    top = L + L // 2
    acc = jnp.zeros((L, HY_WIDTH), F32)
    for r in range(8):
        sh_scr[...] = pad_scr[r:r + 3 * L - 8, :]
        a_lo = -(-(top - L + 1 - r) // 8)
        a_hi = (top - r) // 8

        def body(a, acc, r=r):
            m = top - (a * 8 + r)
            return acc + h_ref[pl.ds(m, 1), :] * sh_scr[pl.ds(pl.multiple_of(a * 8, 8), L), :]

        acc = lax.fori_loop(a_lo, a_hi + 1, body, acc)
    y_ref[0] = (acc + u * bias_ref[...]) * x0_ref[0]


def _hy_direct_call(vg, x0, h, bias):
    bsz, L, _ = vg.shape
    tok = pl.BlockSpec((1, L, HY_WIDTH), lambda b: (b, 0, 0))
    return pl.pallas_call(
        _hy_direct_kernel,
        grid=(bsz,),
        in_specs=[tok, tok, pl.BlockSpec((L, HY_WIDTH), lambda b: (0, 0)),
                  pl.BlockSpec((1, HY_WIDTH), lambda b: (0, 0))],
        out_specs=tok,
        out_shape=jax.ShapeDtypeStruct((bsz, L, HY_WIDTH), F32),
        scratch_shapes=[pltpu.VMEM((3 * L, HY_WIDTH), F32), pltpu.VMEM((3 * L - 8, HY_WIDTH), F32)],
        compiler_params=_cp("arbitrary"),
        name="hy_direct",
    )(vg, x0, h, bias)


def _out_kernel(x_ref, yg_ref, yh_ref, ys_ref, mod_ref, g2_ref, gf_ref, wo_ref, w1_ref, w3_ref, w2_ref,
                o_ref, h_scr, acc_scr, *, final, nff):
    j = pl.program_id(1)
    m = mod_ref[0]

    @pl.when(j == 0)
    def _():
        mix = (_dot(_bf(yg_ref[...]), wo_ref[0:GLA_V])
               + _dot(_bf(yh_ref[...]), wo_ref[GLA_V:GLA_V + HY_WIDTH])
               + _dot(_bf(ys_ref[...]), wo_ref[GLA_V + HY_WIDTH:]))
        x1 = x_ref[...] + m[2:3] * mix
        o_ref[...] = x1
        ms = jnp.mean(x1 * x1, axis=-1, keepdims=True)
        h = x1 * lax.rsqrt(ms + EPS) * g2_ref[...]
        h_scr[...] = _bf(h * (1.0 + m[4:5]) + m[3:4])
        acc_scr[...] = jnp.zeros_like(acc_scr)

    h = h_scr[...]
    a = _dot(h, w1_ref[...])
    b = _dot(h, w3_ref[...])
    acc_scr[...] += _dot(_bf(_silu(a) * b), w2_ref[...])

    @pl.when(j == nff - 1)
    def _():
        x2 = o_ref[...] + m[5:6] * acc_scr[...]
        if final:
            ms = jnp.mean(x2 * x2, axis=-1, keepdims=True)
            x2 = x2 * lax.rsqrt(ms + EPS) * gf_ref[...]
        o_ref[...] = x2


def _out_call(x2, yg, yh, ys, mods, mod_row, g2, gf, wo, w1, w3, w2, final):
    t = x2.shape[0]
    tm = 512
    nff = D_FF // FF_TILE
    c2 = lambda i, j: (0, 0)
    return pl.pallas_call(
        functools.partial(_out_kernel, final=final, nff=nff),
        grid=(t // tm, nff),
        in_specs=[pl.BlockSpec((tm, D_MODEL), lambda i, j: (i, 0)),
                  pl.BlockSpec((tm, GLA_V), lambda i, j: (i, 0)),
                  pl.BlockSpec((tm, HY_WIDTH), lambda i, j: (i, 0)),
                  pl.BlockSpec((tm, SSD_INNER), lambda i, j: (i, 0)),
                  pl.BlockSpec((1, 6, D_MODEL), lambda i, j: (mod_row(i), 0, 0)),
                  pl.BlockSpec((1, D_MODEL), c2),
                  pl.BlockSpec((1, D_MODEL), c2),
                  pl.BlockSpec((D_MODEL, D_MODEL), c2),
                  pl.BlockSpec((D_MODEL, FF_TILE), lambda i, j: (0, j)),
                  pl.BlockSpec((D_MODEL, FF_TILE), lambda i, j: (0, j)),
                  pl.BlockSpec((FF_TILE, D_MODEL), lambda i, j: (j, 0))],
        out_specs=pl.BlockSpec((tm, D_MODEL), lambda i, j: (i, 0)),
        out_shape=jax.ShapeDtypeStruct((t, D_MODEL), F32),
        scratch_shapes=[pltpu.VMEM((tm, D_MODEL), BF16), pltpu.VMEM((tm, D_MODEL), F32)],
        compiler_params=_cp("arbitrary", "arbitrary"),
        name="out_ffn",
    )(x2, yg, yh, ys, mods, g2.reshape(1, D_MODEL), gf.reshape(1, D_MODEL), wo, w1, w3, w2)


def _pack_w_in(w_in):
    z = lambda n: jnp.zeros(w_in.shape[:2] + (n,), w_in.dtype)
    q, k = w_in[..., 0:192], w_in[..., 192:384]
    v, g = w_in[..., 384:768], w_in[..., 768:1152]
    gk = w_in[..., 1152:1184]
    hy = w_in[..., 1184:1952]
    zz = w_in[..., 1952:2336]
    xbc = w_in[..., 2336:3232]
    dt = w_in[..., 3232:3244]
    packed = jnp.concatenate([q, z(64), k, z(64), v, g, gk, z(96), hy, zz, xbc, dt, z(116)], axis=-1)
    return packed.astype(BF16)


def _pad_to(a, shape):
    return jnp.pad(a, [(0, s - d) for d, s in zip(a.shape, shape)])


def _mixers(pg, ph, ps, lw, states, col_major):
    g_f0, g_b0, m_f0, m_b0 = states
    o_f, g_f = _gla_call(pg, None, lw['wgk_f'], lw['bgk_f'], None, g_f0, False)
    gla_y, g_b = _gla_call(pg, o_f, lw['wgk_b'], lw['bgk_b'], lw['gla_nw'], g_b0, True)

    vg, x0 = _hy_pre_call(ph, lw['hy_sw'], lw['hy_sb'])
    L = ph.shape[1]
    h = _hy_filter_call(L, lw['hy_w1'], lw['hy_b1'], lw['hy_w2'], lw['hy_b2'], lw['hy_w3'], lw['hy_freq'],
                        lw['hy_decay'])
    if L == FFT_N // 2:
        m1, f2, f2c, m3 = lw['fft']
        hspec = _fft2_call(_fft1_call(jnp.stack([h, jnp.zeros_like(h)]), m1), f2, f2c, None)
        bmat = _fft2_call(_fft1_call(vg, m1), f2, f2c, hspec)
        hy_y = _fft3_call(bmat, m3, vg, x0, lw['hy_bias'])
    else:
        hy_y = _hy_direct_call(vg, x0, h, lw['hy_bias'])

    ssd_args = (lw['ssd_cw'], lw['ssd_cb'], lw['ssd_dtb'], lw['ssd_alog'])
    y_f, m_f = _ssd_call(ps, None, *ssd_args, None, None, m_f0, False, col_major)
    ssd_y, m_b = _ssd_call(ps, y_f, *ssd_args, lw['ssd_dx'], lw['ssd_nw'], m_b0, True, col_major)
    return (gla_y, hy_y, ssd_y), (g_f, g_b, m_f, m_b)


def kernel(x, c, ctx, c_ctx, mod_w, mod_b, norm1_g, norm2_g, w_in, gla_gk_w_f, gla_gk_b_f, gla_gk_w_b, gla_gk_b_b, gla_norm_w, hy_short_w, hy_short_b, hy_w1, hy_b1, hy_w2, hy_b2, hy_w3, hy_freq, hy_decay, hy_bias, ssd_conv_w, ssd_conv_b, ssd_dt_bias_f, ssd_dt_bias_b, ssd_a_log_f, ssd_a_log_b, ssd_d, ssd_norm_w, w_out, ffn_w1, ffn_w3, ffn_w2, final_g):
    bsz, seq, _ = x.shape
    lc = ctx.shape[1]
    c8 = jnp.concatenate([c, c_ctx[None], jnp.zeros((8 - bsz - 1, D_MODEL), F32)], axis=0)
    mods_all = _mod_call(c8, mod_w, mod_b).reshape(DEPTH, 8, 6, D_MODEL)

    w_in_p = _pack_w_in(w_in)
    fft_mats = tuple(_bf(jnp.asarray(m)) for m in _fft_consts())
    w_out_b, w1_b, w3_b, w2_b = _bf(w_out), _bf(ffn_w1), _bf(ffn_w3), _bf(ffn_w2)

    xt = x.reshape(bsz * seq, D_MODEL)
    ct = ctx.reshape(bsz * lc, D_MODEL)
    tiles_per_seq = seq // 512
    row_x = lambda i: i // tiles_per_seq
    row_c = lambda i: bsz

    zeros_states = (jnp.zeros((bsz, GLA_DV, 256), F32), jnp.zeros((bsz, GLA_DV, 256), F32),
                    jnp.zeros((bsz, 2 * SSD_STATE, SSD_INNER), F32),
                    jnp.zeros((bsz, 2 * SSD_STATE, SSD_INNER), F32))

    for l in range(DEPTH):
        def gkw(w, off):
            return _bf(_pad_to(jnp.pad(w, ((off, 0), (0, 0))), (LANE, 256)))
        lw = {
            'wgk_f': gkw(gla_gk_w_f[l], 0), 'wgk_b': gkw(gla_gk_w_b[l], GLA_LOWRANK),
            'bgk_f': _pad_to(gla_gk_b_f[l][None], (1, 256)), 'bgk_b': _pad_to(gla_gk_b_b[l][None], (1, 256)),
            'gla_nw': jnp.tile(gla_norm_w[l], GLA_HEADS)[None],
            'hy_sw': _pad_to(hy_short_w[l], (8, PH_W)), 'hy_sb': hy_short_b[l][None],
            'hy_w1': _pad_to(hy_w1[l], (LANE, LANE)), 'hy_b1': _pad_to(hy_b1[l][None], (1, LANE)),
            'hy_w2': _pad_to(hy_w2[l], (LANE, LANE)), 'hy_b2': _pad_to(hy_b2[l][None], (1, LANE)),
            'hy_w3': _pad_to(hy_w3[l], (LANE, HY_WIDTH)), 'hy_freq': _pad_to(hy_freq[l][None], (1, LANE)),
            'hy_decay': hy_decay[l][None], 'hy_bias': hy_bias[l][None],
            'ssd_cw': _pad_to(ssd_conv_w[l], (8, SSD_CONV_DIM)), 'ssd_cb': ssd_conv_b[l][None],
            'ssd_dtb': _pad_to(jnp.concatenate([ssd_dt_bias_f[l], ssd_dt_bias_b[l]])[None], (1, LANE)),
            'ssd_alog': _pad_to(jnp.concatenate([ssd_a_log_f[l], ssd_a_log_b[l]])[None], (1, LANE)),
            'ssd_dx': jnp.repeat(ssd_d[l], SSD_P)[None], 'ssd_nw': ssd_norm_w[l][None],
        }
        lw['fft'] = fft_mats
        mods = mods_all[l]
        cg, ch, cs_ = _in_call(ct, mods, row_c, norm1_g[l], w_in_p[l])
        yc, ctx_states = _mixers(cg.reshape(bsz, lc, PG_W), ch.reshape(bsz, lc, PH_W),
                                 cs_.reshape(bsz, lc, PS_W), lw, zeros_states, False)
        xg, xh, xs_ = _in_call(xt, mods, row_x, norm1_g[l], w_in_p[l])
        yx, _ = _mixers(xg.reshape(bsz, seq, PG_W), xh.reshape(bsz, seq, PH_W),
                        xs_.reshape(bsz, seq, PS_W), lw, ctx_states, True)
        last = l == DEPTH - 1
        ffn = (w_out_b[l], w1_b[l], w3_b[l], w2_b[l])
        xt = _out_call(xt, yx[0].reshape(-1, GLA_V), yx[1].reshape(-1, HY_WIDTH), yx[2].reshape(-1, SSD_INNER),
                       mods, row_x, norm2_g[l], final_g, *ffn, last)
        if not last:
            ct = _out_call(ct, yc[0].reshape(-1, GLA_V), yc[1].reshape(-1, HY_WIDTH),
                           yc[2].reshape(-1, SSD_INNER), mods, row_c, norm2_g[l], final_g, *ffn, False)
    return xt.reshape(bsz, seq, D_MODEL)
```

```python
import functools
import math

import numpy as np
import jax
import jax.numpy as jnp
from jax import lax
from jax.experimental import pallas as pl
from jax.experimental.pallas import tpu as pltpu

F32 = jnp.float32
BF16 = jnp.bfloat16

D_MODEL = 1024
DEPTH = 2
GRID_W = 64
EPS = 1e-6
GLA_V = 384
GLA_DV = 64
GLA_HEADS = 6
GLA_DK = 32
GLA_QK = 192
GLA_LOWRANK = 16
GLA_TAU = 16.0
GLA_CHUNK = 64
GLA_BLOCK = 256
HY_WIDTH = 256
HY_BANDS = 16
HY_EMB = 1 + 2 * HY_BANDS
HY_ORDER = 64
SSD_INNER = 384
SSD_HEADS = 6
SSD_GROUPS = 2
SSD_HG = 3
SSD_P = 64
SSD_STATE = 128
SSD_CONV_DIM = SSD_INNER + 2 * SSD_GROUPS * SSD_STATE
SSD_CHUNK = 128
D_FF = 2816
FF_TILE = 1408
PG_W = 1408
PH_W = 768
PS_W = 1408
LANE = 128
FFT_N1 = 128
FFT_N = FFT_N1 * FFT_N1

VMEM_LIMIT = 56 * 1024 * 1024


def _cp(*sem):
    return pltpu.CompilerParams(dimension_semantics=sem, vmem_limit_bytes=VMEM_LIMIT)


def _bf(x):
    return x.astype(BF16)


def _dot(a, b):
    return jnp.dot(a, b, preferred_element_type=F32)


def _dot_nt(a, b):
    return lax.dot_general(a, b, (((1,), (1,)), ((), ())), preferred_element_type=F32)


def _dot_tn(a, b):
    return lax.dot_general(a, b, (((0,), (0,)), ((), ())), preferred_element_type=F32)


def _split3(x):
    hi = _bf(x)
    r1 = x - hi.astype(F32)
    mid = _bf(r1)
    lo = _bf(r1 - mid.astype(F32))
    return hi, mid, lo


def _dot_sel(sel_bf, x):
    hi, mid, lo = _split3(x)
    return _dot(sel_bf, hi) + _dot(sel_bf, mid) + _dot(sel_bf, lo)


def _dot_hp(a, b):
    ah = _bf(a)
    al = _bf(a - ah.astype(F32))
    bh = _bf(b)
    bl = _bf(b - bh.astype(F32))
    return _dot(ah, bh) + _dot(ah, bl) + _dot(al, bh)


def _silu(x):
    return x * jax.nn.sigmoid(x)


def _softplus(x):
    return jnp.maximum(x, 0.0) + jnp.log1p(jnp.exp(-jnp.abs(x)))


def _log_sigmoid(x):
    return jnp.minimum(x, 0.0) - jnp.log1p(jnp.exp(-jnp.abs(x)))


def _mod_kernel(c_ref, w_ref, b_ref, o_ref):
    act = _silu(c_ref[...])
    o_ref[0] = _dot(_bf(act), _bf(w_ref[0])) + b_ref[0]


def _mod_call(c8, mod_w, mod_b):
    nt = 1536
    n = mod_w.shape[-1]
    return pl.pallas_call(
        _mod_kernel,
        grid=(DEPTH, n // nt),
        in_specs=[pl.BlockSpec((8, D_MODEL), lambda l, j: (0, 0)),
                  pl.BlockSpec((1, D_MODEL, nt), lambda l, j: (l, 0, j)),
                  pl.BlockSpec((1, 1, nt), lambda l, j: (l, 0, j))],
        out_specs=pl.BlockSpec((1, 8, nt), lambda l, j: (l, 0, j)),
        out_shape=jax.ShapeDtypeStruct((DEPTH, 8, n), F32),
        compiler_params=_cp("arbitrary", "arbitrary"),
        name="adaln_mod",
    )(c8, mod_w, mod_b.reshape(DEPTH, 1, n))


TOK_TILE = 512
ROWS_PER_TILE = TOK_TILE // GRID_W


def _in_kernel(x_ref, mod_ref, g_ref, w_ref, og_ref, oh_ref, os_ref, *scr, col_major):
    x = x_ref[...]
    ms = jnp.mean(x * x, axis=-1, keepdims=True)
    y = x * lax.rsqrt(ms + EPS) * g_ref[...]
    m = mod_ref[0]
    h = _bf(y * (1.0 + m[1:2]) + m[0:1])
    og_ref[...] = _dot(h, w_ref[:, 0:PG_W])
    oh_ref[...] = _dot(h, w_ref[:, PG_W:PG_W + PH_W])
    ps = _dot(h, w_ref[:, PG_W + PH_W:])
    if col_major:
        ps_scr, = scr
        for k in range(PS_W // LANE):
            for r in range(ROWS_PER_TILE):
                ps_scr[k, pl.ds(r, GRID_W, stride=ROWS_PER_TILE), :] = ps[r * GRID_W:(r + 1) * GRID_W,
                                                                         k * LANE:(k + 1) * LANE]
        for k in range(PS_W // LANE):
            os_ref[0, :, :, k * LANE:(k + 1) * LANE] = ps_scr[k].reshape(GRID_W, ROWS_PER_TILE, LANE)
    else:
        os_ref[...] = ps


def _in_call(x2, mods, mod_row, g, w, bsz, col_major):
    t = x2.shape[0]
    tm = TOK_TILE
    wtot = PG_W + PH_W + PS_W
    seq = t // bsz
    if col_major:
        tiles = seq // tm
        assert seq == GRID_W * SSD_CHUNK and ROWS_PER_TILE == 8
        os_spec = pl.BlockSpec((1, GRID_W, ROWS_PER_TILE, PS_W), lambda i: (i // tiles, 0, i % tiles, 0))
        os_shape = jax.ShapeDtypeStruct((bsz, GRID_W, SSD_CHUNK, PS_W), F32)
        scratch = [pltpu.VMEM((PS_W // LANE, tm, LANE), F32)]
    else:
        os_spec = pl.BlockSpec((tm, PS_W), lambda i: (i, 0))
        os_shape = jax.ShapeDtypeStruct((t, PS_W), F32)
        scratch = []
    pg, ph, ps = pl.pallas_call(
        functools.partial(_in_kernel, col_major=col_major),
        grid=(t // tm,),
        in_specs=[pl.BlockSpec((tm, D_MODEL), lambda i: (i, 0)),
                  pl.BlockSpec((1, 6, D_MODEL), lambda i: (mod_row(i), 0, 0)),
                  pl.BlockSpec((1, D_MODEL), lambda i: (0, 0)),
                  pl.BlockSpec((D_MODEL, wtot), lambda i: (0, 0))],
        out_specs=[pl.BlockSpec((tm, PG_W), lambda i: (i, 0)),
                   pl.BlockSpec((tm, PH_W), lambda i: (i, 0)),
                   os_spec],
        out_shape=[jax.ShapeDtypeStruct((t, PG_W), F32),
                   jax.ShapeDtypeStruct((t, PH_W), F32),
                   os_shape],
        scratch_shapes=scratch,
        compiler_params=_cp("arbitrary"),
        name="in_proj",
    )(x2, mods, g.reshape(1, D_MODEL), w)
    return (pg.reshape(bsz, seq, PG_W), ph.reshape(bsz, seq, PH_W),
            ps.reshape(bsz, seq // SSD_CHUNK, SSD_CHUNK, PS_W))


def _gla_kernel(*refs, reverse, nblk):
    if reverse:
        (p_ref, of_ref, wgk_ref, bgk_ref, nw_ref, s0_ref, y_ref, sf_ref, st_scr) = refs
    else:
        (p_ref, wgk_ref, bgk_ref, s0_ref, of_ref, sf_ref, st_scr) = refs
    i = pl.program_id(1)
    tb = GLA_BLOCK
    nch = tb // GLA_CHUNK

    @pl.when(i == 0)
    def _():
        st_scr[...] = s0_ref[0]

    p = p_ref[0]
    q = p[:, 0:256]
    k = p[:, 256:512]
    v = p[:, 512:896]
    gk = p[:, 1280:1408]
    pre = _dot(_bf(gk), wgk_ref[...]) + bgk_ref[...]
    la = _log_sigmoid(pre) * (1.0 / GLA_TAU)

    ri = lax.broadcasted_iota(jnp.int32, (tb, tb), 0)
    ci = lax.broadcasted_iota(jnp.int32, (tb, tb), 1)
    same = (ri // GLA_CHUNK) == (ci // GLA_CHUNK)
    tri = same & ((ci >= ri) if reverse else (ci <= ri))
    b = _dot_sel(jnp.where(tri, 1.0, 0.0).astype(BF16), la)

    qd = _bf(q * ((GLA_DK ** -0.5) * jnp.exp(b)))
    ki = _bf(k * jnp.exp(-b))
    vb = _bf(v)

    lane = lax.broadcasted_iota(jnp.int32, (1, 256), 1)
    hms = [(lane // GLA_DK) == h for h in range(GLA_HEADS)]

    lhs = jnp.concatenate([jnp.where(hm, qd, jnp.zeros_like(qd)) for hm in hms], axis=0)
    s = _dot_nt(lhs, ki)
    pm = jnp.concatenate([_bf(jnp.where(tri, s[h * tb:(h + 1) * tb], 0.0)) for h in range(GLA_HEADS)], axis=0)
    ra = _dot(pm[:4 * tb], vb[:, 0:256])
    rb = _dot(pm[4 * tb:], vb[:, 256:384])
    hl = lane // GLA_DV
    oa = jnp.zeros((tb, 256), F32)
    for h in range(4):
        oa = oa + jnp.where(hl == h, ra[h * tb:(h + 1) * tb], 0.0)
    ob = jnp.where(hl[:, :LANE] == 0, rb[:tb], rb[tb:])
    o_intra = jnp.concatenate([oa, ob], axis=1)

    st = st_scr[...]
    outs = [None] * nch
    order = range(nch - 1, -1, -1) if reverse else range(nch)
    for c in order:
        r0 = c * GLA_CHUNK
        bc = b[r0:r0 + GLA_CHUNK]
        edge = r0 if reverse else r0 + GLA_CHUNK - 1
        bl = b[edge:edge + 1]
        kd = _bf(k[r0:r0 + GLA_CHUNK] * jnp.exp(bl - bc))
        stbd = _bf(jnp.concatenate([jnp.where(hm, st, 0.0) for hm in hms], axis=0))
        outs[c] = _dot_nt(qd[r0:r0 + GLA_CHUNK], stbd)
        full = _dot_tn(vb[r0:r0 + GLA_CHUNK], kd)
        ds = jnp.zeros((GLA_DV, 256), F32)
        for h in range(GLA_HEADS):
            ds = ds + jnp.where(hms[h], full[h * GLA_DV:(h + 1) * GLA_DV], 0.0)
        st = jnp.exp(bl) * st + ds
    st_scr[...] = st
    o = o_intra + jnp.concatenate(outs, axis=0)

    @pl.when(i == nblk - 1)
    def _():
        sf_ref[0] = st

    if not reverse:
        of_ref[0] = o
    else:
        ot = of_ref[0] + o
        r2 = lax.broadcasted_iota(jnp.int32, (GLA_V, GLA_V), 0) // GLA_DV
        c2 = lax.broadcasted_iota(jnp.int32, (GLA_V, GLA_V), 1) // GLA_DV
        ind = jnp.where(r2 == c2, 1.0, 0.0).astype(BF16)
        sq = ot * ot
        sh = _bf(sq)
        sl = _bf(sq - sh.astype(F32))
        ms = (_dot(sh, ind) + _dot(sl, ind)) * (1.0 / GLA_DV)
        g = p[:, 896:1280]
        y_ref[0] = ot * lax.rsqrt(ms + EPS) * nw_ref[...] * _silu(g)


def _gla_call(pg, o_f, wgk, bgk, nw, s0, reverse):
    bsz, L, _ = pg.shape
    nblk = L // GLA_BLOCK
    blk = (lambda b, i: (b, nblk - 1 - i, 0)) if reverse else (lambda b, i: (b, i, 0))
    const2 = lambda b, i: (0, 0)
    st_spec = pl.BlockSpec((1, GLA_DV, 256), lambda b, i: (b, 0, 0))
    p_spec = pl.BlockSpec((1, GLA_BLOCK, PG_W), blk)
    o_spec = pl.BlockSpec((1, GLA_BLOCK, GLA_V), blk)
    if reverse:
        in_specs = [p_spec, o_spec, pl.BlockSpec((LANE, 256), const2), pl.BlockSpec((1, 256), const2),
                    pl.BlockSpec((1, GLA_V), const2), st_spec]
        args = (pg, o_f, wgk, bgk, nw, s0)
    else:
        in_specs = [p_spec, pl.BlockSpec((LANE, 256), const2), pl.BlockSpec((1, 256), const2), st_spec]
        args = (pg, wgk, bgk, s0)
    return pl.pallas_call(
        functools.partial(_gla_kernel, reverse=reverse, nblk=nblk),
        grid=(bsz, nblk),
        in_specs=in_specs,
        out_specs=[o_spec, st_spec],
        out_shape=[jax.ShapeDtypeStruct((bsz, L, GLA_V), F32),
                   jax.ShapeDtypeStruct((bsz, GLA_DV, 256), F32)],
        scratch_shapes=[pltpu.VMEM((GLA_DV, 256), F32)],
        compiler_params=_cp("arbitrary", "arbitrary"),
        name="gla_bwd" if reverse else "gla_fwd",
    )(*args)


def _expand_heads(t, lo):
    r = t.shape[0]
    lane = lax.broadcasted_iota(jnp.int32, (1, LANE), 1)
    tiles = []
    for j in range(SSD_HEADS // 2):
        a = jnp.broadcast_to(t[:, lo + 2 * j:lo + 2 * j + 1], (r, LANE))
        b = jnp.broadcast_to(t[:, lo + 2 * j + 1:lo + 2 * j + 2], (r, LANE))
        tiles.append(jnp.where(lane < SSD_P, a, b))
    return jnp.concatenate(tiles, axis=1)


def _ssd_kernel(*refs, reverse, nchunk):
    if reverse:
        (cur_ref, prev_ref, next_ref, yf_ref, cw_ref, cb_ref, dtb_ref, alog_ref, dx_ref, nw_ref, s0_ref,
         y_ref, sf_ref, st_scr) = refs
    else:
        (cur_ref, prev_ref, next_ref, cw_ref, cb_ref, dtb_ref, alog_ref, s0_ref,
         yf_ref, sf_ref, st_scr) = refs
    i = pl.program_id(1)
    c = (nchunk - 1 - i) if reverse else i
    q_ = SSD_CHUNK
    lo = SSD_HEADS if reverse else 0

    @pl.when(i == 0)
    def _():
        st_scr[...] = s0_ref[0]

    cur = cur_ref[0, 0]
    x = cur[:, 384:1280]
    has_prev = (c > 0).astype(F32)
    has_next = (c < nchunk - 1).astype(F32)
    prow = prev_ref[0, 0][7:8, 384:1280] * has_prev
    nrow = next_ref[0, 0][0:1, 384:1280] * has_next
    ridx = lax.broadcasted_iota(jnp.int32, (q_, 1), 0)
    xm = jnp.where(ridx == 0, prow, pltpu.roll(x, 1, 0))
    xp = jnp.where(ridx == q_ - 1, nrow, pltpu.roll(x, q_ - 1, 0))
    conv = cw_ref[0:1] * xm + cw_ref[1:2] * x + cw_ref[2:3] * xp + cb_ref[...]
    act = _silu(conv)
    xs = act[:, 0:384]
    bm = act[:, 384:640]
    cm = act[:, 640:896]
    dtt = _softplus(cur[:, 1280:1408] + dtb_ref[...])
    a = -jnp.exp(alog_ref[...]) * dtt

    ri = lax.broadcasted_iota(jnp.int32, (q_, q_), 0)
    ci = lax.broadcasted_iota(jnp.int32, (q_, q_), 1)
    tri = (ci >= ri) if reverse else (ci <= ri)
    cs = _dot_sel(jnp.where(tri, 1.0, 0.0).astype(BF16), a)
    cst = cs.T
    dtT = dtt.T
    edge = 0 if reverse else q_ - 1
    cs_last = cs[edge:edge + 1]

    bmb = _bf(bm)
    cmb = _bf(cm)
    cbs = [_dot_nt(cmb[:, g * SSD_STATE:(g + 1) * SSD_STATE], bmb[:, g * SSD_STATE:(g + 1) * SSD_STATE])
           for g in range(SSD_GROUPS)]
    ms = []
    for h in range(SSD_HEADS):
        l = lo + h
        seg = cs[:, l:l + 1] - cst[l:l + 1, :]
        dec = jnp.exp(jnp.where(tri, seg, -jnp.inf))
        ms.append(_bf(cbs[h // SSD_HG] * dec * dtT[l:l + 1, :]))
    mst = jnp.concatenate(ms, axis=0)
    xsb = _bf(xs)
    ra = _dot(mst[:4 * q_], xsb[:, 0:256])
    rb = _dot(mst[4 * q_:], xsb[:, 256:384])
    lane = lax.broadcasted_iota(jnp.int32, (1, 256), 1)
    hl = lane // SSD_P
    ya = jnp.zeros((q_, 256), F32)
    for h in range(4):
        ya = ya + jnp.where(hl == h, ra[h * q_:(h + 1) * q_], 0.0)
    yb = jnp.where(hl[:, :LANE] == 0, rb[:q_], rb[q_:])
    y = jnp.concatenate([ya, yb], axis=1)

    st = st_scr[...]
    cs_x = _expand_heads(cs, lo)
    csl_x = _expand_heads(cs_last, lo)
    y = y + jnp.exp(cs_x) * _dot(cmb, _bf(st))
    xw = _bf(xs * (jnp.exp(csl_x - cs_x) * _expand_heads(dtt, lo)))
    full = _dot_tn(bmb, xw)
    r2 = lax.broadcasted_iota(jnp.int32, (2 * SSD_STATE, SSD_INNER), 0) // SSD_STATE
    c2 = lax.broadcasted_iota(jnp.int32, (2 * SSD_STATE, SSD_INNER), 1) // (SSD_HG * SSD_P)
    st = jnp.exp(csl_x) * st + jnp.where(r2 == c2, full, 0.0)
    st_scr[...] = st

    @pl.when(i == nchunk - 1)
    def _():
        sf_ref[0] = st

    if not reverse:
        yf_ref[0, 0] = y
    else:
        yt = yf_ref[0, 0] + y + dx_ref[...] * xs
        yz = yt * _silu(cur[:, 0:384])
        l384 = lax.broadcasted_iota(jnp.int32, (1, SSD_INNER), 1)
        g0 = l384 < (SSD_INNER // SSD_GROUPS)
        sq = yz * yz
        m0 = jnp.sum(jnp.where(g0, sq, 0.0), axis=-1, keepdims=True)
        m1 = jnp.sum(jnp.where(g0, 0.0, sq), axis=-1, keepdims=True)
        msq = jnp.where(g0, m0, m1) * (1.0 / (SSD_INNER // SSD_GROUPS))
        y_ref[0, 0] = yz * lax.rsqrt(msq + EPS) * nw_ref[...]


def _ssd_call(ps, y_f, cw, cb, dtb, alog, dx, nw, s0, reverse):
    bsz, nchunk, _, _ = ps.shape
    src = ps
    cidx = (lambda i: nchunk - 1 - i) if reverse else (lambda i: i)
    cur_map = lambda b, i: (b, cidx(i), 0, 0)
    prev_map = lambda b, i: (b, jnp.maximum(cidx(i) - 1, 0), SSD_CHUNK // 8 - 1, 0)
    next_map = lambda b, i: (b, jnp.minimum(cidx(i) + 1, nchunk - 1), 0, 0)
    const2 = lambda b, i: (0, 0)
    cur_spec = pl.BlockSpec((1, 1, SSD_CHUNK, PS_W), cur_map)
    prev_spec = pl.BlockSpec((1, 1, 8, PS_W), prev_map)
    next_spec = pl.BlockSpec((1, 1, 8, PS_W), next_map)
    yf_spec = pl.BlockSpec((1, 1, SSD_CHUNK, SSD_INNER), cur_map)
    st_spec = pl.BlockSpec((1, 2 * SSD_STATE, SSD_INNER), lambda b, i: (b, 0, 0))
    w_specs = [pl.BlockSpec((8, SSD_CONV_DIM), const2), pl.BlockSpec((1, SSD_CONV_DIM), const2),
               pl.BlockSpec((1, LANE), const2), pl.BlockSpec((1, LANE), const2)]
    if reverse:
        in_specs = [cur_spec, prev_spec, next_spec, yf_spec] + w_specs + [
            pl.BlockSpec((1, SSD_INNER), const2), pl.BlockSpec((1, SSD_INNER), const2), st_spec]
        args = (src, src, src, y_f, cw, cb, dtb, alog, dx, nw, s0)
        out_specs = [yf_spec, st_spec]
        out_shape = [jax.ShapeDtypeStruct((bsz, nchunk, SSD_CHUNK, SSD_INNER), F32),
                     jax.ShapeDtypeStruct((bsz, 2 * SSD_STATE, SSD_INNER), F32)]
    else:
        in_specs = [cur_spec, prev_spec, next_spec] + w_specs + [st_spec]
        args = (src, src, src, cw, cb, dtb, alog, s0)
        out_specs = [yf_spec, st_spec]
        out_shape = [jax.ShapeDtypeStruct((bsz, nchunk, SSD_CHUNK, SSD_INNER), F32),
                     jax.ShapeDtypeStruct((bsz, 2 * SSD_STATE, SSD_INNER), F32)]
    y, sf = pl.pallas_call(
        functools.partial(_ssd_kernel, reverse=reverse, nchunk=nchunk),
        grid=(bsz, nchunk),
        in_specs=in_specs,
        out_specs=out_specs,
        out_shape=out_shape,
        scratch_shapes=[pltpu.VMEM((2 * SSD_STATE, SSD_INNER), F32)],
        compiler_params=_cp("arbitrary", "arbitrary"),
        name="ssd_bwd" if reverse else "ssd_fwd",
    )(*args)
    return y, sf


def _hy_pre_kernel(cur_ref, prev_ref, next_ref, w_ref, b_ref, vg_ref, x0_ref, *, nblk):
    i = pl.program_id(1)
    x = cur_ref[0]
    tb = x.shape[0]
    has_prev = (i > 0).astype(F32)
    has_next = (i < nblk - 1).astype(F32)
    prow = prev_ref[0][7:8] * has_prev
    nrow = next_ref[0][0:1] * has_next
    ridx = lax.broadcasted_iota(jnp.int32, (tb, 1), 0)
    xm = jnp.where(ridx == 0, prow, pltpu.roll(x, 1, 0))
    xp = jnp.where(ridx == tb - 1, nrow, pltpu.roll(x, tb - 1, 0))
    u = w_ref[0:1] * xm + w_ref[1:2] * x + w_ref[2:3] * xp + b_ref[...]
    vg = u[:, 2 * HY_WIDTH:] * u[:, HY_WIDTH:2 * HY_WIDTH]
    for hf in range(HY_WIDTH // LANE):
        x0_ref[0, hf] = u[:, hf * LANE:(hf + 1) * LANE]
        vg_ref[0, hf] = vg[:, hf * LANE:(hf + 1) * LANE]


def _hy_pre_call(ph, w8, b):
    bsz, L, _ = ph.shape
    tb = min(512, L)
    nblk = L // tb
    r8 = tb // 8
    return pl.pallas_call(
        functools.partial(_hy_pre_kernel, nblk=nblk),
        grid=(bsz, nblk),
        in_specs=[pl.BlockSpec((1, tb, PH_W), lambda b_, i: (b_, i, 0)),
                  pl.BlockSpec((1, 8, PH_W), lambda b_, i: (b_, jnp.maximum(i * r8 - 1, 0), 0)),
                  pl.BlockSpec((1, 8, PH_W), lambda b_, i: (b_, jnp.minimum((i + 1) * r8, L // 8 - 1), 0)),
                  pl.BlockSpec((8, PH_W), lambda b_, i: (0, 0)),
                  pl.BlockSpec((1, PH_W), lambda b_, i: (0, 0))],
        out_specs=[pl.BlockSpec((1, 2, tb, LANE), lambda b_, i: (b_, 0, i, 0)),
                   pl.BlockSpec((1, 2, tb, LANE), lambda b_, i: (b_, 0, i, 0))],
        out_shape=[jax.ShapeDtypeStruct((bsz, 2, L, LANE), F32),
                   jax.ShapeDtypeStruct((bsz, 2, L, LANE), F32)],
        compiler_params=_cp("arbitrary", "arbitrary"),
        name="hy_pre",
    )(ph, ph, ph, w8, b)


def _hy_filter_kernel(z_ref, w1_ref, b1_ref, w2_ref, b2_ref, w3_ref, fr_ref, dec_ref, h_ref):
    z = z_ref[...]
    fr = fr_ref[...]
    h1 = jnp.sin(fr * (_dot_hp(z, w1_ref[...]) + b1_ref[...]))
    h2 = jnp.sin(fr * (_dot_hp(h1, w2_ref[...]) + b2_ref[...]))
    h = _dot_hp(h2, w3_ref[...])
    win = jnp.exp(-2.0 * jnp.abs(z[:, 0:1]) * dec_ref[...])
    h_ref[...] = h * win


def _hy_features(L):
    t = jnp.arange(L, dtype=F32)
    rel = (t - (L // 2)) / L
    bands = jnp.linspace(1e-4, HY_BANDS - 1, HY_BANDS, dtype=F32)
    ang = 2.0 * math.pi * rel[:, None] * bands
    z = jnp.concatenate([rel[:, None], jnp.cos(ang), -jnp.sin(ang)], axis=-1)
    return jnp.pad(z, ((0, 0), (0, LANE - HY_EMB)))


def _hy_filter_call(L, w1p, b1p, w2p, b2p, w3p, frp, dec):
    z = _hy_features(L)
    tl = min(1024, L)
    c2 = lambda i: (0, 0)
    return pl.pallas_call(
        _hy_filter_kernel,
        grid=(L // tl,),
        in_specs=[pl.BlockSpec((tl, LANE), lambda i: (i, 0)),
                  pl.BlockSpec((LANE, LANE), c2), pl.BlockSpec((1, LANE), c2),
                  pl.BlockSpec((LANE, LANE), c2), pl.BlockSpec((1, LANE), c2),
                  pl.BlockSpec((LANE, HY_WIDTH), c2), pl.BlockSpec((1, LANE), c2),
                  pl.BlockSpec((1, HY_WIDTH), c2)],
        out_specs=pl.BlockSpec((tl, HY_WIDTH), lambda i: (i, 0)),
        out_shape=jax.ShapeDtypeStruct((L, HY_WIDTH), F32),
        compiler_params=_cp("arbitrary"),
        name="hy_filter",
    )(z, w1p, b1p, w2p, b2p, w3p, frp, dec)


@functools.lru_cache(maxsize=None)
def _fft_consts():
    n1 = FFT_N1
    half = n1 // 2
    k = np.arange(n1, dtype=np.float64)
    n2 = k[:, None, None]
    k1 = k[None, :, None]
    nn = np.arange(half, dtype=np.float64)[None, None, :]
    ang = -2.0 * np.pi * (n2 * k1 / FFT_N + nn * k1 / n1)
    mr, mi = np.cos(ang), np.sin(ang)
    m1 = np.concatenate([np.concatenate([mr, -mi], axis=2), np.concatenate([mi, mr], axis=2)], axis=1)
    ang2 = -2.0 * np.pi * np.outer(k, k) / n1
    fr, fi = np.cos(ang2), np.sin(ang2)
    f2 = np.block([[fr, -fi], [fi, fr]])
    f2c = np.block([[fr, fi], [-fi, fr]])
    no = (np.arange(half, dtype=np.float64) + n1 // 4)[None, :, None]
    kk = k[None, None, :]
    ang3 = 2.0 * np.pi * (n2 * kk / FFT_N + no * kk / n1)
    ir, ii = np.cos(ang3) / FFT_N, np.sin(ang3) / FFT_N
    m3 = np.concatenate([np.concatenate([ir, -ii], axis=2), np.concatenate([ii, ir], axis=2)], axis=1)
    return tuple(np.asarray(m, dtype=np.float32) for m in (m1, f2, f2c, m3))


FFT_NB = 16
FFT_HALF = FFT_N1 // 2


def _strided_rows(ref2d, start, n):
    return ref2d[pl.ds(start, n, stride=FFT_NB), :]


def _tok_rows(ref, s, t):
    return jnp.concatenate([ref[s, hf, :, t, :] for hf in range(2)], axis=1)


def _stage_f32(dst, src_ref):
    v = src_ref[0].astype(F32).reshape(FFT_N1 * FFT_NB, HY_WIDTH)
    dst[0] = v[:, :LANE]
    dst[1] = v[:, LANE:]


def _staged_rows(scr, t):
    return jnp.concatenate([scr[hf, pl.ds(t, FFT_N1, stride=FFT_NB), :] for hf in range(2)], axis=1)


def _fft1_kernel(u_ref, m_ref, a_ref):
    for t in range(FFT_NB):
        rhs = _bf(jnp.concatenate([_tok_rows(u_ref, 0, t), _tok_rows(u_ref, 1, t)], axis=0))
        a_ref[0, t] = _bf(_dot(m_ref[t], rhs))


def _fft1_call(u, m1):
    npair = u.shape[0] // 2
    return pl.pallas_call(
        _fft1_kernel,
        grid=(npair, FFT_N1 // FFT_NB),
        in_specs=[pl.BlockSpec((2, 2, FFT_HALF, FFT_NB, LANE), lambda p, j: (p, 0, 0, j, 0)),
                  pl.BlockSpec((FFT_NB, 2 * FFT_N1, FFT_N1), lambda p, j: (j, 0, 0))],
        out_specs=pl.BlockSpec((1, FFT_NB, 2 * FFT_N1, HY_WIDTH), lambda p, j: (p, j, 0, 0)),
        out_shape=jax.ShapeDtypeStruct((npair, FFT_N1, 2 * FFT_N1, HY_WIDTH), BF16),
        compiler_params=_cp("arbitrary", "arbitrary"),
        name="hy_fft1",
    )(u, m1)


def _fft2_kernel(*refs, spectrum):
    if spectrum:
        ar_ref, ai_ref, f_ref, o_ref, sr, si = refs
    else:
        ar_ref, ai_ref, f_ref, fc_ref, h_ref, o_ref, sr, si = refs
    _stage_f32(sr, ar_ref)
    _stage_f32(si, ai_ref)
    for t in range(FFT_NB):
        rhs = _bf(jnp.concatenate([_staged_rows(sr, t), _staged_rows(si, t)], axis=0))
        x = _dot(f_ref[...], rhs)
        if spectrum:
            o_ref[t] = x
        else:
            xr, xi = x[:FFT_N1], x[FFT_N1:]
            hr, hi = h_ref[t, :FFT_N1], h_ref[t, FFT_N1:]
            y = jnp.concatenate([xr * hr - xi * hi, xr * hi + xi * hr], axis=0)
            o_ref[0, t] = _bf(_dot(fc_ref[...], _bf(y)))


def _fft2_call(a, f2, f2c, hspec):
    npair = a.shape[0]
    av = a
    nj = FFT_N1 // FFT_NB
    c2 = lambda p, j: (0, 0)
    in_specs = [pl.BlockSpec((1, FFT_N1, FFT_NB, HY_WIDTH), lambda p, j: (p, 0, j, 0)),
                pl.BlockSpec((1, FFT_N1, FFT_NB, HY_WIDTH), lambda p, j: (p, 0, nj + j, 0)),
                pl.BlockSpec((2 * FFT_N1, 2 * FFT_N1), c2)]
    staging = [pltpu.VMEM((2, FFT_N1 * FFT_NB, LANE), F32), pltpu.VMEM((2, FFT_N1 * FFT_NB, LANE), F32)]
    if hspec is None:
        return pl.pallas_call(
            functools.partial(_fft2_kernel, spectrum=True),
            grid=(1, nj),
            in_specs=in_specs,
            out_specs=pl.BlockSpec((FFT_NB, 2 * FFT_N1, HY_WIDTH), lambda p, j: (j, 0, 0)),
            out_shape=jax.ShapeDtypeStruct((FFT_N1, 2 * FFT_N1, HY_WIDTH), F32),
            scratch_shapes=staging,
            compiler_params=_cp("arbitrary", "arbitrary"),
            name="hy_fft2_spec",
        )(av, av, f2)
    in_specs += [pl.BlockSpec((2 * FFT_N1, 2 * FFT_N1), c2),
                 pl.BlockSpec((FFT_NB, 2 * FFT_N1, HY_WIDTH), lambda p, j: (j, 0, 0))]
    return pl.pallas_call(
        functools.partial(_fft2_kernel, spectrum=False),
        grid=(npair, nj),
        in_specs=in_specs,
        out_specs=pl.BlockSpec((1, FFT_NB, 2 * FFT_N1, HY_WIDTH), lambda p, j: (p, j, 0, 0)),
        out_shape=jax.ShapeDtypeStruct((npair, FFT_N1, 2 * FFT_N1, HY_WIDTH), BF16),
        scratch_shapes=staging,
        compiler_params=_cp("arbitrary", "arbitrary"),
        name="hy_fft2",
    )(av, av, f2, f2c, hspec)


def _fft3_kernel(br_ref, bi_ref, m_ref, vg_ref, x0_ref, bias_ref, y_ref, sr, si):
    _stage_f32(sr, br_ref)
    _stage_f32(si, bi_ref)
    for t in range(FFT_NB):
        rhs = _bf(jnp.concatenate([_staged_rows(sr, t), _staged_rows(si, t)], axis=0))
        out = _dot(m_ref[t], rhs)
        for s in range(2):
            conv = out[s * FFT_HALF:(s + 1) * FFT_HALF]
            y = (conv + _tok_rows(vg_ref, s, t) * bias_ref[...]) * _tok_rows(x0_ref, s, t)
            for hf in range(2):
                y_ref[s, hf, :, t, :] = y[:, hf * LANE:(hf + 1) * LANE]


def _fft3_call(bmat, m3, vg, x0, bias):
    npair = bmat.shape[0]
    nj = FFT_N1 // FFT_NB
    tok_spec = pl.BlockSpec((2, 2, FFT_HALF, FFT_NB, LANE), lambda p, j: (p, 0, 0, j, 0))
    return pl.pallas_call(
        _fft3_kernel,
        grid=(npair, nj),
        in_specs=[pl.BlockSpec((1, FFT_N1, FFT_NB, HY_WIDTH), lambda p, j: (p, 0, j, 0)),
                  pl.BlockSpec((1, FFT_N1, FFT_NB, HY_WIDTH), lambda p, j: (p, 0, nj + j, 0)),
                  pl.BlockSpec((FFT_NB, FFT_N1, 2 * FFT_N1), lambda p, j: (j, 0, 0)),
                  tok_spec, tok_spec,
                  pl.BlockSpec((1, HY_WIDTH), lambda p, j: (0, 0))],
        out_specs=tok_spec,
        out_shape=jax.ShapeDtypeStruct(vg.shape, F32),
        scratch_shapes=[pltpu.VMEM((2, FFT_N1 * FFT_NB, LANE), F32), pltpu.VMEM((2, FFT_N1 * FFT_NB, LANE), F32)],
        compiler_params=_cp("arbitrary", "arbitrary"),
        name="hy_fft3",
    )(bmat, bmat, m3, vg, x0, bias)


def _hy_direct_kernel(vg_ref, x0_ref, h_ref, bias_ref, y_ref, pad_scr, sh_scr):
    L = vg_ref.shape[2]
    u = jnp.concatenate([vg_ref[0, 0], vg_ref[0, 1]], axis=1)
    pad_scr[...] = jnp.zeros_like(pad_scr)
    pad_scr[L:2 * L, :] = u
    top = L + L // 2
    acc = jnp.zeros((L, HY_WIDTH), F32)
    for r in range(8):
        sh_scr[...] = pad_scr[r:r + 3 * L - 8, :]
        a_lo = -(-(top - L + 1 - r) // 8)
        a_hi = (top - r) // 8

        def body(a, acc, r=r):
            m = top - (a * 8 + r)
            return acc + h_ref[pl.ds(m, 1), :] * sh_scr[pl.ds(pl.multiple_of(a * 8, 8), L), :]

        acc = lax.fori_loop(a_lo, a_hi + 1, body, acc)
    y = (acc + u * bias_ref[...]) * jnp.concatenate([x0_ref[0, 0], x0_ref[0, 1]], axis=1)
    y_ref[0, 0] = y[:, :LANE]
    y_ref[0, 1] = y[:, LANE:]


def _hy_direct_call(vg, x0, h, bias):
    bsz, _, L, _ = vg.shape
    tok = pl.BlockSpec((1, 2, L, LANE), lambda b: (b, 0, 0, 0))
    return pl.pallas_call(
        _hy_direct_kernel,
        grid=(bsz,),
        in_specs=[tok, tok, pl.BlockSpec((L, HY_WIDTH), lambda b: (0, 0)),
                  pl.BlockSpec((1, HY_WIDTH), lambda b: (0, 0))],
        out_specs=tok,
        out_shape=jax.ShapeDtypeStruct(vg.shape, F32),
        scratch_shapes=[pltpu.VMEM((3 * L, HY_WIDTH), F32), pltpu.VMEM((3 * L - 8, HY_WIDTH), F32)],
        compiler_params=_cp("arbitrary"),
        name="hy_direct",
    )(vg, x0, h, bias)


def _out_kernel(x_ref, yg_ref, yh_ref, ys_ref, mod_ref, g2_ref, gf_ref, wo_ref, w1_ref, w3_ref, w2_ref,
                o_ref, h_scr, acc_scr, *scr, final, nff, col_major):
    j = pl.program_id(1)
    m = mod_ref[0]

    @pl.when(j == 0)
    def _():
        if col_major:
            ys_scr, = scr
            for c in range(GRID_W):
                for k in range(SSD_INNER // LANE):
                    ys_scr[k, pl.ds(c, ROWS_PER_TILE, stride=GRID_W), :] = ys_ref[0, c, :, k * LANE:(k + 1) * LANE]
            ys = jnp.concatenate([ys_scr[k] for k in range(SSD_INNER // LANE)], axis=1)
        else:
            ys = ys_ref[...]
        yh = jnp.concatenate([yh_ref[0, 0], yh_ref[0, 1]], axis=1)
        mix = (_dot(_bf(yg_ref[...]), wo_ref[0:GLA_V])
               + _dot(_bf(yh), wo_ref[GLA_V:GLA_V + HY_WIDTH])
               + _dot(_bf(ys), wo_ref[GLA_V + HY_WIDTH:]))
        x1 = x_ref[...] + m[2:3] * mix
        o_ref[...] = x1
        ms = jnp.mean(x1 * x1, axis=-1, keepdims=True)
        h = x1 * lax.rsqrt(ms + EPS) * g2_ref[...]
        h_scr[...] = _bf(h * (1.0 + m[4:5]) + m[3:4])
        acc_scr[...] = jnp.zeros_like(acc_scr)

    h = h_scr[...]
    a = _dot(h, w1_ref[...])
    b = _dot(h, w3_ref[...])
    acc_scr[...] += _dot(_bf(_silu(a) * b), w2_ref[...])

    @pl.when(j == nff - 1)
    def _():
        x2 = o_ref[...] + m[5:6] * acc_scr[...]
        if final:
            ms = jnp.mean(x2 * x2, axis=-1, keepdims=True)
            x2 = x2 * lax.rsqrt(ms + EPS) * gf_ref[...]
        o_ref[...] = x2


def _out_call(x2, yg, yh, ys, mods, mod_row, g2, gf, wo, w1, w3, w2, final, col_major):
    t = x2.shape[0]
    seq = yh.shape[2]
    tm = min(TOK_TILE, seq)
    tiles = seq // tm
    nff = D_FF // FF_TILE
    c2 = lambda i, j: (0, 0)
    if col_major:
        assert tm == TOK_TILE
        ys_spec = pl.BlockSpec((1, GRID_W, ROWS_PER_TILE, SSD_INNER), lambda i, j: (i // tiles, 0, i % tiles, 0))
        extra = [pltpu.VMEM((SSD_INNER // LANE, tm, LANE), F32)]
    else:
        ys = ys.reshape(t, SSD_INNER)
        ys_spec = pl.BlockSpec((tm, SSD_INNER), lambda i, j: (i, 0))
        extra = []
    return pl.pallas_call(
        functools.partial(_out_kernel, final=final, nff=nff, col_major=col_major),
        grid=(t // tm, nff),
        in_specs=[pl.BlockSpec((tm, D_MODEL), lambda i, j: (i, 0)),
                  pl.BlockSpec((tm, GLA_V), lambda i, j: (i, 0)),
                  pl.BlockSpec((1, 2, tm, LANE), lambda i, j: (i // tiles, 0, i % tiles, 0)),
                  ys_spec,
                  pl.BlockSpec((1, 6, D_MODEL), lambda i, j: (mod_row(i), 0, 0)),
                  pl.BlockSpec((1, D_MODEL), c2),
                  pl.BlockSpec((1, D_MODEL), c2),
                  pl.BlockSpec((D_MODEL, D_MODEL), c2),
                  pl.BlockSpec((D_MODEL, FF_TILE), lambda i, j: (0, j)),
                  pl.BlockSpec((D_MODEL, FF_TILE), lambda i, j: (0, j)),
                  pl.BlockSpec((FF_TILE, D_MODEL), lambda i, j: (j, 0))],
        out_specs=pl.BlockSpec((tm, D_MODEL), lambda i, j: (i, 0)),
        out_shape=jax.ShapeDtypeStruct((t, D_MODEL), F32),
        scratch_shapes=[pltpu.VMEM((tm, D_MODEL), BF16), pltpu.VMEM((tm, D_MODEL), F32)] + extra,
        compiler_params=_cp("arbitrary", "arbitrary"),
        name="out_ffn",
    )(x2, yg, yh, ys, mods, g2.reshape(1, D_MODEL), gf.reshape(1, D_MODEL), wo, w1, w3, w2)


def _pack_w_in(w_in):
    z = lambda n: jnp.zeros(w_in.shape[:2] + (n,), w_in.dtype)
    q, k = w_in[..., 0:192], w_in[..., 192:384]
    v, g = w_in[..., 384:768], w_in[..., 768:1152]
    gk = w_in[..., 1152:1184]
    hy = w_in[..., 1184:1952]
    zz = w_in[..., 1952:2336]
    xbc = w_in[..., 2336:3232]
    dt = w_in[..., 3232:3244]
    packed = jnp.concatenate([q, z(64), k, z(64), v, g, gk, z(96), hy, zz, xbc, dt, z(116)], axis=-1)
    return packed.astype(BF16)


def _pad_to(a, shape):
    return jnp.pad(a, [(0, s - d) for d, s in zip(a.shape, shape)])


def _mixers(pg, ph, ps, lw, states):
    g_f0, g_b0, m_f0, m_b0 = states
    o_f, g_f = _gla_call(pg, None, lw['wgk_f'], lw['bgk_f'], None, g_f0, False)
    gla_y, g_b = _gla_call(pg, o_f, lw['wgk_b'], lw['bgk_b'], lw['gla_nw'], g_b0, True)

    vg, x0 = _hy_pre_call(ph, lw['hy_sw'], lw['hy_sb'])
    L = ph.shape[1]
    h = _hy_filter_call(L, lw['hy_w1'], lw['hy_b1'], lw['hy_w2'], lw['hy_b2'], lw['hy_w3'], lw['hy_freq'],
                        lw['hy_decay'])
    if L == FFT_N // 2:
        m1, f2, f2c, m3 = lw['fft']
        tok5 = lambda t: t.reshape(t.shape[0], 2, FFT_HALF, FFT_N1, LANE)
        hsplit = h.reshape(L, 2, LANE).transpose(1, 0, 2)
        hspec = _fft2_call(_fft1_call(tok5(jnp.stack([hsplit, jnp.zeros_like(hsplit)])), m1), f2, f2c, None)
        vg5 = tok5(vg)
        bmat = _fft2_call(_fft1_call(vg5, m1), f2, f2c, hspec)
        hy_y = _fft3_call(bmat, m3, vg5, tok5(x0), lw['hy_bias']).reshape(vg.shape)
    else:
        hy_y = _hy_direct_call(vg, x0, h, lw['hy_bias'])

    ssd_args = (lw['ssd_cw'], lw['ssd_cb'], lw['ssd_dtb'], lw['ssd_alog'])
    y_f, m_f = _ssd_call(ps, None, *ssd_args, None, None, m_f0, False)
    ssd_y, m_b = _ssd_call(ps, y_f, *ssd_args, lw['ssd_dx'], lw['ssd_nw'], m_b0, True)
    return (gla_y, hy_y, ssd_y), (g_f, g_b, m_f, m_b)


def kernel(x, c, ctx, c_ctx, mod_w, mod_b, norm1_g, norm2_g, w_in, gla_gk_w_f, gla_gk_b_f, gla_gk_w_b, gla_gk_b_b, gla_norm_w, hy_short_w, hy_short_b, hy_w1, hy_b1, hy_w2, hy_b2, hy_w3, hy_freq, hy_decay, hy_bias, ssd_conv_w, ssd_conv_b, ssd_dt_bias_f, ssd_dt_bias_b, ssd_a_log_f, ssd_a_log_b, ssd_d, ssd_norm_w, w_out, ffn_w1, ffn_w3, ffn_w2, final_g):
    bsz, seq, _ = x.shape
    lc = ctx.shape[1]
    c8 = jnp.concatenate([c, c_ctx[None], jnp.zeros((8 - bsz - 1, D_MODEL), F32)], axis=0)
    mods_all = _mod_call(c8, mod_w, mod_b).reshape(DEPTH, 8, 6, D_MODEL)

    w_in_p = _pack_w_in(w_in)
    fft_mats = tuple(_bf(jnp.asarray(m)) for m in _fft_consts())
    w_out_b, w1_b, w3_b, w2_b = _bf(w_out), _bf(ffn_w1), _bf(ffn_w3), _bf(ffn_w2)

    xt = x.reshape(bsz * seq, D_MODEL)
    ct = ctx.reshape(bsz * lc, D_MODEL)
    tiles_per_seq = seq // TOK_TILE
    row_x = lambda i: i // tiles_per_seq
    row_c = lambda i: bsz

    zeros_states = (jnp.zeros((bsz, GLA_DV, 256), F32), jnp.zeros((bsz, GLA_DV, 256), F32),
                    jnp.zeros((bsz, 2 * SSD_STATE, SSD_INNER), F32),
                    jnp.zeros((bsz, 2 * SSD_STATE, SSD_INNER), F32))

    for l in range(DEPTH):
        def gkw(w, off):
            return _bf(_pad_to(jnp.pad(w, ((off, 0), (0, 0))), (LANE, 256)))
        lw = {
            'wgk_f': gkw(gla_gk_w_f[l], 0), 'wgk_b': gkw(gla_gk_w_b[l], GLA_LOWRANK),
            'bgk_f': _pad_to(gla_gk_b_f[l][None], (1, 256)), 'bgk_b': _pad_to(gla_gk_b_b[l][None], (1, 256)),
            'gla_nw': jnp.tile(gla_norm_w[l], GLA_HEADS)[None],
            'hy_sw': _pad_to(hy_short_w[l], (8, PH_W)), 'hy_sb': hy_short_b[l][None],
            'hy_w1': _pad_to(hy_w1[l], (LANE, LANE)), 'hy_b1': _pad_to(hy_b1[l][None], (1, LANE)),
            'hy_w2': _pad_to(hy_w2[l], (LANE, LANE)), 'hy_b2': _pad_to(hy_b2[l][None], (1, LANE)),
            'hy_w3': _pad_to(hy_w3[l], (LANE, HY_WIDTH)), 'hy_freq': _pad_to(hy_freq[l][None], (1, LANE)),
            'hy_decay': hy_decay[l][None], 'hy_bias': hy_bias[l][None],
            'ssd_cw': _pad_to(ssd_conv_w[l], (8, SSD_CONV_DIM)), 'ssd_cb': ssd_conv_b[l][None],
            'ssd_dtb': _pad_to(jnp.concatenate([ssd_dt_bias_f[l], ssd_dt_bias_b[l]])[None], (1, LANE)),
            'ssd_alog': _pad_to(jnp.concatenate([ssd_a_log_f[l], ssd_a_log_b[l]])[None], (1, LANE)),
            'ssd_dx': jnp.repeat(ssd_d[l], SSD_P)[None], 'ssd_nw': ssd_norm_w[l][None],
        }
        lw['fft'] = fft_mats
        mods = mods_all[l]
        yc, ctx_states = _mixers(*_in_call(ct, mods, row_c, norm1_g[l], w_in_p[l], bsz, False), lw, zeros_states)
        yx, _ = _mixers(*_in_call(xt, mods, row_x, norm1_g[l], w_in_p[l], bsz, True), lw, ctx_states)
        last = l == DEPTH - 1
        ffn = (w_out_b[l], w1_b[l], w3_b[l], w2_b[l])
        xt = _out_call(xt, yx[0].reshape(-1, GLA_V), yx[1], yx[2],
                       mods, row_x, norm2_g[l], final_g, *ffn, last, True)
        if not last:
            ct = _out_call(ct, yc[0].reshape(-1, GLA_V), yc[1], yc[2],
                           mods, row_c, norm2_g[l], final_g, *ffn, False, False)
    return xt.reshape(bsz, seq, D_MODEL)
```

```python
import functools
import math

import numpy as np
import jax
import jax.numpy as jnp
from jax import lax
from jax.experimental import pallas as pl
from jax.experimental.pallas import tpu as pltpu

F32 = jnp.float32
BF16 = jnp.bfloat16

D_MODEL = 1024
DEPTH = 2
GRID_W = 64
EPS = 1e-6
GLA_V = 384
GLA_DV = 64
GLA_HEADS = 6
GLA_DK = 32
GLA_QK = 192
GLA_LOWRANK = 16
GLA_TAU = 16.0
GLA_CHUNK = 64
GLA_BLOCK = 256
HY_WIDTH = 256
HY_BANDS = 16
HY_EMB = 1 + 2 * HY_BANDS
HY_ORDER = 64
SSD_INNER = 384
SSD_HEADS = 6
SSD_GROUPS = 2
SSD_HG = 3
SSD_P = 64
SSD_STATE = 128
SSD_CONV_DIM = SSD_INNER + 2 * SSD_GROUPS * SSD_STATE
SSD_CHUNK = 128
D_FF = 2816
FF_TILE = 1408
PG_W = 1408
PH_W = 768
PS_W = 1408
LANE = 128
FFT_N1 = 128
FFT_N = FFT_N1 * FFT_N1

VMEM_LIMIT = 56 * 1024 * 1024


def _cp(*sem):
    return pltpu.CompilerParams(dimension_semantics=sem, vmem_limit_bytes=VMEM_LIMIT)


def _bf(x):
    return x.astype(BF16)


def _dot(a, b):
    return jnp.dot(a, b, preferred_element_type=F32)


def _dot_nt(a, b):
    return lax.dot_general(a, b, (((1,), (1,)), ((), ())), preferred_element_type=F32)


def _dot_tn(a, b):
    return lax.dot_general(a, b, (((0,), (0,)), ((), ())), preferred_element_type=F32)


def _split3(x):
    hi = _bf(x)
    r1 = x - hi.astype(F32)
    mid = _bf(r1)
    lo = _bf(r1 - mid.astype(F32))
    return hi, mid, lo


def _dot_sel(sel_bf, x):
    hi, mid, lo = _split3(x)
    return _dot(sel_bf, hi) + _dot(sel_bf, mid) + _dot(sel_bf, lo)


def _dot_hp(a, b):
    ah = _bf(a)
    al = _bf(a - ah.astype(F32))
    bh = _bf(b)
    bl = _bf(b - bh.astype(F32))
    return _dot(ah, bh) + _dot(ah, bl) + _dot(al, bh)


def _silu(x):
    return x * jax.nn.sigmoid(x)


def _softplus(x):
    return jnp.maximum(x, 0.0) + jnp.log1p(jnp.exp(-jnp.abs(x)))


def _log_sigmoid(x):
    return jnp.minimum(x, 0.0) - jnp.log1p(jnp.exp(-jnp.abs(x)))


def _mod_kernel(c_ref, w_ref, b_ref, o_ref):
    act = _silu(c_ref[...])
    o_ref[0] = _dot(_bf(act), _bf(w_ref[0])) + b_ref[0]


def _mod_call(c8, mod_w, mod_b):
    nt = 1536
    n = mod_w.shape[-1]
    return pl.pallas_call(
        _mod_kernel,
        grid=(DEPTH, n // nt),
        in_specs=[pl.BlockSpec((8, D_MODEL), lambda l, j: (0, 0)),
                  pl.BlockSpec((1, D_MODEL, nt), lambda l, j: (l, 0, j)),
                  pl.BlockSpec((1, 1, nt), lambda l, j: (l, 0, j))],
        out_specs=pl.BlockSpec((1, 8, nt), lambda l, j: (l, 0, j)),
        out_shape=jax.ShapeDtypeStruct((DEPTH, 8, n), F32),
        compiler_params=_cp("arbitrary", "arbitrary"),
        name="adaln_mod",
    )(c8, mod_w, mod_b.reshape(DEPTH, 1, n))


TOK_TILE = 512
ROWS_PER_TILE = TOK_TILE // GRID_W


def _in_kernel(x_ref, mod_ref, g_ref, w_ref, og_ref, oh_ref, os_ref, *scr, col_major):
    x = x_ref[...]
    ms = jnp.mean(x * x, axis=-1, keepdims=True)
    y = x * lax.rsqrt(ms + EPS) * g_ref[...]
    m = mod_ref[0]
    h = _bf(y * (1.0 + m[1:2]) + m[0:1])
    og_ref[...] = _dot(h, w_ref[:, 0:PG_W])
    oh_ref[...] = _dot(h, w_ref[:, PG_W:PG_W + PH_W])
    ps = _dot(h, w_ref[:, PG_W + PH_W:])
    if col_major:
        ps_scr, = scr
        for k in range(PS_W // LANE):
            for r in range(ROWS_PER_TILE):
                ps_scr[k, pl.ds(r, GRID_W, stride=ROWS_PER_TILE), :] = ps[r * GRID_W:(r + 1) * GRID_W,
                                                                         k * LANE:(k + 1) * LANE]
        for k in range(PS_W // LANE):
            os_ref[0, :, :, k * LANE:(k + 1) * LANE] = ps_scr[k].reshape(GRID_W, ROWS_PER_TILE, LANE)
    else:
        os_ref[...] = ps


def _in_call(x2, mods, mod_row, g, w, bsz, col_major):
    t = x2.shape[0]
    tm = TOK_TILE
    wtot = PG_W + PH_W + PS_W
    seq = t // bsz
    if col_major:
        tiles = seq // tm
        assert seq == GRID_W * SSD_CHUNK and ROWS_PER_TILE == 8
        os_spec = pl.BlockSpec((1, GRID_W, ROWS_PER_TILE, PS_W), lambda i: (i // tiles, 0, i % tiles, 0))
        os_shape = jax.ShapeDtypeStruct((bsz, GRID_W, SSD_CHUNK, PS_W), F32)
        scratch = [pltpu.VMEM((PS_W // LANE, tm, LANE), F32)]
    else:
        os_spec = pl.BlockSpec((tm, PS_W), lambda i: (i, 0))
        os_shape = jax.ShapeDtypeStruct((t, PS_W), F32)
        scratch = []
    pg, ph, ps = pl.pallas_call(
        functools.partial(_in_kernel, col_major=col_major),
        grid=(t // tm,),
        in_specs=[pl.BlockSpec((tm, D_MODEL), lambda i: (i, 0)),
                  pl.BlockSpec((1, 6, D_MODEL), lambda i: (mod_row(i), 0, 0)),
                  pl.BlockSpec((1, D_MODEL), lambda i: (0, 0)),
                  pl.BlockSpec((D_MODEL, wtot), lambda i: (0, 0), pipeline_mode=pl.Buffered(1))],
        out_specs=[pl.BlockSpec((tm, PG_W), lambda i: (i, 0)),
                   pl.BlockSpec((tm, PH_W), lambda i: (i, 0)),
                   os_spec],
        out_shape=[jax.ShapeDtypeStruct((t, PG_W), F32),
                   jax.ShapeDtypeStruct((t, PH_W), F32),
                   os_shape],
        scratch_shapes=scratch,
        compiler_params=_cp("arbitrary"),
        name="in_proj",
    )(x2, mods, g.reshape(1, D_MODEL), w)
    return (pg.reshape(bsz, seq, PG_W), ph.reshape(bsz, seq, PH_W),
            ps.reshape(bsz, seq // SSD_CHUNK, SSD_CHUNK, PS_W))


def _gla_kernel(*refs, reverse, nblk):
    if reverse:
        (p_ref, of_ref, wgk_ref, bgk_ref, nw_ref, s0_ref, y_ref, sf_ref, st_scr) = refs
    else:
        (p_ref, wgk_ref, bgk_ref, s0_ref, of_ref, sf_ref, st_scr) = refs
    i = pl.program_id(1)
    tb = GLA_BLOCK

    @pl.when(i == 0)
    def _():
        st_scr[...] = s0_ref[...]

    ri = lax.broadcasted_iota(jnp.int32, (tb, tb), 0)
    ci = lax.broadcasted_iota(jnp.int32, (tb, tb), 1)
    same = (ri // GLA_CHUNK) == (ci // GLA_CHUNK)
    tri = same & ((ci >= ri) if reverse else (ci <= ri))
    tri_bf = jnp.where(tri, 1.0, 0.0).astype(BF16)
    lane = lax.broadcasted_iota(jnp.int32, (1, 256), 1)
    hms = [(lane // GLA_DK) == h for h in range(GLA_HEADS)]

    nseq = p_ref.shape[0]
    seqs = range(nseq)
    ps = [p_ref[s_] for s_ in seqs]
    os_, sts = _gla_blocks(ps, [st_scr[s_] for s_ in seqs], wgk_ref[...], bgk_ref[...], tri, tri_bf, lane, hms,
                           reverse)
    for s_ in seqs:
        st_scr[s_] = sts[s_]

    @pl.when(i == nblk - 1)
    def _():
        for s_ in seqs:
            sf_ref[s_] = sts[s_]

    if not reverse:
        for s_ in seqs:
            of_ref[s_] = os_[s_]
    else:
        r2 = lax.broadcasted_iota(jnp.int32, (GLA_V, GLA_V), 0) // GLA_DV
        c2 = lax.broadcasted_iota(jnp.int32, (GLA_V, GLA_V), 1) // GLA_DV
        ind = jnp.where(r2 == c2, 1.0, 0.0).astype(BF16)
        ot = [of_ref[s_] + os_[s_] for s_ in seqs]
        sq = [t * t for t in ot]
        sh = [_bf(t) for t in sq]
        sl = [_bf(sq[s_] - sh[s_].astype(F32)) for s_ in seqs]
        ms = [(_dot(sh[s_], ind) + _dot(sl[s_], ind)) * (1.0 / GLA_DV) for s_ in seqs]
        for s_ in seqs:
            g = ps[s_][:, 896:1280]
            y_ref[s_] = ot[s_] * lax.rsqrt(ms[s_] + EPS) * nw_ref[...] * _silu(g)


def _gla_blocks(ps, sts, wgk, bgk, tri, tri_bf, lane, hms, reverse):
    tb = GLA_BLOCK
    nch = tb // GLA_CHUNK
    seqs = range(len(ps))
    k = [p[:, 256:512] for p in ps]
    pre = [_dot(_bf(p[:, 1280:1408]), wgk) + bgk for p in ps]
    la = [_log_sigmoid(t) * (1.0 / GLA_TAU) for t in pre]
    b = [_dot_sel(tri_bf, t) for t in la]
    qd = [_bf(ps[s][:, 0:256] * ((GLA_DK ** -0.5) * jnp.exp(b[s]))) for s in seqs]
    ki = [_bf(k[s] * jnp.exp(-b[s])) for s in seqs]
    vb = [_bf(p[:, 512:896]) for p in ps]

    lhs = [jnp.concatenate([jnp.where(hm, t, jnp.zeros_like(t)) for hm in hms], axis=0) for t in qd]
    sc = [_dot_nt(lhs[s], ki[s]) for s in seqs]
    pm = [jnp.concatenate([_bf(jnp.where(tri, t[h * tb:(h + 1) * tb], 0.0)) for h in range(GLA_HEADS)], axis=0)
          for t in sc]
    ra = [_dot(pm[s][:4 * tb], vb[s][:, 0:256]) for s in seqs]
    rb = [_dot(pm[s][4 * tb:], vb[s][:, 256:384]) for s in seqs]
    hl = lane // GLA_DV
    o_intra = []
    for s in seqs:
        oa = jnp.zeros((tb, 256), F32)
        for h in range(4):
            oa = oa + jnp.where(hl == h, ra[s][h * tb:(h + 1) * tb], 0.0)
        ob = jnp.where(hl[:, :LANE] == 0, rb[s][:tb], rb[s][tb:])
        o_intra.append(jnp.concatenate([oa, ob], axis=1))

    sts = list(sts)
    outs = [[None] * nch for _ in seqs]
    order = range(nch - 1, -1, -1) if reverse else range(nch)
    for c in order:
        r0 = c * GLA_CHUNK
        rows = slice(r0, r0 + GLA_CHUNK)
        edge = r0 if reverse else r0 + GLA_CHUNK - 1
        bl = [t[edge:edge + 1] for t in b]
        kd = [_bf(k[s][rows] * jnp.exp(bl[s] - b[s][rows])) for s in seqs]
        stbd = [_bf(jnp.concatenate([jnp.where(hm, t, 0.0) for hm in hms], axis=0)) for t in sts]
        for s in seqs:
            outs[s][c] = _dot_nt(qd[s][rows], stbd[s])
        full = [_dot_tn(vb[s][rows], kd[s]) for s in seqs]
        for s in seqs:
            ds = jnp.zeros((GLA_DV, 256), F32)
            for h in range(GLA_HEADS):
                ds = ds + jnp.where(hms[h], full[s][h * GLA_DV:(h + 1) * GLA_DV], 0.0)
            sts[s] = jnp.exp(bl[s]) * sts[s] + ds
    return [o_intra[s] + jnp.concatenate(outs[s], axis=0) for s in seqs], sts


SEQ_PER_STEP = 4
GLA_SEQ_PER_STEP = 4


def _gla_call(pg, o_f, wgk, bgk, nw, s0, reverse):
    bsz, L, _ = pg.shape
    nblk = L // GLA_BLOCK
    ns = GLA_SEQ_PER_STEP
    blk = (lambda b, i: (b, nblk - 1 - i, 0)) if reverse else (lambda b, i: (b, i, 0))
    const2 = lambda b, i: (0, 0)
    st_spec = pl.BlockSpec((ns, GLA_DV, 256), lambda b, i: (b, 0, 0))
    p_spec = pl.BlockSpec((ns, GLA_BLOCK, PG_W), blk)
    o_spec = pl.BlockSpec((ns, GLA_BLOCK, GLA_V), blk)
    if reverse:
        in_specs = [p_spec, o_spec, pl.BlockSpec((LANE, 256), const2), pl.BlockSpec((1, 256), const2),
                    pl.BlockSpec((1, GLA_V), const2), st_spec]
        args = (pg, o_f, wgk, bgk, nw, s0)
    else:
        in_specs = [p_spec, pl.BlockSpec((LANE, 256), const2), pl.BlockSpec((1, 256), const2), st_spec]
        args = (pg, wgk, bgk, s0)
    return pl.pallas_call(
        functools.partial(_gla_kernel, reverse=reverse, nblk=nblk),
        grid=(bsz // ns, nblk),
        in_specs=in_specs,
        out_specs=[o_spec, st_spec],
        out_shape=[jax.ShapeDtypeStruct((bsz, L, GLA_V), F32),
                   jax.ShapeDtypeStruct((bsz, GLA_DV, 256), F32)],
        scratch_shapes=[pltpu.VMEM((ns, GLA_DV, 256), F32)],
        compiler_params=_cp("arbitrary", "arbitrary"),
        name="gla_bwd" if reverse else "gla_fwd",
    )(*args)


def _expand_heads(t, lo):
    r = t.shape[0]
    lane = lax.broadcasted_iota(jnp.int32, (1, LANE), 1)
    tiles = []
    for j in range(SSD_HEADS // 2):
        a = jnp.broadcast_to(t[:, lo + 2 * j:lo + 2 * j + 1], (r, LANE))
        b = jnp.broadcast_to(t[:, lo + 2 * j + 1:lo + 2 * j + 2], (r, LANE))
        tiles.append(jnp.where(lane < SSD_P, a, b))
    return jnp.concatenate(tiles, axis=1)


def _ssd_prep(cur, prow, nrow, cw, cb, dtb, alog):
    q_ = SSD_CHUNK
    x = cur[:, 384:1280]
    ridx = lax.broadcasted_iota(jnp.int32, (q_, 1), 0)
    xm = jnp.where(ridx == 0, prow, pltpu.roll(x, 1, 0))
    xp = jnp.where(ridx == q_ - 1, nrow, pltpu.roll(x, q_ - 1, 0))
    act = _silu(cw[0:1] * xm + cw[1:2] * x + cw[2:3] * xp + cb)
    dtt = _softplus(cur[:, 1280:1408] + dtb)
    a = -jnp.exp(alog) * dtt
    return act, dtt, a


def _ssd_scans(xs, bmb, cmb, dtt, a, sts, tri, tri_bf, reverse):
    q_ = SSD_CHUNK
    lo = SSD_HEADS if reverse else 0
    seqs = range(len(xs))
    cs = [_dot_sel(tri_bf, t) for t in a]
    cst = [t.T for t in cs]
    dtT = [t.T for t in dtt]
    edge = 0 if reverse else q_ - 1
    cs_last = [t[edge:edge + 1] for t in cs]
    grp = lambda t, g: t[:, g * SSD_STATE:(g + 1) * SSD_STATE]
    cbs = [[_dot_nt(grp(cmb[s], g), grp(bmb[s], g)) for g in range(SSD_GROUPS)] for s in seqs]
    ms = [[] for _ in seqs]
    for h in range(SSD_HEADS):
        l = lo + h
        for s in seqs:
            seg = cs[s][:, l:l + 1] - cst[s][l:l + 1, :]
            dec = jnp.exp(jnp.where(tri, seg, -jnp.inf))
            ms[s].append(_bf(cbs[s][h // SSD_HG] * dec * dtT[s][l:l + 1, :]))
    mst = [jnp.concatenate(t, axis=0) for t in ms]
    xsb = [_bf(t) for t in xs]
    ra = [_dot(mst[s][:4 * q_], xsb[s][:, 0:256]) for s in seqs]
    rb = [_dot(mst[s][4 * q_:], xsb[s][:, 256:384]) for s in seqs]
    lane = lax.broadcasted_iota(jnp.int32, (1, 256), 1)
    hl = lane // SSD_P
    cs_x = [_expand_heads(t, lo) for t in cs]
    csl_x = [_expand_heads(t, lo) for t in cs_last]
    dt_x = [_expand_heads(t, lo) for t in dtt]
    ystate = [_dot(cmb[s], _bf(sts[s])) for s in seqs]
    xw = [_bf(xs[s] * (jnp.exp(csl_x[s] - cs_x[s]) * dt_x[s])) for s in seqs]
    full = [_dot_tn(bmb[s], xw[s]) for s in seqs]
    r2 = lax.broadcasted_iota(jnp.int32, (2 * SSD_STATE, SSD_INNER), 0) // SSD_STATE
    c2 = lax.broadcasted_iota(jnp.int32, (2 * SSD_STATE, SSD_INNER), 1) // (SSD_HG * SSD_P)
    ys, new_sts = [], []
    for s in seqs:
        ya = jnp.zeros((q_, 256), F32)
        for h in range(4):
            ya = ya + jnp.where(hl == h, ra[s][h * q_:(h + 1) * q_], 0.0)
        yb = jnp.where(hl[:, :LANE] == 0, rb[s][:q_], rb[s][q_:])
        ys.append(jnp.concatenate([ya, yb], axis=1) + jnp.exp(cs_x[s]) * ystate[s])
        new_sts.append(jnp.exp(csl_x[s]) * sts[s] + jnp.where(r2 == c2, full[s], 0.0))
    return ys, new_sts


def _ssd_kernel(*refs, reverse, nchunk):
    if reverse:
        (z_ref, act_ref, dta_ref, yf_ref, dx_ref, nw_ref, s0_ref, y_ref, sf_ref, st_scr) = refs
    else:
        (cur_ref, prev_ref, next_ref, cw_ref, cb_ref, dtb_ref, alog_ref, s0_ref,
         yf_ref, act_ref, dta_ref, sf_ref, st_scr) = refs
    i = pl.program_id(1)
    c = (nchunk - 1 - i) if reverse else i
    q_ = SSD_CHUNK

    @pl.when(i == 0)
    def _():
        st_scr[...] = s0_ref[...]

    ri = lax.broadcasted_iota(jnp.int32, (q_, q_), 0)
    ci = lax.broadcasted_iota(jnp.int32, (q_, q_), 1)
    tri = (ci >= ri) if reverse else (ci <= ri)
    tri_bf = jnp.where(tri, 1.0, 0.0).astype(BF16)

    seqs = range(st_scr.shape[0])
    if reverse:
        actb = [act_ref[s_, 0] for s_ in seqs]
        xs = [t[:, 0:384].astype(F32) for t in actb]
        dtt = [dta_ref[s_, 0][:, :LANE] for s_ in seqs]
        a = [dta_ref[s_, 0][:, LANE:] for s_ in seqs]
    else:
        has_prev = (c > 0).astype(F32)
        has_next = (c < nchunk - 1).astype(F32)
        actb, xs, dtt, a = [], [], [], []
        for s_ in seqs:
            prow = prev_ref[s_, 0][7:8, 384:1280] * has_prev
            nrow = next_ref[s_, 0][0:1, 384:1280] * has_next
            act, dt_, a_ = _ssd_prep(cur_ref[s_, 0], prow, nrow, cw_ref, cb_ref[...], dtb_ref[...], alog_ref[...])
            actb.append(_bf(act))
            act_ref[s_, 0] = actb[s_]
            dta_ref[s_, 0] = jnp.concatenate([dt_, a_], axis=1)
            xs.append(act[:, 0:384])
            dtt.append(dt_)
            a.append(a_)
    ys, sts = _ssd_scans(xs, [t[:, 384:640] for t in actb], [t[:, 640:896] for t in actb], dtt, a,
                         [st_scr[s_] for s_ in seqs], tri, tri_bf, reverse)
    for s_ in seqs:
        st_scr[s_] = sts[s_]

    @pl.when(i == nchunk - 1)
    def _():
        for s_ in seqs:
            sf_ref[s_] = sts[s_]

    for s_ in seqs:
        if not reverse:
            yf_ref[s_, 0] = ys[s_]
        else:
            yt = yf_ref[s_, 0] + ys[s_] + dx_ref[...] * xs[s_]
            yz = yt * _silu(z_ref[s_, 0])
            l384 = lax.broadcasted_iota(jnp.int32, (1, SSD_INNER), 1)
            g0 = l384 < (SSD_INNER // SSD_GROUPS)
            sq = yz * yz
            m0 = jnp.sum(jnp.where(g0, sq, 0.0), axis=-1, keepdims=True)
            m1 = jnp.sum(jnp.where(g0, 0.0, sq), axis=-1, keepdims=True)
            msq = jnp.where(g0, m0, m1) * (1.0 / (SSD_INNER // SSD_GROUPS))
            y_ref[s_, 0] = yz * lax.rsqrt(msq + EPS) * nw_ref[...]


def _ssd_fwd_call(ps, cw, cb, dtb, alog, s0):
    bsz, nchunk, _, _ = ps.shape
    ns = SEQ_PER_STEP
    cur_map = lambda b, i: (b, i, 0, 0)
    prev_map = lambda b, i: (b, jnp.maximum(i - 1, 0), SSD_CHUNK // 8 - 1, 0)
    next_map = lambda b, i: (b, jnp.minimum(i + 1, nchunk - 1), 0, 0)
    const2 = lambda b, i: (0, 0)
    chunk = lambda w: pl.BlockSpec((ns, 1, SSD_CHUNK, w), cur_map)
    st_spec = pl.BlockSpec((ns, 2 * SSD_STATE, SSD_INNER), lambda b, i: (b, 0, 0))
    return pl.pallas_call(
        functools.partial(_ssd_kernel, reverse=False, nchunk=nchunk),
        grid=(bsz // ns, nchunk),
        in_specs=[chunk(PS_W), pl.BlockSpec((ns, 1, 8, PS_W), prev_map), pl.BlockSpec((ns, 1, 8, PS_W), next_map),
                  pl.BlockSpec((8, SSD_CONV_DIM), const2), pl.BlockSpec((1, SSD_CONV_DIM), const2),
                  pl.BlockSpec((1, LANE), const2), pl.BlockSpec((1, LANE), const2), st_spec],
        out_specs=[chunk(SSD_INNER), chunk(SSD_CONV_DIM), chunk(2 * LANE), st_spec],
        out_shape=[jax.ShapeDtypeStruct((bsz, nchunk, SSD_CHUNK, SSD_INNER), F32),
                   jax.ShapeDtypeStruct((bsz, nchunk, SSD_CHUNK, SSD_CONV_DIM), BF16),
                   jax.ShapeDtypeStruct((bsz, nchunk, SSD_CHUNK, 2 * LANE), F32),
                   jax.ShapeDtypeStruct((bsz, 2 * SSD_STATE, SSD_INNER), F32)],
        scratch_shapes=[pltpu.VMEM((ns, 2 * SSD_STATE, SSD_INNER), F32)],
        compiler_params=_cp("arbitrary", "arbitrary"),
        name="ssd_fwd",
    )(ps, ps, ps, cw, cb, dtb, alog, s0)


def _ssd_bwd_call(ps, act, dta, y_f, dx, nw, s0):
    bsz, nchunk, _, _ = ps.shape
    ns = SEQ_PER_STEP
    cur_map = lambda b, i: (b, nchunk - 1 - i, 0, 0)
    const2 = lambda b, i: (0, 0)
    chunk = lambda w: pl.BlockSpec((ns, 1, SSD_CHUNK, w), cur_map)
    st_spec = pl.BlockSpec((ns, 2 * SSD_STATE, SSD_INNER), lambda b, i: (b, 0, 0))
    return pl.pallas_call(
        functools.partial(_ssd_kernel, reverse=True, nchunk=nchunk),
        grid=(bsz // ns, nchunk),
        in_specs=[chunk(SSD_INNER), chunk(SSD_CONV_DIM), chunk(2 * LANE), chunk(SSD_INNER),
                  pl.BlockSpec((1, SSD_INNER), const2), pl.BlockSpec((1, SSD_INNER), const2), st_spec],
        out_specs=[chunk(SSD_INNER), st_spec],
        out_shape=[jax.ShapeDtypeStruct((bsz, nchunk, SSD_CHUNK, SSD_INNER), F32),
                   jax.ShapeDtypeStruct((bsz, 2 * SSD_STATE, SSD_INNER), F32)],
        scratch_shapes=[pltpu.VMEM((ns, 2 * SSD_STATE, SSD_INNER), F32)],
        compiler_params=_cp("arbitrary", "arbitrary"),
        name="ssd_bwd",
    )(ps, act, dta, y_f, dx, nw, s0)


def _hy_pre_kernel(cur_ref, prev_ref, next_ref, w_ref, b_ref, vg_ref, x0_ref, *, nblk):
    i = pl.program_id(1)
    x = cur_ref[0]
    tb = x.shape[0]
    has_prev = (i > 0).astype(F32)
    has_next = (i < nblk - 1).astype(F32)
    prow = prev_ref[0][7:8] * has_prev
    nrow = next_ref[0][0:1] * has_next
    ridx = lax.broadcasted_iota(jnp.int32, (tb, 1), 0)
    xm = jnp.where(ridx == 0, prow, pltpu.roll(x, 1, 0))
    xp = jnp.where(ridx == tb - 1, nrow, pltpu.roll(x, tb - 1, 0))
    u = w_ref[0:1] * xm + w_ref[1:2] * x + w_ref[2:3] * xp + b_ref[...]
    vg = u[:, 2 * HY_WIDTH:] * u[:, HY_WIDTH:2 * HY_WIDTH]
    for hf in range(HY_WIDTH // LANE):
        x0_ref[0, hf] = u[:, hf * LANE:(hf + 1) * LANE]
        vg_ref[0, hf] = vg[:, hf * LANE:(hf + 1) * LANE]


def _hy_pre_call(ph, w8, b):
    bsz, L, _ = ph.shape
    tb = min(512, L)
    nblk = L // tb
    r8 = tb // 8
    return pl.pallas_call(
        functools.partial(_hy_pre_kernel, nblk=nblk),
        grid=(bsz, nblk),
        in_specs=[pl.BlockSpec((1, tb, PH_W), lambda b_, i: (b_, i, 0)),
                  pl.BlockSpec((1, 8, PH_W), lambda b_, i: (b_, jnp.maximum(i * r8 - 1, 0), 0)),
                  pl.BlockSpec((1, 8, PH_W), lambda b_, i: (b_, jnp.minimum((i + 1) * r8, L // 8 - 1), 0)),
                  pl.BlockSpec((8, PH_W), lambda b_, i: (0, 0)),
                  pl.BlockSpec((1, PH_W), lambda b_, i: (0, 0))],
        out_specs=[pl.BlockSpec((1, 2, tb, LANE), lambda b_, i: (b_, 0, i, 0)),
                   pl.BlockSpec((1, 2, tb, LANE), lambda b_, i: (b_, 0, i, 0))],
        out_shape=[jax.ShapeDtypeStruct((bsz, 2, L, LANE), F32),
                   jax.ShapeDtypeStruct((bsz, 2, L, LANE), F32)],
        compiler_params=_cp("arbitrary", "arbitrary"),
        name="hy_pre",
    )(ph, ph, ph, w8, b)


def _hy_filter_kernel(z_ref, w1_ref, b1_ref, w2_ref, b2_ref, w3_ref, fr_ref, dec_ref, h_ref):
    z = z_ref[...]
    fr = fr_ref[...]
    h1 = jnp.sin(fr * (_dot_hp(z, w1_ref[...]) + b1_ref[...]))
    h2 = jnp.sin(fr * (_dot_hp(h1, w2_ref[...]) + b2_ref[...]))
    h = _dot_hp(h2, w3_ref[...])
    win = jnp.exp(-2.0 * jnp.abs(z[:, 0:1]) * dec_ref[...])
    h_ref[...] = h * win


def _hy_features(L):
    t = jnp.arange(L, dtype=F32)
    rel = (t - (L // 2)) / L
    bands = jnp.linspace(1e-4, HY_BANDS - 1, HY_BANDS, dtype=F32)
    ang = 2.0 * math.pi * rel[:, None] * bands
    z = jnp.concatenate([rel[:, None], jnp.cos(ang), -jnp.sin(ang)], axis=-1)
    return jnp.pad(z, ((0, 0), (0, LANE - HY_EMB)))


def _hy_filter_call(L, w1p, b1p, w2p, b2p, w3p, frp, dec):
    z = _hy_features(L)
    tl = min(1024, L)
    c2 = lambda i: (0, 0)
    return pl.pallas_call(
        _hy_filter_kernel,
        grid=(L // tl,),
        in_specs=[pl.BlockSpec((tl, LANE), lambda i: (i, 0)),
                  pl.BlockSpec((LANE, LANE), c2), pl.BlockSpec((1, LANE), c2),
                  pl.BlockSpec((LANE, LANE), c2), pl.BlockSpec((1, LANE), c2),
                  pl.BlockSpec((LANE, HY_WIDTH), c2), pl.BlockSpec((1, LANE), c2),
                  pl.BlockSpec((1, HY_WIDTH), c2)],
        out_specs=pl.BlockSpec((tl, HY_WIDTH), lambda i: (i, 0)),
        out_shape=jax.ShapeDtypeStruct((L, HY_WIDTH), F32),
        compiler_params=_cp("arbitrary"),
        name="hy_filter",
    )(z, w1p, b1p, w2p, b2p, w3p, frp, dec)


@functools.lru_cache(maxsize=None)
def _fft_consts():
    n1 = FFT_N1
    half = n1 // 2
    k = np.arange(n1, dtype=np.float64)
    n2 = k[:, None, None]
    k1 = k[None, :, None]
    nn = np.arange(half, dtype=np.float64)[None, None, :]
    ang = -2.0 * np.pi * (n2 * k1 / FFT_N + nn * k1 / n1)
    mr, mi = np.cos(ang), np.sin(ang)
    m1 = np.concatenate([np.concatenate([mr, -mi], axis=2), np.concatenate([mi, mr], axis=2)], axis=1)
    ang2 = -2.0 * np.pi * np.outer(k, k) / n1
    fr, fi = np.cos(ang2), np.sin(ang2)
    f2 = np.block([[fr, -fi], [fi, fr]])
    f2c = np.block([[fr, fi], [-fi, fr]])
    no = (np.arange(half, dtype=np.float64) + n1 // 4)[None, :, None]
    kk = k[None, None, :]
    ang3 = 2.0 * np.pi * (n2 * kk / FFT_N + no * kk / n1)
    ir, ii = np.cos(ang3) / FFT_N, np.sin(ang3) / FFT_N
    m3 = np.concatenate([np.concatenate([ir, -ii], axis=2), np.concatenate([ii, ir], axis=2)], axis=1)
    return tuple(np.asarray(m, dtype=np.float32) for m in (m1, f2, f2c, m3))


FFT_NB = 16
FFT_HALF = FFT_N1 // 2


def _strided_rows(ref2d, start, n):
    return ref2d[pl.ds(start, n, stride=FFT_NB), :]


def _tok_rows(ref, s, t):
    return jnp.concatenate([ref[s, hf, :, t, :] for hf in range(2)], axis=1)


def _stage_f32(dst, src_ref):
    v = src_ref[0].astype(F32).reshape(FFT_N1 * FFT_NB, HY_WIDTH)
    dst[0] = v[:, :LANE]
    dst[1] = v[:, LANE:]


def _staged_rows(scr, t):
    return jnp.concatenate([scr[hf, pl.ds(t, FFT_N1, stride=FFT_NB), :] for hf in range(2)], axis=1)


def _fft1_kernel(u_ref, m_ref, a_ref):
    for t in range(FFT_NB):
        rhs = _bf(jnp.concatenate([_tok_rows(u_ref, 0, t), _tok_rows(u_ref, 1, t)], axis=0))
        a_ref[0, t] = _bf(_dot(m_ref[t], rhs))


def _fft1_call(u, m1):
    npair = u.shape[0] // 2
    return pl.pallas_call(
        _fft1_kernel,
        grid=(npair, FFT_N1 // FFT_NB),
        in_specs=[pl.BlockSpec((2, 2, FFT_HALF, FFT_NB, LANE), lambda p, j: (p, 0, 0, j, 0)),
                  pl.BlockSpec((FFT_NB, 2 * FFT_N1, FFT_N1), lambda p, j: (j, 0, 0))],
        out_specs=pl.BlockSpec((1, FFT_NB, 2 * FFT_N1, HY_WIDTH), lambda p, j: (p, j, 0, 0)),
        out_shape=jax.ShapeDtypeStruct((npair, FFT_N1, 2 * FFT_N1, HY_WIDTH), BF16),
        compiler_params=_cp("arbitrary", "arbitrary"),
        name="hy_fft1",
    )(u, m1)


def _fft2_kernel(*refs, spectrum):
    if spectrum:
        ar_ref, ai_ref, f_ref, o_ref, sr, si = refs
    else:
        ar_ref, ai_ref, f_ref, fc_ref, h_ref, o_ref, sr, si = refs
    _stage_f32(sr, ar_ref)
    _stage_f32(si, ai_ref)
    for t in range(FFT_NB):
        rhs = _bf(jnp.concatenate([_staged_rows(sr, t), _staged_rows(si, t)], axis=0))
        x = _dot(f_ref[...], rhs)
        if spectrum:
            o_ref[t] = x
        else:
            xr, xi = x[:FFT_N1], x[FFT_N1:]
            hr, hi = h_ref[t, :FFT_N1], h_ref[t, FFT_N1:]
            y = jnp.concatenate([xr * hr - xi * hi, xr * hi + xi * hr], axis=0)
            o_ref[0, t] = _bf(_dot(fc_ref[...], _bf(y)))


def _fft2_call(a, f2, f2c, hspec):
    npair = a.shape[0]
    av = a
    nj = FFT_N1 // FFT_NB
    c2 = lambda p, j: (0, 0)
    in_specs = [pl.BlockSpec((1, FFT_N1, FFT_NB, HY_WIDTH), lambda p, j: (p, 0, j, 0)),
                pl.BlockSpec((1, FFT_N1, FFT_NB, HY_WIDTH), lambda p, j: (p, 0, nj + j, 0)),
                pl.BlockSpec((2 * FFT_N1, 2 * FFT_N1), c2)]
    staging = [pltpu.VMEM((2, FFT_N1 * FFT_NB, LANE), F32), pltpu.VMEM((2, FFT_N1 * FFT_NB, LANE), F32)]
    if hspec is None:
        return pl.pallas_call(
            functools.partial(_fft2_kernel, spectrum=True),
            grid=(1, nj),
            in_specs=in_specs,
            out_specs=pl.BlockSpec((FFT_NB, 2 * FFT_N1, HY_WIDTH), lambda p, j: (j, 0, 0)),
            out_shape=jax.ShapeDtypeStruct((FFT_N1, 2 * FFT_N1, HY_WIDTH), F32),
            scratch_shapes=staging,
            compiler_params=_cp("arbitrary", "arbitrary"),
            name="hy_fft2_spec",
        )(av, av, f2)
    in_specs += [pl.BlockSpec((2 * FFT_N1, 2 * FFT_N1), c2),
                 pl.BlockSpec((FFT_NB, 2 * FFT_N1, HY_WIDTH), lambda p, j: (j, 0, 0))]
    return pl.pallas_call(
        functools.partial(_fft2_kernel, spectrum=False),
        grid=(npair, nj),
        in_specs=in_specs,
        out_specs=pl.BlockSpec((1, FFT_NB, 2 * FFT_N1, HY_WIDTH), lambda p, j: (p, j, 0, 0)),
        out_shape=jax.ShapeDtypeStruct((npair, FFT_N1, 2 * FFT_N1, HY_WIDTH), BF16),
        scratch_shapes=staging,
        compiler_params=_cp("arbitrary", "arbitrary"),
        name="hy_fft2",
    )(av, av, f2, f2c, hspec)


def _fft3_kernel(br_ref, bi_ref, m_ref, vg_ref, x0_ref, bias_ref, y_ref, sr, si):
    _stage_f32(sr, br_ref)
    _stage_f32(si, bi_ref)
    for t in range(FFT_NB):
        rhs = _bf(jnp.concatenate([_staged_rows(sr, t), _staged_rows(si, t)], axis=0))
        out = _dot(m_ref[t], rhs)
        for s in range(2):
            conv = out[s * FFT_HALF:(s + 1) * FFT_HALF]
            y = (conv + _tok_rows(vg_ref, s, t) * bias_ref[...]) * _tok_rows(x0_ref, s, t)
            for hf in range(2):
                y_ref[s, hf, :, t, :] = y[:, hf * LANE:(hf + 1) * LANE]


def _fft3_call(bmat, m3, vg, x0, bias):
    npair = bmat.shape[0]
    nj = FFT_N1 // FFT_NB
    tok_spec = pl.BlockSpec((2, 2, FFT_HALF, FFT_NB, LANE), lambda p, j: (p, 0, 0, j, 0))
    return pl.pallas_call(
        _fft3_kernel,
        grid=(npair, nj),
        in_specs=[pl.BlockSpec((1, FFT_N1, FFT_NB, HY_WIDTH), lambda p, j: (p, 0, j, 0)),
                  pl.BlockSpec((1, FFT_N1, FFT_NB, HY_WIDTH), lambda p, j: (p, 0, nj + j, 0)),
                  pl.BlockSpec((FFT_NB, FFT_N1, 2 * FFT_N1), lambda p, j: (j, 0, 0)),
                  tok_spec, tok_spec,
                  pl.BlockSpec((1, HY_WIDTH), lambda p, j: (0, 0))],
        out_specs=tok_spec,
        out_shape=jax.ShapeDtypeStruct(vg.shape, F32),
        scratch_shapes=[pltpu.VMEM((2, FFT_N1 * FFT_NB, LANE), F32), pltpu.VMEM((2, FFT_N1 * FFT_NB, LANE), F32)],
        compiler_params=_cp("arbitrary", "arbitrary"),
        name="hy_fft3",
    )(bmat, bmat, m3, vg, x0, bias)


def _hy_direct_kernel(vg_ref, x0_ref, h_ref, bias_ref, y_ref, pad_scr, sh_scr):
    L = vg_ref.shape[2]
    u = jnp.concatenate([vg_ref[0, 0], vg_ref[0, 1]], axis=1)
    pad_scr[...] = jnp.zeros_like(pad_scr)
    pad_scr[L:2 * L, :] = u
    top = L + L // 2
    acc = jnp.zeros((L, HY_WIDTH), F32)
    for r in range(8):
        sh_scr[...] = pad_scr[r:r + 3 * L - 8, :]
        a_lo = -(-(top - L + 1 - r) // 8)
        a_hi = (top - r) // 8

        def body(a, acc, r=r):
            m = top - (a * 8 + r)
            return acc + h_ref[pl.ds(m, 1), :] * sh_scr[pl.ds(pl.multiple_of(a * 8, 8), L), :]

        acc = lax.fori_loop(a_lo, a_hi + 1, body, acc)
    y = (acc + u * bias_ref[...]) * jnp.concatenate([x0_ref[0, 0], x0_ref[0, 1]], axis=1)
    y_ref[0, 0] = y[:, :LANE]
    y_ref[0, 1] = y[:, LANE:]


def _hy_direct_call(vg, x0, h, bias):
    bsz, _, L, _ = vg.shape
    tok = pl.BlockSpec((1, 2, L, LANE), lambda b: (b, 0, 0, 0))
    return pl.pallas_call(
        _hy_direct_kernel,
        grid=(bsz,),
        in_specs=[tok, tok, pl.BlockSpec((L, HY_WIDTH), lambda b: (0, 0)),
                  pl.BlockSpec((1, HY_WIDTH), lambda b: (0, 0))],
        out_specs=tok,
        out_shape=jax.ShapeDtypeStruct(vg.shape, F32),
        scratch_shapes=[pltpu.VMEM((3 * L, HY_WIDTH), F32), pltpu.VMEM((3 * L - 8, HY_WIDTH), F32)],
        compiler_params=_cp("arbitrary"),
        name="hy_direct",
    )(vg, x0, h, bias)


def _out_kernel(x_ref, yg_ref, yh_ref, ys_ref, mod_ref, g2_ref, gf_ref, wo_ref, w1_ref, w3_ref, w2_ref,
                o_ref, *scr, final, col_major):
    m = mod_ref[0]
    if col_major:
        ys_scr, = scr
        for c in range(GRID_W):
            for k in range(SSD_INNER // LANE):
                ys_scr[k, pl.ds(c, ROWS_PER_TILE, stride=GRID_W), :] = ys_ref[0, c, :, k * LANE:(k + 1) * LANE]
        ys = jnp.concatenate([ys_scr[k] for k in range(SSD_INNER // LANE)], axis=1)
    else:
        ys = ys_ref[...]
    yh = jnp.concatenate([yh_ref[0, 0], yh_ref[0, 1]], axis=1)
    mix = (_dot(_bf(yg_ref[...]), wo_ref[0:GLA_V])
           + _dot(_bf(yh), wo_ref[GLA_V:GLA_V + HY_WIDTH])
           + _dot(_bf(ys), wo_ref[GLA_V + HY_WIDTH:]))
    x1 = x_ref[...] + m[2:3] * mix
    ms = jnp.mean(x1 * x1, axis=-1, keepdims=True)
    h = x1 * lax.rsqrt(ms + EPS) * g2_ref[...]
    h = _bf(h * (1.0 + m[4:5]) + m[3:4])
    ffn = None
    for j in range(D_FF // FF_TILE):
        cols = slice(j * FF_TILE, (j + 1) * FF_TILE)
        a = _dot(h, w1_ref[:, cols])
        b = _dot(h, w3_ref[:, cols])
        part = _dot(_bf(_silu(a) * b), w2_ref[cols, :])
        ffn = part if ffn is None else ffn + part
    x2 = x1 + m[5:6] * ffn
    if final:
        ms = jnp.mean(x2 * x2, axis=-1, keepdims=True)
        x2 = x2 * lax.rsqrt(ms + EPS) * gf_ref[...]
    o_ref[...] = x2


def _out_call(x2, yg, yh, ys, mods, mod_row, g2, gf, wo, w1, w3, w2, final, col_major):
    t = x2.shape[0]
    seq = yh.shape[2]
    tm = min(TOK_TILE, seq)
    tiles = seq // tm
    c2 = lambda i: (0, 0)
    if col_major:
        assert tm == TOK_TILE
        ys_spec = pl.BlockSpec((1, GRID_W, ROWS_PER_TILE, SSD_INNER), lambda i: (i // tiles, 0, i % tiles, 0))
        extra = [pltpu.VMEM((SSD_INNER // LANE, tm, LANE), F32)]
    else:
        ys = ys.reshape(t, SSD_INNER)
        ys_spec = pl.BlockSpec((tm, SSD_INNER), lambda i: (i, 0))
        extra = []
    resident = lambda shape: pl.BlockSpec(shape, c2, pipeline_mode=pl.Buffered(1))
    return pl.pallas_call(
        functools.partial(_out_kernel, final=final, col_major=col_major),
        grid=(t // tm,),
        in_specs=[pl.BlockSpec((tm, D_MODEL), lambda i: (i, 0)),
                  pl.BlockSpec((tm, GLA_V), lambda i: (i, 0)),
                  pl.BlockSpec((1, 2, tm, LANE), lambda i: (i // tiles, 0, i % tiles, 0)),
                  ys_spec,
                  pl.BlockSpec((1, 6, D_MODEL), lambda i: (mod_row(i), 0, 0)),
                  pl.BlockSpec((1, D_MODEL), c2),
                  pl.BlockSpec((1, D_MODEL), c2),
                  resident((D_MODEL, D_MODEL)),
                  resident((D_MODEL, D_FF)),
                  resident((D_MODEL, D_FF)),
                  resident((D_FF, D_MODEL))],
        out_specs=pl.BlockSpec((tm, D_MODEL), lambda i: (i, 0)),
        out_shape=jax.ShapeDtypeStruct((t, D_MODEL), F32),
        scratch_shapes=extra,
        compiler_params=_cp("arbitrary"),
        name="out_ffn",
    )(x2, yg, yh, ys, mods, g2.reshape(1, D_MODEL), gf.reshape(1, D_MODEL), wo, w1, w3, w2)


def _pack_w_in(w_in):
    z = lambda n: jnp.zeros(w_in.shape[:2] + (n,), w_in.dtype)
    q, k = w_in[..., 0:192], w_in[..., 192:384]
    v, g = w_in[..., 384:768], w_in[..., 768:1152]
    gk = w_in[..., 1152:1184]
    hy = w_in[..., 1184:1952]
    zz = w_in[..., 1952:2336]
    xbc = w_in[..., 2336:3232]
    dt = w_in[..., 3232:3244]
    packed = jnp.concatenate([q, z(64), k, z(64), v, g, gk, z(96), hy, zz, xbc, dt, z(116)], axis=-1)
    return packed.astype(BF16)


def _pad_to(a, shape):
    return jnp.pad(a, [(0, s - d) for d, s in zip(a.shape, shape)])


def _mixers(pg, ph, ps, lw, states):
    g_f0, g_b0, m_f0, m_b0 = states
    o_f, g_f = _gla_call(pg, None, lw['wgk_f'], lw['bgk_f'], None, g_f0, False)
    gla_y, g_b = _gla_call(pg, o_f, lw['wgk_b'], lw['bgk_b'], lw['gla_nw'], g_b0, True)

    vg, x0 = _hy_pre_call(ph, lw['hy_sw'], lw['hy_sb'])
    L = ph.shape[1]
    h = _hy_filter_call(L, lw['hy_w1'], lw['hy_b1'], lw['hy_w2'], lw['hy_b2'], lw['hy_w3'], lw['hy_freq'],
                        lw['hy_decay'])
    if L == FFT_N // 2:
        m1, f2, f2c, m3 = lw['fft']
        tok5 = lambda t: t.reshape(t.shape[0], 2, FFT_HALF, FFT_N1, LANE)
        hsplit = h.reshape(L, 2, LANE).transpose(1, 0, 2)
        hspec = _fft2_call(_fft1_call(tok5(jnp.stack([hsplit, jnp.zeros_like(hsplit)])), m1), f2, f2c, None)
        vg5 = tok5(vg)
        bmat = _fft2_call(_fft1_call(vg5, m1), f2, f2c, hspec)
        hy_y = _fft3_call(bmat, m3, vg5, tok5(x0), lw['hy_bias']).reshape(vg.shape)
    else:
        hy_y = _hy_direct_call(vg, x0, h, lw['hy_bias'])

    y_f, act, dta, m_f = _ssd_fwd_call(ps, lw['ssd_cw'], lw['ssd_cb'], lw['ssd_dtb'], lw['ssd_alog'], m_f0)
    ssd_y, m_b = _ssd_bwd_call(ps, act, dta, y_f, lw['ssd_dx'], lw['ssd_nw'], m_b0)
    return (gla_y, hy_y, ssd_y), (g_f, g_b, m_f, m_b)


def kernel(x, c, ctx, c_ctx, mod_w, mod_b, norm1_g, norm2_g, w_in, gla_gk_w_f, gla_gk_b_f, gla_gk_w_b, gla_gk_b_b, gla_norm_w, hy_short_w, hy_short_b, hy_w1, hy_b1, hy_w2, hy_b2, hy_w3, hy_freq, hy_decay, hy_bias, ssd_conv_w, ssd_conv_b, ssd_dt_bias_f, ssd_dt_bias_b, ssd_a_log_f, ssd_a_log_b, ssd_d, ssd_norm_w, w_out, ffn_w1, ffn_w3, ffn_w2, final_g):
    bsz, seq, _ = x.shape
    lc = ctx.shape[1]
    c8 = jnp.concatenate([c, c_ctx[None], jnp.zeros((8 - bsz - 1, D_MODEL), F32)], axis=0)
    mods_all = _mod_call(c8, mod_w, mod_b).reshape(DEPTH, 8, 6, D_MODEL)

    w_in_p = _pack_w_in(w_in)
    fft_mats = tuple(_bf(jnp.asarray(m)) for m in _fft_consts())
    w_out_b, w1_b, w3_b, w2_b = _bf(w_out), _bf(ffn_w1), _bf(ffn_w3), _bf(ffn_w2)

    xt = x.reshape(bsz * seq, D_MODEL)
    ct = ctx.reshape(bsz * lc, D_MODEL)
    tiles_per_seq = seq // TOK_TILE
    row_x = lambda i: i // tiles_per_seq
    row_c = lambda i: bsz

    zeros_states = (jnp.zeros((bsz, GLA_DV, 256), F32), jnp.zeros((bsz, GLA_DV, 256), F32),
                    jnp.zeros((bsz, 2 * SSD_STATE, SSD_INNER), F32),
                    jnp.zeros((bsz, 2 * SSD_STATE, SSD_INNER), F32))

    for l in range(DEPTH):
        def gkw(w, off):
            return _bf(_pad_to(jnp.pad(w, ((off, 0), (0, 0))), (LANE, 256)))
        lw = {
            'wgk_f': gkw(gla_gk_w_f[l], 0), 'wgk_b': gkw(gla_gk_w_b[l], GLA_LOWRANK),
            'bgk_f': _pad_to(gla_gk_b_f[l][None], (1, 256)), 'bgk_b': _pad_to(gla_gk_b_b[l][None], (1, 256)),
            'gla_nw': jnp.tile(gla_norm_w[l], GLA_HEADS)[None],
            'hy_sw': _pad_to(hy_short_w[l], (8, PH_W)), 'hy_sb': hy_short_b[l][None],
            'hy_w1': _pad_to(hy_w1[l], (LANE, LANE)), 'hy_b1': _pad_to(hy_b1[l][None], (1, LANE)),
            'hy_w2': _pad_to(hy_w2[l], (LANE, LANE)), 'hy_b2': _pad_to(hy_b2[l][None], (1, LANE)),
            'hy_w3': _pad_to(hy_w3[l], (LANE, HY_WIDTH)), 'hy_freq': _pad_to(hy_freq[l][None], (1, LANE)),
            'hy_decay': hy_decay[l][None], 'hy_bias': hy_bias[l][None],
            'ssd_cw': _pad_to(ssd_conv_w[l], (8, SSD_CONV_DIM)), 'ssd_cb': ssd_conv_b[l][None],
            'ssd_dtb': _pad_to(jnp.concatenate([ssd_dt_bias_f[l], ssd_dt_bias_b[l]])[None], (1, LANE)),
            'ssd_alog': _pad_to(jnp.concatenate([ssd_a_log_f[l], ssd_a_log_b[l]])[None], (1, LANE)),
            'ssd_dx': jnp.repeat(ssd_d[l], SSD_P)[None], 'ssd_nw': ssd_norm_w[l][None],
        }
        lw['fft'] = fft_mats
        mods = mods_all[l]
        yc, ctx_states = _mixers(*_in_call(ct, mods, row_c, norm1_g[l], w_in_p[l], bsz, False), lw, zeros_states)
        yx, _ = _mixers(*_in_call(xt, mods, row_x, norm1_g[l], w_in_p[l], bsz, True), lw, ctx_states)
        last = l == DEPTH - 1
        ffn = (w_out_b[l], w1_b[l], w3_b[l], w2_b[l])
        xt = _out_call(xt, yx[0].reshape(-1, GLA_V), yx[1], yx[2],
                       mods, row_x, norm2_g[l], final_g, *ffn, last, True)
        if not last:
            ct = _out_call(ct, yc[0].reshape(-1, GLA_V), yc[1], yc[2],
                           mods, row_c, norm2_g[l], final_g, *ffn, False, False)
    return xt.reshape(bsz, seq, D_MODEL)
```

```python
import functools
import math

import numpy as np
import jax
import jax.numpy as jnp
from jax import lax
from jax.experimental import pallas as pl
from jax.experimental.pallas import tpu as pltpu

F32 = jnp.float32
BF16 = jnp.bfloat16

D_MODEL = 1024
DEPTH = 2
GRID_W = 64
EPS = 1e-6
GLA_V = 384
GLA_DV = 64
GLA_HEADS = 6
GLA_DK = 32
GLA_QK = 192
GLA_LOWRANK = 16
GLA_TAU = 16.0
GLA_CHUNK = 64
GLA_BLOCK = 256
HY_WIDTH = 256
HY_BANDS = 16
HY_EMB = 1 + 2 * HY_BANDS
HY_ORDER = 64
SSD_INNER = 384
SSD_HEADS = 6
SSD_GROUPS = 2
SSD_HG = 3
SSD_P = 64
SSD_STATE = 128
SSD_CONV_DIM = SSD_INNER + 2 * SSD_GROUPS * SSD_STATE
SSD_CHUNK = 128
D_FF = 2816
FF_TILE = 1408
PG_W = 1408
PH_W = 768
PS_W = 1408
LANE = 128
FFT_N1 = 128
FFT_N = FFT_N1 * FFT_N1

VMEM_LIMIT = 56 * 1024 * 1024


def _cp(*sem):
    return pltpu.CompilerParams(dimension_semantics=sem, vmem_limit_bytes=VMEM_LIMIT)


def _bf(x):
    return x.astype(BF16)


def _dot(a, b):
    return jnp.dot(a, b, preferred_element_type=F32)


def _dot_nt(a, b):
    return lax.dot_general(a, b, (((1,), (1,)), ((), ())), preferred_element_type=F32)


def _dot_tn(a, b):
    return lax.dot_general(a, b, (((0,), (0,)), ((), ())), preferred_element_type=F32)


def _split3(x):
    hi = _bf(x)
    r1 = x - hi.astype(F32)
    mid = _bf(r1)
    lo = _bf(r1 - mid.astype(F32))
    return hi, mid, lo


def _dot_sel(sel_bf, x):
    hi, mid, lo = _split3(x)
    return _dot(sel_bf, hi) + _dot(sel_bf, mid) + _dot(sel_bf, lo)


def _dot_sel2(sel_bf, x):
    hi = _bf(x)
    lo = _bf(x - hi.astype(F32))
    return _dot(sel_bf, hi) + _dot(sel_bf, lo)


def _dot_hp(a, b):
    ah = _bf(a)
    al = _bf(a - ah.astype(F32))
    bh = _bf(b)
    bl = _bf(b - bh.astype(F32))
    return _dot(ah, bh) + _dot(ah, bl) + _dot(al, bh)


def _silu(x):
    return x * jax.nn.sigmoid(x)


def _softplus(x):
    return jnp.maximum(x, 0.0) + jnp.log1p(jnp.exp(-jnp.abs(x)))


def _log_sigmoid(x):
    return jnp.minimum(x, 0.0) - jnp.log1p(jnp.exp(-jnp.abs(x)))


def _mod_kernel(c_ref, w_ref, b_ref, o_ref):
    act = _silu(c_ref[...])
    o_ref[0] = _dot(_bf(act), _bf(w_ref[0])) + b_ref[0]


def _mod_call(c8, mod_w, mod_b):
    nt = 1536
    n = mod_w.shape[-1]
    return pl.pallas_call(
        _mod_kernel,
        grid=(DEPTH, n // nt),
        in_specs=[pl.BlockSpec((8, D_MODEL), lambda l, j: (0, 0)),
                  pl.BlockSpec((1, D_MODEL, nt), lambda l, j: (l, 0, j)),
                  pl.BlockSpec((1, 1, nt), lambda l, j: (l, 0, j))],
        out_specs=pl.BlockSpec((1, 8, nt), lambda l, j: (l, 0, j)),
        out_shape=jax.ShapeDtypeStruct((DEPTH, 8, n), F32),
        compiler_params=_cp("arbitrary", "arbitrary"),
        name="adaln_mod",
    )(c8, mod_w, mod_b.reshape(DEPTH, 1, n))


TOK_TILE = 512
ROW_GROUPS = 2
ROWS_PER_TILE = TOK_TILE // GRID_W


def _in_kernel(x_ref, mod_ref, g_ref, w_ref, og_ref, oh_ref, os_ref, *scr, col_major):
    m = mod_ref[0]
    tm = x_ref.shape[0]
    rgs = [slice(r * (tm // ROW_GROUPS), (r + 1) * (tm // ROW_GROUPS)) for r in range(ROW_GROUPS)]
    x = [x_ref[r, :] for r in rgs]
    ms = [jnp.mean(t * t, axis=-1, keepdims=True) for t in x]
    y = [t * lax.rsqrt(s + EPS) * g_ref[...] for t, s in zip(x, ms)]
    h = [_bf(t * (1.0 + m[1:2]) + m[0:1]) for t in y]
    for r, t in zip(rgs, h):
        og_ref[r, :] = _bf(_dot(t, w_ref[:, 0:PG_W]))
    for r, t in zip(rgs, h):
        oh_ref[r, :] = _bf(_dot(t, w_ref[:, PG_W:PG_W + PH_W]))
    ps = jnp.concatenate([_dot(t, w_ref[:, PG_W + PH_W:]) for t in h], axis=0)
    if col_major:
        ps_scr, = scr
        for k in range(PS_W // LANE):
            for r in range(ROWS_PER_TILE):
                ps_scr[k, pl.ds(r, GRID_W, stride=ROWS_PER_TILE), :] = ps[r * GRID_W:(r + 1) * GRID_W,
                                                                         k * LANE:(k + 1) * LANE]
        for k in range(PS_W // LANE):
            os_ref[0, :, :, k * LANE:(k + 1) * LANE] = ps_scr[k].reshape(GRID_W, ROWS_PER_TILE, LANE)
    else:
        os_ref[...] = ps


def _in_call(x2, mods, mod_row, g, w, bsz, col_major):
    t = x2.shape[0]
    tm = TOK_TILE
    wtot = PG_W + PH_W + PS_W
    seq = t // bsz
    if col_major:
        tiles = seq // tm
        assert seq == GRID_W * SSD_CHUNK and ROWS_PER_TILE == 8
        os_spec = pl.BlockSpec((1, GRID_W, ROWS_PER_TILE, PS_W), lambda i: (i // tiles, 0, i % tiles, 0))
        os_shape = jax.ShapeDtypeStruct((bsz, GRID_W, SSD_CHUNK, PS_W), F32)
        scratch = [pltpu.VMEM((PS_W // LANE, tm, LANE), F32)]
    else:
        os_spec = pl.BlockSpec((tm, PS_W), lambda i: (i, 0))
        os_shape = jax.ShapeDtypeStruct((t, PS_W), F32)
        scratch = []
    pg, ph, ps = pl.pallas_call(
        functools.partial(_in_kernel, col_major=col_major),
        grid=(t // tm,),
        in_specs=[pl.BlockSpec((tm, D_MODEL), lambda i: (i, 0)),
                  pl.BlockSpec((1, 6, D_MODEL), lambda i: (mod_row(i), 0, 0)),
                  pl.BlockSpec((1, D_MODEL), lambda i: (0, 0)),
                  pl.BlockSpec((D_MODEL, wtot), lambda i: (0, 0), pipeline_mode=pl.Buffered(1))],
        out_specs=[pl.BlockSpec((tm, PG_W), lambda i: (i, 0)),
                   pl.BlockSpec((tm, PH_W), lambda i: (i, 0)),
                   os_spec],
        out_shape=[jax.ShapeDtypeStruct((t, PG_W), BF16),
                   jax.ShapeDtypeStruct((t, PH_W), BF16),
                   os_shape],
        scratch_shapes=scratch,
        compiler_params=_cp("arbitrary"),
        name="in_proj",
    )(x2, mods, g.reshape(1, D_MODEL), w)
    return (pg.reshape(bsz, seq, PG_W), ph.reshape(bsz, seq, PH_W),
            ps.reshape(bsz, seq // SSD_CHUNK, SSD_CHUNK, PS_W))


def _gla_kernel(*refs, reverse, nblk):
    if reverse:
        (p_ref, of_ref, wgk_ref, bgk_ref, nw_ref, s0_ref, y_ref, sf_ref, st_scr) = refs
    else:
        (p_ref, wgk_ref, bgk_ref, s0_ref, of_ref, sf_ref, st_scr) = refs
    i = pl.program_id(1)
    tb = GLA_BLOCK

    @pl.when(i == 0)
    def _():
        st_scr[...] = s0_ref[...]

    ri = lax.broadcasted_iota(jnp.int32, (tb, tb), 0)
    ci = lax.broadcasted_iota(jnp.int32, (tb, tb), 1)
    same = (ri // GLA_CHUNK) == (ci // GLA_CHUNK)
    tri = same & ((ci >= ri) if reverse else (ci <= ri))
    tri_bf = jnp.where(tri, 1.0, 0.0).astype(BF16)
    lane = lax.broadcasted_iota(jnp.int32, (1, 256), 1)
    hms = [(lane // GLA_DK) == h for h in range(GLA_HEADS)]

    nseq = p_ref.shape[0]
    seqs = range(nseq)
    ps = [p_ref[s_] for s_ in seqs]
    os_, sts = _gla_blocks(ps, [st_scr[s_] for s_ in seqs], wgk_ref[...], bgk_ref[...], tri, tri_bf, lane, hms,
                           reverse)
    for s_ in seqs:
        st_scr[s_] = sts[s_]

    @pl.when(i == nblk - 1)
    def _():
        for s_ in seqs:
            sf_ref[s_] = sts[s_]

    if not reverse:
        for s_ in seqs:
            of_ref[s_] = os_[s_]
    else:
        r2 = lax.broadcasted_iota(jnp.int32, (GLA_V, GLA_V), 0) // GLA_DV
        c2 = lax.broadcasted_iota(jnp.int32, (GLA_V, GLA_V), 1) // GLA_DV
        ind = jnp.where(r2 == c2, 1.0, 0.0).astype(BF16)
        ot = [of_ref[s_] + os_[s_] for s_ in seqs]
        sq = [t * t for t in ot]
        sh = [_bf(t) for t in sq]
        sl = [_bf(sq[s_] - sh[s_].astype(F32)) for s_ in seqs]
        ms = [(_dot(sh[s_], ind) + _dot(sl[s_], ind)) * (1.0 / GLA_DV) for s_ in seqs]
        for s_ in seqs:
            g = ps[s_][:, 896:1280].astype(F32)
            y_ref[s_] = ot[s_] * lax.rsqrt(ms[s_] + EPS) * nw_ref[...] * _silu(g)


def _gla_blocks(ps, sts, wgk, bgk, tri, tri_bf, lane, hms, reverse):
    tb = GLA_BLOCK
    nch = tb // GLA_CHUNK
    seqs = range(len(ps))
    k = [p[:, 256:512].astype(F32) for p in ps]
    pre = [_dot(p[:, 1280:1408], wgk) + bgk for p in ps]
    la = [_log_sigmoid(t) * (1.0 / GLA_TAU) for t in pre]
    b = [_dot_sel2(tri_bf, t) for t in la]
    qd = [_bf(ps[s][:, 0:256].astype(F32) * ((GLA_DK ** -0.5) * jnp.exp(b[s]))) for s in seqs]
    ki = [_bf(k[s] * jnp.exp(-b[s])) for s in seqs]
    vb = [p[:, 512:896] for p in ps]

    lhs = [jnp.concatenate([jnp.where(hm, t, jnp.zeros_like(t)) for hm in hms], axis=0) for t in qd]
    sc = [_dot_nt(lhs[s], ki[s]) for s in seqs]
    pm = [jnp.concatenate([_bf(jnp.where(tri, t[h * tb:(h + 1) * tb], 0.0)) for h in range(GLA_HEADS)], axis=0)
          for t in sc]
    ra = [_dot(pm[s][:4 * tb], vb[s][:, 0:256]) for s in seqs]
    rb = [_dot(pm[s][4 * tb:], vb[s][:, 256:384]) for s in seqs]
    hl = lane // GLA_DV
    o_intra = []
    for s in seqs:
        oa = jnp.zeros((tb, 256), F32)
        for h in range(4):
            oa = oa + jnp.where(hl == h, ra[s][h * tb:(h + 1) * tb], 0.0)
        ob = jnp.where(hl[:, :LANE] == 0, rb[s][:tb], rb[s][tb:])
        o_intra.append(jnp.concatenate([oa, ob], axis=1))

    sts = list(sts)
    outs = [[None] * nch for _ in seqs]
    order = range(nch - 1, -1, -1) if reverse else range(nch)
    for c in order:
        r0 = c * GLA_CHUNK
        rows = slice(r0, r0 + GLA_CHUNK)
        edge = r0 if reverse else r0 + GLA_CHUNK - 1
        bl = [t[edge:edge + 1] for t in b]
        kd = [_bf(k[s][rows] * jnp.exp(bl[s] - b[s][rows])) for s in seqs]
        stbd = [_bf(jnp.concatenate([jnp.where(hm, t, 0.0) for hm in hms], axis=0)) for t in sts]
        for s in seqs:
            outs[s][c] = _dot_nt(qd[s][rows], stbd[s])
        full = [_dot_tn(vb[s][rows], kd[s]) for s in seqs]
        for s in seqs:
            ds = jnp.zeros((GLA_DV, 256), F32)
            for h in range(GLA_HEADS):
                ds = ds + jnp.where(hms[h], full[s][h * GLA_DV:(h + 1) * GLA_DV], 0.0)
            sts[s] = jnp.exp(bl[s]) * sts[s] + ds
    return [o_intra[s] + jnp.concatenate(outs[s], axis=0) for s in seqs], sts


SEQ_PER_STEP = 4
GLA_SEQ_PER_STEP = 4


def _gla_call(pg, o_f, wgk, bgk, nw, s0, reverse):
    bsz, L, _ = pg.shape
    nblk = L // GLA_BLOCK
    ns = GLA_SEQ_PER_STEP
    blk = (lambda b, i: (b, nblk - 1 - i, 0)) if reverse else (lambda b, i: (b, i, 0))
    const2 = lambda b, i: (0, 0)
    st_spec = pl.BlockSpec((ns, GLA_DV, 256), lambda b, i: (b, 0, 0))
    p_spec = pl.BlockSpec((ns, GLA_BLOCK, PG_W), blk)
    o_spec = pl.BlockSpec((ns, GLA_BLOCK, GLA_V), blk)
    if reverse:
        in_specs = [p_spec, o_spec, pl.BlockSpec((LANE, 256), const2), pl.BlockSpec((1, 256), const2),
                    pl.BlockSpec((1, GLA_V), const2), st_spec]
        args = (pg, o_f, wgk, bgk, nw, s0)
    else:
        in_specs = [p_spec, pl.BlockSpec((LANE, 256), const2), pl.BlockSpec((1, 256), const2), st_spec]
        args = (pg, wgk, bgk, s0)
    return pl.pallas_call(
        functools.partial(_gla_kernel, reverse=reverse, nblk=nblk),
        grid=(bsz // ns, nblk),
        in_specs=in_specs,
        out_specs=[o_spec, st_spec],
        out_shape=[jax.ShapeDtypeStruct((bsz, L, GLA_V), F32),
                   jax.ShapeDtypeStruct((bsz, GLA_DV, 256), F32)],
        scratch_shapes=[pltpu.VMEM((ns, GLA_DV, 256), F32)],
        compiler_params=_cp("arbitrary", "arbitrary"),
        name="gla_bwd" if reverse else "gla_fwd",
    )(*args)


def _expand_heads(t, lo):
    r = t.shape[0]
    lane = lax.broadcasted_iota(jnp.int32, (1, LANE), 1)
    tiles = []
    for j in range(SSD_HEADS // 2):
        a = jnp.broadcast_to(t[:, lo + 2 * j:lo + 2 * j + 1], (r, LANE))
        b = jnp.broadcast_to(t[:, lo + 2 * j + 1:lo + 2 * j + 2], (r, LANE))
        tiles.append(jnp.where(lane < SSD_P, a, b))
    return jnp.concatenate(tiles, axis=1)


def _ssd_prep(cur, prow, nrow, cw, cb, dtb, alog):
    q_ = SSD_CHUNK
    x = cur[:, 384:1280]
    ridx = lax.broadcasted_iota(jnp.int32, (q_, 1), 0)
    xm = jnp.where(ridx == 0, prow, pltpu.roll(x, 1, 0))
    xp = jnp.where(ridx == q_ - 1, nrow, pltpu.roll(x, q_ - 1, 0))
    act = _silu(cw[0:1] * xm + cw[1:2] * x + cw[2:3] * xp + cb)
    dtt = _softplus(cur[:, 1280:1408] + dtb)
    a = -jnp.exp(alog) * dtt
    return act, dtt, a


def _ssd_scans(xs, bmb, cmb, dtt, a, sts, tri, tri_bf, reverse):
    q_ = SSD_CHUNK
    lo = SSD_HEADS if reverse else 0
    seqs = range(len(xs))
    cs = [_dot_sel(tri_bf, t) for t in a]
    cst = [t.T for t in cs]
    dtT = [t.T for t in dtt]
    edge = 0 if reverse else q_ - 1
    cs_last = [t[edge:edge + 1] for t in cs]
    grp = lambda t, g: t[:, g * SSD_STATE:(g + 1) * SSD_STATE]
    cbs = [[_dot_nt(grp(cmb[s], g), grp(bmb[s], g)) for g in range(SSD_GROUPS)] for s in seqs]
    ms = [[] for _ in seqs]
    for h in range(SSD_HEADS):
        l = lo + h
        for s in seqs:
            seg = cs[s][:, l:l + 1] - cst[s][l:l + 1, :]
            dec = jnp.exp(jnp.where(tri, seg, -jnp.inf))
            ms[s].append(_bf(cbs[s][h // SSD_HG] * dec * dtT[s][l:l + 1, :]))
    mst = [jnp.concatenate(t, axis=0) for t in ms]
    xsb = [_bf(t) for t in xs]
    ra = [_dot(mst[s][:4 * q_], xsb[s][:, 0:256]) for s in seqs]
    rb = [_dot(mst[s][4 * q_:], xsb[s][:, 256:384]) for s in seqs]
    lane = lax.broadcasted_iota(jnp.int32, (1, 256), 1)
    hl = lane // SSD_P
    cs_x = [_expand_heads(t, lo) for t in cs]
    csl_x = [_expand_heads(t, lo) for t in cs_last]
    dt_x = [_expand_heads(t, lo) for t in dtt]
    ystate = [_dot(cmb[s], _bf(sts[s])) for s in seqs]
    xw = [_bf(xs[s] * (jnp.exp(csl_x[s] - cs_x[s]) * dt_x[s])) for s in seqs]
    full = [_dot_tn(bmb[s], xw[s]) for s in seqs]
    r2 = lax.broadcasted_iota(jnp.int32, (2 * SSD_STATE, SSD_INNER), 0) // SSD_STATE
    c2 = lax.broadcasted_iota(jnp.int32, (2 * SSD_STATE, SSD_INNER), 1) // (SSD_HG * SSD_P)
    ys, new_sts = [], []
    for s in seqs:
        ya = jnp.zeros((q_, 256), F32)
        for h in range(4):
            ya = ya + jnp.where(hl == h, ra[s][h * q_:(h + 1) * q_], 0.0)
        yb = jnp.where(hl[:, :LANE] == 0, rb[s][:q_], rb[s][q_:])
        ys.append(jnp.concatenate([ya, yb], axis=1) + jnp.exp(cs_x[s]) * ystate[s])
        new_sts.append(jnp.exp(csl_x[s]) * sts[s] + jnp.where(r2 == c2, full[s], 0.0))
    return ys, new_sts


def _ssd_kernel(*refs, reverse, nchunk):
    if reverse:
        (z_ref, act_ref, dta_ref, yf_ref, dx_ref, nw_ref, s0_ref, y_ref, sf_ref, st_scr) = refs
    else:
        (cur_ref, prev_ref, next_ref, cw_ref, cb_ref, dtb_ref, alog_ref, s0_ref,
         yf_ref, act_ref, dta_ref, sf_ref, st_scr) = refs
    i = pl.program_id(1)
    c = (nchunk - 1 - i) if reverse else i
    q_ = SSD_CHUNK

    @pl.when(i == 0)
    def _():
        st_scr[...] = s0_ref[...]

    ri = lax.broadcasted_iota(jnp.int32, (q_, q_), 0)
    ci = lax.broadcasted_iota(jnp.int32, (q_, q_), 1)
    tri = (ci >= ri) if reverse else (ci <= ri)
    tri_bf = jnp.where(tri, 1.0, 0.0).astype(BF16)

    seqs = range(st_scr.shape[0])
    if reverse:
        actb = [act_ref[s_, 0] for s_ in seqs]
        xs = [t[:, 0:384].astype(F32) for t in actb]
        dtt = [dta_ref[s_, 0][:, :LANE] for s_ in seqs]
        a = [dta_ref[s_, 0][:, LANE:] for s_ in seqs]
    else:
        has_prev = (c > 0).astype(F32)
        has_next = (c < nchunk - 1).astype(F32)
        actb, xs, dtt, a = [], [], [], []
        for s_ in seqs:
            prow = prev_ref[s_, 0][7:8, 384:1280] * has_prev
            nrow = next_ref[s_, 0][0:1, 384:1280] * has_next
            act, dt_, a_ = _ssd_prep(cur_ref[s_, 0], prow, nrow, cw_ref, cb_ref[...], dtb_ref[...], alog_ref[...])
            actb.append(_bf(act))
            act_ref[s_, 0] = actb[s_]
            dta_ref[s_, 0] = jnp.concatenate([dt_, a_], axis=1)
            xs.append(act[:, 0:384])
            dtt.append(dt_)
            a.append(a_)
    ys, sts = _ssd_scans(xs, [t[:, 384:640] for t in actb], [t[:, 640:896] for t in actb], dtt, a,
                         [st_scr[s_] for s_ in seqs], tri, tri_bf, reverse)
    for s_ in seqs:
        st_scr[s_] = sts[s_]

    @pl.when(i == nchunk - 1)
    def _():
        for s_ in seqs:
            sf_ref[s_] = sts[s_]

    for s_ in seqs:
        if not reverse:
            yf_ref[s_, 0] = ys[s_]
        else:
            yt = yf_ref[s_, 0] + ys[s_] + dx_ref[...] * xs[s_]
            yz = yt * _silu(z_ref[s_, 0])
            l384 = lax.broadcasted_iota(jnp.int32, (1, SSD_INNER), 1)
            g0 = l384 < (SSD_INNER // SSD_GROUPS)
            sq = yz * yz
            m0 = jnp.sum(jnp.where(g0, sq, 0.0), axis=-1, keepdims=True)
            m1 = jnp.sum(jnp.where(g0, 0.0, sq), axis=-1, keepdims=True)
            msq = jnp.where(g0, m0, m1) * (1.0 / (SSD_INNER // SSD_GROUPS))
            y_ref[s_, 0] = yz * lax.rsqrt(msq + EPS) * nw_ref[...]


def _ssd_fwd_call(ps, cw, cb, dtb, alog, s0):
    bsz, nchunk, _, _ = ps.shape
    ns = SEQ_PER_STEP
    cur_map = lambda b, i: (b, i, 0, 0)
    prev_map = lambda b, i: (b, jnp.maximum(i - 1, 0), SSD_CHUNK // 8 - 1, 0)
    next_map = lambda b, i: (b, jnp.minimum(i + 1, nchunk - 1), 0, 0)
    const2 = lambda b, i: (0, 0)
    chunk = lambda w: pl.BlockSpec((ns, 1, SSD_CHUNK, w), cur_map)
    st_spec = pl.BlockSpec((ns, 2 * SSD_STATE, SSD_INNER), lambda b, i: (b, 0, 0))
    return pl.pallas_call(
        functools.partial(_ssd_kernel, reverse=False, nchunk=nchunk),
        grid=(bsz // ns, nchunk),
        in_specs=[chunk(PS_W), pl.BlockSpec((ns, 1, 8, PS_W), prev_map), pl.BlockSpec((ns, 1, 8, PS_W), next_map),
                  pl.BlockSpec((8, SSD_CONV_DIM), const2), pl.BlockSpec((1, SSD_CONV_DIM), const2),
                  pl.BlockSpec((1, LANE), const2), pl.BlockSpec((1, LANE), const2), st_spec],
        out_specs=[chunk(SSD_INNER), chunk(SSD_CONV_DIM), chunk(2 * LANE), st_spec],
        out_shape=[jax.ShapeDtypeStruct((bsz, nchunk, SSD_CHUNK, SSD_INNER), F32),
                   jax.ShapeDtypeStruct((bsz, nchunk, SSD_CHUNK, SSD_CONV_DIM), BF16),
                   jax.ShapeDtypeStruct((bsz, nchunk, SSD_CHUNK, 2 * LANE), F32),
                   jax.ShapeDtypeStruct((bsz, 2 * SSD_STATE, SSD_INNER), F32)],
        scratch_shapes=[pltpu.VMEM((ns, 2 * SSD_STATE, SSD_INNER), F32)],
        compiler_params=_cp("arbitrary", "arbitrary"),
        name="ssd_fwd",
    )(ps, ps, ps, cw, cb, dtb, alog, s0)


def _ssd_bwd_call(ps, act, dta, y_f, dx, nw, s0):
    bsz, nchunk, _, _ = ps.shape
    ns = SEQ_PER_STEP
    cur_map = lambda b, i: (b, nchunk - 1 - i, 0, 0)
    const2 = lambda b, i: (0, 0)
    chunk = lambda w: pl.BlockSpec((ns, 1, SSD_CHUNK, w), cur_map)
    st_spec = pl.BlockSpec((ns, 2 * SSD_STATE, SSD_INNER), lambda b, i: (b, 0, 0))
    return pl.pallas_call(
        functools.partial(_ssd_kernel, reverse=True, nchunk=nchunk),
        grid=(bsz // ns, nchunk),
        in_specs=[chunk(SSD_INNER), chunk(SSD_CONV_DIM), chunk(2 * LANE), chunk(SSD_INNER),
                  pl.BlockSpec((1, SSD_INNER), const2), pl.BlockSpec((1, SSD_INNER), const2), st_spec],
        out_specs=[chunk(SSD_INNER), st_spec],
        out_shape=[jax.ShapeDtypeStruct((bsz, nchunk, SSD_CHUNK, SSD_INNER), F32),
                   jax.ShapeDtypeStruct((bsz, 2 * SSD_STATE, SSD_INNER), F32)],
        scratch_shapes=[pltpu.VMEM((ns, 2 * SSD_STATE, SSD_INNER), F32)],
        compiler_params=_cp("arbitrary", "arbitrary"),
        name="ssd_bwd",
    )(ps, act, dta, y_f, dx, nw, s0)


HALO_ROWS = 16


def _hy_pre_kernel(cur_ref, prev_ref, next_ref, w_ref, b_ref, vg_ref, x0_ref, *, nblk):
    i = pl.program_id(1)
    x = cur_ref[0].astype(F32)
    tb = x.shape[0]
    has_prev = (i > 0).astype(F32)
    has_next = (i < nblk - 1).astype(F32)
    prow = prev_ref[0].astype(F32)[HALO_ROWS - 1:HALO_ROWS] * has_prev
    nrow = next_ref[0].astype(F32)[0:1] * has_next
    ridx = lax.broadcasted_iota(jnp.int32, (tb, 1), 0)
    xm = jnp.where(ridx == 0, prow, pltpu.roll(x, 1, 0))
    xp = jnp.where(ridx == tb - 1, nrow, pltpu.roll(x, tb - 1, 0))
    u = w_ref[0:1] * xm + w_ref[1:2] * x + w_ref[2:3] * xp + b_ref[...]
    vg = u[:, 2 * HY_WIDTH:] * u[:, HY_WIDTH:2 * HY_WIDTH]
    for hf in range(HY_WIDTH // LANE):
        x0_ref[0, hf] = u[:, hf * LANE:(hf + 1) * LANE]
        vg_ref[0, hf] = vg[:, hf * LANE:(hf + 1) * LANE]


def _hy_pre_call(ph, w8, b):
    bsz, L, _ = ph.shape
    tb = min(512, L)
    nblk = L // tb
    r8 = tb // HALO_ROWS
    return pl.pallas_call(
        functools.partial(_hy_pre_kernel, nblk=nblk),
        grid=(bsz, nblk),
        in_specs=[pl.BlockSpec((1, tb, PH_W), lambda b_, i: (b_, i, 0)),
                  pl.BlockSpec((1, HALO_ROWS, PH_W), lambda b_, i: (b_, jnp.maximum(i * r8 - 1, 0), 0)),
                  pl.BlockSpec((1, HALO_ROWS, PH_W),
                               lambda b_, i: (b_, jnp.minimum((i + 1) * r8, L // HALO_ROWS - 1), 0)),
                  pl.BlockSpec((8, PH_W), lambda b_, i: (0, 0)),
                  pl.BlockSpec((1, PH_W), lambda b_, i: (0, 0))],
        out_specs=[pl.BlockSpec((1, 2, tb, LANE), lambda b_, i: (b_, 0, i, 0)),
                   pl.BlockSpec((1, 2, tb, LANE), lambda b_, i: (b_, 0, i, 0))],
        out_shape=[jax.ShapeDtypeStruct((bsz, 2, L, LANE), F32),
                   jax.ShapeDtypeStruct((bsz, 2, L, LANE), F32)],
        compiler_params=_cp("arbitrary", "arbitrary"),
        name="hy_pre",
    )(ph, ph, ph, w8, b)


def _hy_filter_kernel(z_ref, w1_ref, b1_ref, w2_ref, b2_ref, w3_ref, fr_ref, dec_ref, h_ref):
    z = z_ref[...]
    fr = fr_ref[...]
    h1 = jnp.sin(fr * (_dot_hp(z, w1_ref[...]) + b1_ref[...]))
    h2 = jnp.sin(fr * (_dot_hp(h1, w2_ref[...]) + b2_ref[...]))
    h = _dot_hp(h2, w3_ref[...])
    win = jnp.exp(-2.0 * jnp.abs(z[:, 0:1]) * dec_ref[...])
    h_ref[...] = h * win


def _hy_features(L):
    t = jnp.arange(L, dtype=F32)
    rel = (t - (L // 2)) / L
    bands = jnp.linspace(1e-4, HY_BANDS - 1, HY_BANDS, dtype=F32)
    ang = 2.0 * math.pi * rel[:, None] * bands
    z = jnp.concatenate([rel[:, None], jnp.cos(ang), -jnp.sin(ang)], axis=-1)
    return jnp.pad(z, ((0, 0), (0, LANE - HY_EMB)))


def _hy_filter_call(L, w1p, b1p, w2p, b2p, w3p, frp, dec):
    z = _hy_features(L)
    tl = min(1024, L)
    c2 = lambda i: (0, 0)
    return pl.pallas_call(
        _hy_filter_kernel,
        grid=(L // tl,),
        in_specs=[pl.BlockSpec((tl, LANE), lambda i: (i, 0)),
                  pl.BlockSpec((LANE, LANE), c2), pl.BlockSpec((1, LANE), c2),
                  pl.BlockSpec((LANE, LANE), c2), pl.BlockSpec((1, LANE), c2),
                  pl.BlockSpec((LANE, HY_WIDTH), c2), pl.BlockSpec((1, LANE), c2),
                  pl.BlockSpec((1, HY_WIDTH), c2)],
        out_specs=pl.BlockSpec((tl, HY_WIDTH), lambda i: (i, 0)),
        out_shape=jax.ShapeDtypeStruct((L, HY_WIDTH), F32),
        compiler_params=_cp("arbitrary"),
        name="hy_filter",
    )(z, w1p, b1p, w2p, b2p, w3p, frp, dec)


@functools.lru_cache(maxsize=None)
def _fft_consts():
    n1 = FFT_N1
    half = n1 // 2
    k = np.arange(n1, dtype=np.float64)
    n2 = k[:, None, None]
    k1 = k[None, :, None]
    nn = np.arange(half, dtype=np.float64)[None, None, :]
    ang = -2.0 * np.pi * (n2 * k1 / FFT_N + nn * k1 / n1)
    mr, mi = np.cos(ang), np.sin(ang)
    m1 = np.concatenate([np.concatenate([mr, -mi], axis=2), np.concatenate([mi, mr], axis=2)], axis=1)
    ang2 = -2.0 * np.pi * np.outer(k, k) / n1
    fr, fi = np.cos(ang2), np.sin(ang2)
    f2 = np.block([[fr, -fi], [fi, fr]])
    f2c = np.block([[fr, fi], [-fi, fr]])
    no = (np.arange(half, dtype=np.float64) + n1 // 4)[None, :, None]
    kk = k[None, None, :]
    ang3 = 2.0 * np.pi * (n2 * kk / FFT_N + no * kk / n1)
    ir, ii = np.cos(ang3) / FFT_N, np.sin(ang3) / FFT_N
    m3 = np.concatenate([np.concatenate([ir, -ii], axis=2), np.concatenate([ii, ir], axis=2)], axis=1)
    return tuple(np.asarray(m, dtype=np.float32) for m in (m1, f2, f2c, m3))


FFT_NB = 16
FFT_HALF = FFT_N1 // 2


def _strided_rows(ref2d, start, n):
    return ref2d[pl.ds(start, n, stride=FFT_NB), :]


def _tok_rows(ref, s, t):
    return jnp.concatenate([ref[s, hf, :, t, :] for hf in range(2)], axis=1)


def _stage_f32(dst, src_ref):
    v = src_ref[0].astype(F32).reshape(FFT_N1 * FFT_NB, HY_WIDTH)
    dst[0] = v[:, :LANE]
    dst[1] = v[:, LANE:]


def _staged_rows(scr, t):
    return jnp.concatenate([scr[hf, pl.ds(t, FFT_N1, stride=FFT_NB), :] for hf in range(2)], axis=1)


def _fft1_kernel(u_ref, m_ref, a_ref):
    for t in range(FFT_NB):
        rhs = _bf(jnp.concatenate([_tok_rows(u_ref, 0, t), _tok_rows(u_ref, 1, t)], axis=0))
        a_ref[0, t] = _bf(_dot(m_ref[t], rhs))


def _fft1_call(u, m1):
    npair = u.shape[0] // 2
    return pl.pallas_call(
        _fft1_kernel,
        grid=(npair, FFT_N1 // FFT_NB),
        in_specs=[pl.BlockSpec((2, 2, FFT_HALF, FFT_NB, LANE), lambda p, j: (p, 0, 0, j, 0)),
                  pl.BlockSpec((FFT_NB, 2 * FFT_N1, FFT_N1), lambda p, j: (j, 0, 0))],
        out_specs=pl.BlockSpec((1, FFT_NB, 2 * FFT_N1, HY_WIDTH), lambda p, j: (p, j, 0, 0)),
        out_shape=jax.ShapeDtypeStruct((npair, FFT_N1, 2 * FFT_N1, HY_WIDTH), BF16),
        compiler_params=_cp("arbitrary", "arbitrary"),
        name="hy_fft1",
    )(u, m1)


def _fft2_kernel(*refs, spectrum):
    if spectrum:
        ar_ref, ai_ref, f_ref, o_ref, sr, si = refs
    else:
        ar_ref, ai_ref, f_ref, fc_ref, h_ref, o_ref, sr, si = refs
    _stage_f32(sr, ar_ref)
    _stage_f32(si, ai_ref)
    for t in range(FFT_NB):
        rhs = _bf(jnp.concatenate([_staged_rows(sr, t), _staged_rows(si, t)], axis=0))
        x = _dot(f_ref[...], rhs)
        if spectrum:
            o_ref[t] = x
        else:
            xr, xi = x[:FFT_N1], x[FFT_N1:]
            hr, hi = h_ref[t, :FFT_N1], h_ref[t, FFT_N1:]
            y = jnp.concatenate([xr * hr - xi * hi, xr * hi + xi * hr], axis=0)
            o_ref[0, t] = _bf(_dot(fc_ref[...], _bf(y)))


def _fft2_call(a, f2, f2c, hspec):
    npair = a.shape[0]
    av = a
    nj = FFT_N1 // FFT_NB
    c2 = lambda p, j: (0, 0)
    in_specs = [pl.BlockSpec((1, FFT_N1, FFT_NB, HY_WIDTH), lambda p, j: (p, 0, j, 0)),
                pl.BlockSpec((1, FFT_N1, FFT_NB, HY_WIDTH), lambda p, j: (p, 0, nj + j, 0)),
                pl.BlockSpec((2 * FFT_N1, 2 * FFT_N1), c2)]
    staging = [pltpu.VMEM((2, FFT_N1 * FFT_NB, LANE), F32), pltpu.VMEM((2, FFT_N1 * FFT_NB, LANE), F32)]
    if hspec is None:
        return pl.pallas_call(
            functools.partial(_fft2_kernel, spectrum=True),
            grid=(1, nj),
            in_specs=in_specs,
            out_specs=pl.BlockSpec((FFT_NB, 2 * FFT_N1, HY_WIDTH), lambda p, j: (j, 0, 0)),
            out_shape=jax.ShapeDtypeStruct((FFT_N1, 2 * FFT_N1, HY_WIDTH), F32),
            scratch_shapes=staging,
            compiler_params=_cp("arbitrary", "arbitrary"),
            name="hy_fft2_spec",
        )(av, av, f2)
    in_specs += [pl.BlockSpec((2 * FFT_N1, 2 * FFT_N1), c2),
                 pl.BlockSpec((FFT_NB, 2 * FFT_N1, HY_WIDTH), lambda p, j: (j, 0, 0))]
    return pl.pallas_call(
        functools.partial(_fft2_kernel, spectrum=False),
        grid=(npair, nj),
        in_specs=in_specs,
        out_specs=pl.BlockSpec((1, FFT_NB, 2 * FFT_N1, HY_WIDTH), lambda p, j: (p, j, 0, 0)),
        out_shape=jax.ShapeDtypeStruct((npair, FFT_N1, 2 * FFT_N1, HY_WIDTH), BF16),
        scratch_shapes=staging,
        compiler_params=_cp("arbitrary", "arbitrary"),
        name="hy_fft2",
    )(av, av, f2, f2c, hspec)


def _fft3_kernel(br_ref, bi_ref, m_ref, vg_ref, x0_ref, bias_ref, y_ref, sr, si):
    _stage_f32(sr, br_ref)
    _stage_f32(si, bi_ref)
    for t in range(FFT_NB):
        rhs = _bf(jnp.concatenate([_staged_rows(sr, t), _staged_rows(si, t)], axis=0))
        out = _dot(m_ref[t], rhs)
        for s in range(2):
            conv = out[s * FFT_HALF:(s + 1) * FFT_HALF]
            y = (conv + _tok_rows(vg_ref, s, t) * bias_ref[...]) * _tok_rows(x0_ref, s, t)
            for hf in range(2):
                y_ref[s, hf, :, t, :] = y[:, hf * LANE:(hf + 1) * LANE]


def _fft3_call(bmat, m3, vg, x0, bias):
    npair = bmat.shape[0]
    nj = FFT_N1 // FFT_NB
    tok_spec = pl.BlockSpec((2, 2, FFT_HALF, FFT_NB, LANE), lambda p, j: (p, 0, 0, j, 0))
    return pl.pallas_call(
        _fft3_kernel,
        grid=(npair, nj),
        in_specs=[pl.BlockSpec((1, FFT_N1, FFT_NB, HY_WIDTH), lambda p, j: (p, 0, j, 0)),
                  pl.BlockSpec((1, FFT_N1, FFT_NB, HY_WIDTH), lambda p, j: (p, 0, nj + j, 0)),
                  pl.BlockSpec((FFT_NB, FFT_N1, 2 * FFT_N1), lambda p, j: (j, 0, 0)),
                  tok_spec, tok_spec,
                  pl.BlockSpec((1, HY_WIDTH), lambda p, j: (0, 0))],
        out_specs=tok_spec,
        out_shape=jax.ShapeDtypeStruct(vg.shape, F32),
        scratch_shapes=[pltpu.VMEM((2, FFT_N1 * FFT_NB, LANE), F32), pltpu.VMEM((2, FFT_N1 * FFT_NB, LANE), F32)],
        compiler_params=_cp("arbitrary", "arbitrary"),
        name="hy_fft3",
    )(bmat, bmat, m3, vg, x0, bias)


def _hy_direct_kernel(vg_ref, x0_ref, h_ref, bias_ref, y_ref, pad_scr, sh_scr):
    L = vg_ref.shape[2]
    u = jnp.concatenate([vg_ref[0, 0], vg_ref[0, 1]], axis=1)
    pad_scr[...] = jnp.zeros_like(pad_scr)
    pad_scr[L:2 * L, :] = u
    top = L + L // 2
    acc = jnp.zeros((L, HY_WIDTH), F32)
    for r in range(8):
        sh_scr[...] = pad_scr[r:r + 3 * L - 8, :]
        a_lo = -(-(top - L + 1 - r) // 8)
        a_hi = (top - r) // 8

        def body(a, acc, r=r):
            m = top - (a * 8 + r)
            return acc + h_ref[pl.ds(m, 1), :] * sh_scr[pl.ds(pl.multiple_of(a * 8, 8), L), :]

        acc = lax.fori_loop(a_lo, a_hi + 1, body, acc)
    y = (acc + u * bias_ref[...]) * jnp.concatenate([x0_ref[0, 0], x0_ref[0, 1]], axis=1)
    y_ref[0, 0] = y[:, :LANE]
    y_ref[0, 1] = y[:, LANE:]


def _hy_direct_call(vg, x0, h, bias):
    bsz, _, L, _ = vg.shape
    tok = pl.BlockSpec((1, 2, L, LANE), lambda b: (b, 0, 0, 0))
    return pl.pallas_call(
        _hy_direct_kernel,
        grid=(bsz,),
        in_specs=[tok, tok, pl.BlockSpec((L, HY_WIDTH), lambda b: (0, 0)),
                  pl.BlockSpec((1, HY_WIDTH), lambda b: (0, 0))],
        out_specs=tok,
        out_shape=jax.ShapeDtypeStruct(vg.shape, F32),
        scratch_shapes=[pltpu.VMEM((3 * L, HY_WIDTH), F32), pltpu.VMEM((3 * L - 8, HY_WIDTH), F32)],
        compiler_params=_cp("arbitrary"),
        name="hy_direct",
    )(vg, x0, h, bias)


def _out_kernel(x_ref, yg_ref, yh_ref, ys_ref, mod_ref, g2_ref, gf_ref, wo_ref, w1_ref, w3_ref, w2_ref,
                o_ref, *scr, final, col_major):
    m = mod_ref[0]
    if col_major:
        ys_scr, = scr
        for c in range(GRID_W):
            for k in range(SSD_INNER // LANE):
                ys_scr[k, pl.ds(c, ROWS_PER_TILE, stride=GRID_W), :] = ys_ref[0, c, :, k * LANE:(k + 1) * LANE]
        ys = jnp.concatenate([ys_scr[k] for k in range(SSD_INNER // LANE)], axis=1)
    else:
        ys = ys_ref[...]
    yh = jnp.concatenate([yh_ref[0, 0], yh_ref[0, 1]], axis=1)
    tm = x_ref.shape[0]
    nrg = ROW_GROUPS if tm % (8 * ROW_GROUPS) == 0 else 1
    rgs = [slice(r * (tm // nrg), (r + 1) * (tm // nrg)) for r in range(nrg)]
    mix = [(_dot(_bf(yg_ref[r, :]), wo_ref[0:GLA_V])
            + _dot(_bf(yh[r]), wo_ref[GLA_V:GLA_V + HY_WIDTH])
            + _dot(_bf(ys[r]), wo_ref[GLA_V + HY_WIDTH:])) for r in rgs]
    x1 = [x_ref[r, :] + m[2:3] * mx for r, mx in zip(rgs, mix)]
    ms = [jnp.mean(t * t, axis=-1, keepdims=True) for t in x1]
    h = [t * lax.rsqrt(s + EPS) * g2_ref[...] for t, s in zip(x1, ms)]
    h = [_bf(t * (1.0 + m[4:5]) + m[3:4]) for t in h]
    ffn = [None] * nrg
    for j in range(D_FF // FF_TILE):
        cols = slice(j * FF_TILE, (j + 1) * FF_TILE)
        a = [_dot(t, w1_ref[:, cols]) for t in h]
        b = [_dot(t, w3_ref[:, cols]) for t in h]
        part = [_dot(_bf(_silu(u) * v), w2_ref[cols, :]) for u, v in zip(a, b)]
        ffn = [p if f is None else f + p for f, p in zip(ffn, part)]
    for r, t, f in zip(rgs, x1, ffn):
        x2 = t + m[5:6] * f
        if final:
            ms2 = jnp.mean(x2 * x2, axis=-1, keepdims=True)
            x2 = x2 * lax.rsqrt(ms2 + EPS) * gf_ref[...]
        o_ref[r, :] = x2


def _out_call(x2, yg, yh, ys, mods, mod_row, g2, gf, wo, w1, w3, w2, final, col_major):
    t = x2.shape[0]
    seq = yh.shape[2]
    tm = min(TOK_TILE, seq)
    tiles = seq // tm
    c2 = lambda i: (0, 0)
    if col_major:
        assert tm == TOK_TILE
        ys_spec = pl.BlockSpec((1, GRID_W, ROWS_PER_TILE, SSD_INNER), lambda i: (i // tiles, 0, i % tiles, 0))
        extra = [pltpu.VMEM((SSD_INNER // LANE, tm, LANE), F32)]
    else:
        ys = ys.reshape(t, SSD_INNER)
        ys_spec = pl.BlockSpec((tm, SSD_INNER), lambda i: (i, 0))
        extra = []
    resident = lambda shape: pl.BlockSpec(shape, c2, pipeline_mode=pl.Buffered(1))
    return pl.pallas_call(
        functools.partial(_out_kernel, final=final, col_major=col_major),
        grid=(t // tm,),
        in_specs=[pl.BlockSpec((tm, D_MODEL), lambda i: (i, 0)),
                  pl.BlockSpec((tm, GLA_V), lambda i: (i, 0)),
                  pl.BlockSpec((1, 2, tm, LANE), lambda i: (i // tiles, 0, i % tiles, 0)),
                  ys_spec,
                  pl.BlockSpec((1, 6, D_MODEL), lambda i: (mod_row(i), 0, 0)),
                  pl.BlockSpec((1, D_MODEL), c2),
                  pl.BlockSpec((1, D_MODEL), c2),
                  resident((D_MODEL, D_MODEL)),
                  resident((D_MODEL, D_FF)),
                  resident((D_MODEL, D_FF)),
                  resident((D_FF, D_MODEL))],
        out_specs=pl.BlockSpec((tm, D_MODEL), lambda i: (i, 0)),
        out_shape=jax.ShapeDtypeStruct((t, D_MODEL), F32),
        scratch_shapes=extra,
        compiler_params=_cp("arbitrary"),
        name="out_ffn",
    )(x2, yg, yh, ys, mods, g2.reshape(1, D_MODEL), gf.reshape(1, D_MODEL), wo, w1, w3, w2)


def _pack_w_in(w_in):
    w_in = w_in.astype(BF16)
    z = lambda n: jnp.zeros(w_in.shape[:2] + (n,), w_in.dtype)
    q, k = w_in[..., 0:192], w_in[..., 192:384]
    v, g = w_in[..., 384:768], w_in[..., 768:1152]
    gk = w_in[..., 1152:1184]
    hy = w_in[..., 1184:1952]
    zz = w_in[..., 1952:2336]
    xbc = w_in[..., 2336:3232]
    dt = w_in[..., 3232:3244]
    packed = jnp.concatenate([q, z(64), k, z(64), v, g, gk, z(96), hy, zz, xbc, dt, z(116)], axis=-1)
    return packed.astype(BF16)


def _pad_to(a, shape):
    return jnp.pad(a, [(0, s - d) for d, s in zip(a.shape, shape)])


def _mixers(pg, ph, ps, lw, states):
    g_f0, g_b0, m_f0, m_b0 = states
    o_f, g_f = _gla_call(pg, None, lw['wgk_f'], lw['bgk_f'], None, g_f0, False)
    gla_y, g_b = _gla_call(pg, o_f, lw['wgk_b'], lw['bgk_b'], lw['gla_nw'], g_b0, True)

    vg, x0 = _hy_pre_call(ph, lw['hy_sw'], lw['hy_sb'])
    L = ph.shape[1]
    h = _hy_filter_call(L, lw['hy_w1'], lw['hy_b1'], lw['hy_w2'], lw['hy_b2'], lw['hy_w3'], lw['hy_freq'],
                        lw['hy_decay'])
    if L == FFT_N // 2:
        m1, f2, f2c, m3 = lw['fft']
        tok5 = lambda t: t.reshape(t.shape[0], 2, FFT_HALF, FFT_N1, LANE)
        hsplit = h.reshape(L, 2, LANE).transpose(1, 0, 2)
        hspec = _fft2_call(_fft1_call(tok5(jnp.stack([hsplit, jnp.zeros_like(hsplit)])), m1), f2, f2c, None)
        vg5 = tok5(vg)
        bmat = _fft2_call(_fft1_call(vg5, m1), f2, f2c, hspec)
        hy_y = _fft3_call(bmat, m3, vg5, tok5(x0), lw['hy_bias']).reshape(vg.shape)
    else:
        hy_y = _hy_direct_call(vg, x0, h, lw['hy_bias'])

    y_f, act, dta, m_f = _ssd_fwd_call(ps, lw['ssd_cw'], lw['ssd_cb'], lw['ssd_dtb'], lw['ssd_alog'], m_f0)
    ssd_y, m_b = _ssd_bwd_call(ps, act, dta, y_f, lw['ssd_dx'], lw['ssd_nw'], m_b0)
    return (gla_y, hy_y, ssd_y), (g_f, g_b, m_f, m_b)


def kernel(x, c, ctx, c_ctx, mod_w, mod_b, norm1_g, norm2_g, w_in, gla_gk_w_f, gla_gk_b_f, gla_gk_w_b, gla_gk_b_b, gla_norm_w, hy_short_w, hy_short_b, hy_w1, hy_b1, hy_w2, hy_b2, hy_w3, hy_freq, hy_decay, hy_bias, ssd_conv_w, ssd_conv_b, ssd_dt_bias_f, ssd_dt_bias_b, ssd_a_log_f, ssd_a_log_b, ssd_d, ssd_norm_w, w_out, ffn_w1, ffn_w3, ffn_w2, final_g):
    bsz, seq, _ = x.shape
    lc = ctx.shape[1]
    c8 = jnp.concatenate([c, c_ctx[None], jnp.zeros((8 - bsz - 1, D_MODEL), F32)], axis=0)
    mods_all = _mod_call(c8, mod_w, mod_b).reshape(DEPTH, 8, 6, D_MODEL)

    w_in_p = _pack_w_in(w_in)
    fft_mats = tuple(_bf(jnp.asarray(m)) for m in _fft_consts())
    w_out_b, w1_b, w3_b, w2_b = _bf(w_out), _bf(ffn_w1), _bf(ffn_w3), _bf(ffn_w2)

    xt = x.reshape(bsz * seq, D_MODEL)
    ct = ctx.reshape(bsz * lc, D_MODEL)
    tiles_per_seq = seq // TOK_TILE
    row_x = lambda i: i // tiles_per_seq
    row_c = lambda i: bsz

    zeros_states = (jnp.zeros((bsz, GLA_DV, 256), F32), jnp.zeros((bsz, GLA_DV, 256), F32),
                    jnp.zeros((bsz, 2 * SSD_STATE, SSD_INNER), F32),
                    jnp.zeros((bsz, 2 * SSD_STATE, SSD_INNER), F32))

    for l in range(DEPTH):
        def gkw(w, off):
            return _bf(_pad_to(jnp.pad(w, ((off, 0), (0, 0))), (LANE, 256)))
        lw = {
            'wgk_f': gkw(gla_gk_w_f[l], 0), 'wgk_b': gkw(gla_gk_w_b[l], GLA_LOWRANK),
            'bgk_f': _pad_to(gla_gk_b_f[l][None], (1, 256)), 'bgk_b': _pad_to(gla_gk_b_b[l][None], (1, 256)),
            'gla_nw': jnp.tile(gla_norm_w[l], GLA_HEADS)[None],
            'hy_sw': _pad_to(hy_short_w[l], (8, PH_W)), 'hy_sb': hy_short_b[l][None],
            'hy_w1': _pad_to(hy_w1[l], (LANE, LANE)), 'hy_b1': _pad_to(hy_b1[l][None], (1, LANE)),
            'hy_w2': _pad_to(hy_w2[l], (LANE, LANE)), 'hy_b2': _pad_to(hy_b2[l][None], (1, LANE)),
            'hy_w3': _pad_to(hy_w3[l], (LANE, HY_WIDTH)), 'hy_freq': _pad_to(hy_freq[l][None], (1, LANE)),
            'hy_decay': hy_decay[l][None], 'hy_bias': hy_bias[l][None],
            'ssd_cw': _pad_to(ssd_conv_w[l], (8, SSD_CONV_DIM)), 'ssd_cb': ssd_conv_b[l][None],
            'ssd_dtb': _pad_to(jnp.concatenate([ssd_dt_bias_f[l], ssd_dt_bias_b[l]])[None], (1, LANE)),
            'ssd_alog': _pad_to(jnp.concatenate([ssd_a_log_f[l], ssd_a_log_b[l]])[None], (1, LANE)),
            'ssd_dx': jnp.repeat(ssd_d[l], SSD_P)[None], 'ssd_nw': ssd_norm_w[l][None],
        }
        lw['fft'] = fft_mats
        mods = mods_all[l]
        yc, ctx_states = _mixers(*_in_call(ct, mods, row_c, norm1_g[l], w_in_p[l], bsz, False), lw, zeros_states)
        yx, _ = _mixers(*_in_call(xt, mods, row_x, norm1_g[l], w_in_p[l], bsz, True), lw, ctx_states)
        last = l == DEPTH - 1
        ffn = (w_out_b[l], w1_b[l], w3_b[l], w2_b[l])
        xt = _out_call(xt, yx[0].reshape(-1, GLA_V), yx[1], yx[2],
                       mods, row_x, norm2_g[l], final_g, *ffn, last, True)
        if not last:
            ct = _out_call(ct, yc[0].reshape(-1, GLA_V), yc[1], yc[2],
                           mods, row_c, norm2_g[l], final_g, *ffn, False, False)
    return xt.reshape(bsz, seq, D_MODEL)
```

```python
import functools
import math

import numpy as np
import jax
import jax.numpy as jnp
from jax import lax
from jax.experimental import pallas as pl
from jax.experimental.pallas import tpu as pltpu

F32 = jnp.float32
BF16 = jnp.bfloat16

D_MODEL = 1024
DEPTH = 2
GRID_W = 64
EPS = 1e-6
GLA_V = 384
GLA_DV = 64
GLA_HEADS = 6
GLA_DK = 32
GLA_QK = 192
GLA_LOWRANK = 16
GLA_TAU = 16.0
GLA_CHUNK = 64
GLA_BLOCK = 256
HY_WIDTH = 256
HY_BANDS = 16
HY_EMB = 1 + 2 * HY_BANDS
HY_ORDER = 64
SSD_INNER = 384
SSD_HEADS = 6
SSD_GROUPS = 2
SSD_HG = 3
SSD_P = 64
SSD_STATE = 128
SSD_CONV_DIM = SSD_INNER + 2 * SSD_GROUPS * SSD_STATE
SSD_CHUNK = 128
D_FF = 2816
FF_TILE = 1408
PG_W = 1408
PH_W = 768
PS_W = 1408
LANE = 128
FFT_N1 = 128
FFT_N = FFT_N1 * FFT_N1

VMEM_LIMIT = 56 * 1024 * 1024


def _cp(*sem):
    return pltpu.CompilerParams(dimension_semantics=sem, vmem_limit_bytes=VMEM_LIMIT)


def _bf(x):
    return x.astype(BF16)


def _dot(a, b):
    return jnp.dot(a, b, preferred_element_type=F32)


def _dot_nt(a, b):
    return lax.dot_general(a, b, (((1,), (1,)), ((), ())), preferred_element_type=F32)


def _dot_tn(a, b):
    return lax.dot_general(a, b, (((0,), (0,)), ((), ())), preferred_element_type=F32)


def _split3(x):
    hi = _bf(x)
    r1 = x - hi.astype(F32)
    mid = _bf(r1)
    lo = _bf(r1 - mid.astype(F32))
    return hi, mid, lo


def _dot_sel(sel_bf, x):
    hi, mid, lo = _split3(x)
    return _dot(sel_bf, hi) + _dot(sel_bf, mid) + _dot(sel_bf, lo)


def _dot_sel2(sel_bf, x):
    hi = _bf(x)
    lo = _bf(x - hi.astype(F32))
    return _dot(sel_bf, hi) + _dot(sel_bf, lo)


def _dot_hp(a, b):
    ah = _bf(a)
    al = _bf(a - ah.astype(F32))
    bh = _bf(b)
    bl = _bf(b - bh.astype(F32))
    return _dot(ah, bh) + _dot(ah, bl) + _dot(al, bh)


def _silu(x):
    return x * jax.nn.sigmoid(x)


def _softplus(x):
    return jnp.maximum(x, 0.0) + jnp.log1p(jnp.exp(-jnp.abs(x)))


def _log_sigmoid(x):
    return jnp.minimum(x, 0.0) - jnp.log1p(jnp.exp(-jnp.abs(x)))


def _mod_kernel(c_ref, w_ref, b_ref, o_ref):
    act = _silu(c_ref[...])
    o_ref[0] = _dot(_bf(act), _bf(w_ref[0])) + b_ref[0]


def _mod_call(c8, mod_w, mod_b):
    nt = 1536
    n = mod_w.shape[-1]
    return pl.pallas_call(
        _mod_kernel,
        grid=(DEPTH, n // nt),
        in_specs=[pl.BlockSpec((8, D_MODEL), lambda l, j: (0, 0)),
                  pl.BlockSpec((1, D_MODEL, nt), lambda l, j: (l, 0, j)),
                  pl.BlockSpec((1, 1, nt), lambda l, j: (l, 0, j))],
        out_specs=pl.BlockSpec((1, 8, nt), lambda l, j: (l, 0, j)),
        out_shape=jax.ShapeDtypeStruct((DEPTH, 8, n), F32),
        compiler_params=_cp("arbitrary", "arbitrary"),
        name="adaln_mod",
    )(c8, mod_w, mod_b.reshape(DEPTH, 1, n))


TOK_TILE = 512
ROW_GROUPS = 2
ROWS_PER_TILE = TOK_TILE // GRID_W


def _in_kernel(x_ref, xp_ref, xn_ref, mod_ref, g_ref, w_ref, hw_ref, hb_ref, og_ref, vg_ref, x0_ref, os_ref, *scr,
               col_major, tiles):
    m = mod_ref[0]
    i = pl.program_id(0)

    def modulated(t):
        ms = jnp.mean(t * t, axis=-1, keepdims=True)
        return _bf((t * lax.rsqrt(ms + EPS) * g_ref[...]) * (1.0 + m[1:2]) + m[0:1])

    tm = x_ref.shape[0]
    rgs = [slice(r * (tm // ROW_GROUPS), (r + 1) * (tm // ROW_GROUPS)) for r in range(ROW_GROUPS)]
    h = [modulated(x_ref[r, :]) for r in rgs]
    for r, t in zip(rgs, h):
        og_ref[r, :] = _bf(_dot(t, w_ref[:, 0:PG_W]))
    w_hy = w_ref[:, PG_W:PG_W + PH_W]
    ph = jnp.concatenate([_dot(t, w_hy) for t in h], axis=0)
    ps = jnp.concatenate([_dot(t, w_ref[:, PG_W + PH_W:]) for t in h], axis=0)

    halo = _dot(modulated(jnp.concatenate([xp_ref[...], xn_ref[...]], axis=0)), w_hy)
    prow = halo[7:8] * (i % tiles != 0).astype(F32)
    nrow = halo[8:9] * (i % tiles != tiles - 1).astype(F32)
    ridx = lax.broadcasted_iota(jnp.int32, (tm, 1), 0)
    pm = jnp.where(ridx == 0, prow, pltpu.roll(ph, 1, 0))
    pn = jnp.where(ridx == tm - 1, nrow, pltpu.roll(ph, tm - 1, 0))
    u = hw_ref[0:1] * pm + hw_ref[1:2] * ph + hw_ref[2:3] * pn + hb_ref[...]
    vg = u[:, 2 * HY_WIDTH:] * u[:, HY_WIDTH:2 * HY_WIDTH]
    for hf in range(HY_WIDTH // LANE):
        x0_ref[0, hf] = u[:, hf * LANE:(hf + 1) * LANE]
        vg_ref[0, hf] = vg[:, hf * LANE:(hf + 1) * LANE]

    if col_major:
        ps_scr, = scr
        for k in range(PS_W // LANE):
            for r in range(ROWS_PER_TILE):
                ps_scr[k, pl.ds(r, GRID_W, stride=ROWS_PER_TILE), :] = ps[r * GRID_W:(r + 1) * GRID_W,
                                                                         k * LANE:(k + 1) * LANE]
        for k in range(PS_W // LANE):
            os_ref[0, :, :, k * LANE:(k + 1) * LANE] = ps_scr[k].reshape(GRID_W, ROWS_PER_TILE, LANE)
    else:
        os_ref[...] = ps


def _in_call(x2, mods, mod_row, g, w_all, layer, hy_w8, hy_b, bsz, col_major):
    t = x2.shape[0]
    seq = t // bsz
    tm = min(TOK_TILE, seq)
    tiles = seq // tm
    wtot = PG_W + PH_W + PS_W
    hb = tm // 8
    if col_major:
        assert seq == GRID_W * SSD_CHUNK and ROWS_PER_TILE == 8 and tm == TOK_TILE
        os_spec = pl.BlockSpec((1, GRID_W, ROWS_PER_TILE, PS_W), lambda i: (i // tiles, 0, i % tiles, 0))
        os_shape = jax.ShapeDtypeStruct((bsz, GRID_W, SSD_CHUNK, PS_W), F32)
        scratch = [pltpu.VMEM((PS_W // LANE, tm, LANE), F32)]
    else:
        os_spec = pl.BlockSpec((tm, PS_W), lambda i: (i, 0))
        os_shape = jax.ShapeDtypeStruct((t, PS_W), F32)
        scratch = []
    nblk8 = t // 8
    tok_spec = pl.BlockSpec((1, 2, tm, LANE), lambda i: (i // tiles, 0, i % tiles, 0))
    tok_shape = jax.ShapeDtypeStruct((bsz, 2, seq, LANE), F32)
    pg, vg, x0, ps = pl.pallas_call(
        functools.partial(_in_kernel, col_major=col_major, tiles=tiles),
        grid=(t // tm,),
        in_specs=[pl.BlockSpec((tm, D_MODEL), lambda i: (i, 0)),
                  pl.BlockSpec((8, D_MODEL), lambda i: (jnp.maximum(i * hb - 1, 0), 0)),
                  pl.BlockSpec((8, D_MODEL), lambda i: (jnp.minimum((i + 1) * hb, nblk8 - 1), 0)),
                  pl.BlockSpec((1, 6, D_MODEL), lambda i: (mod_row(i), 0, 0)),
                  pl.BlockSpec((1, D_MODEL), lambda i: (0, 0)),
                  pl.BlockSpec((None, D_MODEL, wtot), lambda i: (layer, 0, 0), pipeline_mode=pl.Buffered(1)),
                  pl.BlockSpec((8, PH_W), lambda i: (0, 0)),
                  pl.BlockSpec((1, PH_W), lambda i: (0, 0))],
        out_specs=[pl.BlockSpec((tm, PG_W), lambda i: (i, 0)), tok_spec, tok_spec, os_spec],
        out_shape=[jax.ShapeDtypeStruct((t, PG_W), BF16), tok_shape, tok_shape, os_shape],
        scratch_shapes=scratch,
        compiler_params=_cp("arbitrary"),
        name="in_proj",
    )(x2, x2, x2, mods, g.reshape(1, D_MODEL), w_all, hy_w8, hy_b)
    return (pg.reshape(bsz, seq, PG_W), vg, x0, ps.reshape(bsz, seq // SSD_CHUNK, SSD_CHUNK, PS_W))


def _gla_kernel(*refs, reverse, nblk):
    if reverse:
        (p_ref, of_ref, wgk_ref, bgk_ref, nw_ref, s0_ref, y_ref, sf_ref, st_scr) = refs
    else:
        (p_ref, wgk_ref, bgk_ref, s0_ref, of_ref, sf_ref, st_scr) = refs
    i = pl.program_id(1)
    tb = GLA_BLOCK

    @pl.when(i == 0)
    def _():
        st_scr[...] = s0_ref[...]

    ri = lax.broadcasted_iota(jnp.int32, (tb, tb), 0)
    ci = lax.broadcasted_iota(jnp.int32, (tb, tb), 1)
    same = (ri // GLA_CHUNK) == (ci // GLA_CHUNK)
    tri = same & ((ci >= ri) if reverse else (ci <= ri))
    tri_bf = jnp.where(tri, 1.0, 0.0).astype(BF16)
    lane = lax.broadcasted_iota(jnp.int32, (1, 256), 1)
    hms = [(lane // GLA_DK) == h for h in range(GLA_HEADS)]

    nseq = p_ref.shape[0]
    seqs = range(nseq)
    ps = [p_ref[s_] for s_ in seqs]
    os_, sts = _gla_blocks(ps, [st_scr[s_] for s_ in seqs], wgk_ref[...], bgk_ref[...], tri, tri_bf, lane, hms,
                           reverse)
    for s_ in seqs:
        st_scr[s_] = sts[s_]

    @pl.when(i == nblk - 1)
    def _():
        for s_ in seqs:
            sf_ref[s_] = sts[s_]

    if not reverse:
        for s_ in seqs:
            of_ref[s_] = os_[s_]
    else:
        r2 = lax.broadcasted_iota(jnp.int32, (GLA_V, GLA_V), 0) // GLA_DV
        c2 = lax.broadcasted_iota(jnp.int32, (GLA_V, GLA_V), 1) // GLA_DV
        ind = jnp.where(r2 == c2, 1.0, 0.0).astype(BF16)
        ot = [of_ref[s_] + os_[s_] for s_ in seqs]
        sq = [t * t for t in ot]
        sh = [_bf(t) for t in sq]
        sl = [_bf(sq[s_] - sh[s_].astype(F32)) for s_ in seqs]
        ms = [(_dot(sh[s_], ind) + _dot(sl[s_], ind)) * (1.0 / GLA_DV) for s_ in seqs]
        for s_ in seqs:
            g = ps[s_][:, 896:1280].astype(F32)
            y_ref[s_] = ot[s_] * lax.rsqrt(ms[s_] + EPS) * nw_ref[...] * _silu(g)


def _gla_blocks(ps, sts, wgk, bgk, tri, tri_bf, lane, hms, reverse):
    tb = GLA_BLOCK
    nch = tb // GLA_CHUNK
    seqs = range(len(ps))
    k = [p[:, 256:512].astype(F32) for p in ps]
    pre = [_dot(p[:, 1280:1408], wgk) + bgk for p in ps]
    la = [_log_sigmoid(t) * (1.0 / GLA_TAU) for t in pre]
    b = [_dot_sel2(tri_bf, t) for t in la]
    qd = [_bf(ps[s][:, 0:256].astype(F32) * ((GLA_DK ** -0.5) * jnp.exp(b[s]))) for s in seqs]
    ki = [_bf(k[s] * jnp.exp(-b[s])) for s in seqs]
    vb = [p[:, 512:896] for p in ps]

    lhs = [jnp.concatenate([jnp.where(hm, t, jnp.zeros_like(t)) for hm in hms], axis=0) for t in qd]
    sc = [_dot_nt(lhs[s], ki[s]) for s in seqs]
    pm = [jnp.concatenate([_bf(jnp.where(tri, t[h * tb:(h + 1) * tb], 0.0)) for h in range(GLA_HEADS)], axis=0)
          for t in sc]
    ra = [_dot(pm[s][:4 * tb], vb[s][:, 0:256]) for s in seqs]
    rb = [_dot(pm[s][4 * tb:], vb[s][:, 256:384]) for s in seqs]
    hl = lane // GLA_DV
    o_intra = []
    for s in seqs:
        oa = jnp.zeros((tb, 256), F32)
        for h in range(4):
            oa = oa + jnp.where(hl == h, ra[s][h * tb:(h + 1) * tb], 0.0)
        ob = jnp.where(hl[:, :LANE] == 0, rb[s][:tb], rb[s][tb:])
        o_intra.append(jnp.concatenate([oa, ob], axis=1))

    sts = list(sts)
    outs = [[None] * nch for _ in seqs]
    order = range(nch - 1, -1, -1) if reverse else range(nch)
    for c in order:
        r0 = c * GLA_CHUNK
        rows = slice(r0, r0 + GLA_CHUNK)
        edge = r0 if reverse else r0 + GLA_CHUNK - 1
        bl = [t[edge:edge + 1] for t in b]
        kd = [_bf(k[s][rows] * jnp.exp(bl[s] - b[s][rows])) for s in seqs]
        stbd = [_bf(jnp.concatenate([jnp.where(hm, t, 0.0) for hm in hms], axis=0)) for t in sts]
        for s in seqs:
            outs[s][c] = _dot_nt(qd[s][rows], stbd[s])
        full = [_dot_tn(vb[s][rows], kd[s]) for s in seqs]
        for s in seqs:
            ds = jnp.zeros((GLA_DV, 256), F32)
            for h in range(GLA_HEADS):
                ds = ds + jnp.where(hms[h], full[s][h * GLA_DV:(h + 1) * GLA_DV], 0.0)
            sts[s] = jnp.exp(bl[s]) * sts[s] + ds
    return [o_intra[s] + jnp.concatenate(outs[s], axis=0) for s in seqs], sts


SEQ_PER_STEP = 4
GLA_SEQ_PER_STEP = 4


def _gla_call(pg, o_f, wgk, bgk, nw, s0, reverse):
    bsz, L, _ = pg.shape
    nblk = L // GLA_BLOCK
    ns = GLA_SEQ_PER_STEP
    blk = (lambda b, i: (b, nblk - 1 - i, 0)) if reverse else (lambda b, i: (b, i, 0))
    const2 = lambda b, i: (0, 0)
    st_spec = pl.BlockSpec((ns, GLA_DV, 256), lambda b, i: (b, 0, 0))
    p_spec = pl.BlockSpec((ns, GLA_BLOCK, PG_W), blk)
    o_spec = pl.BlockSpec((ns, GLA_BLOCK, GLA_V), blk)
    if reverse:
        in_specs = [p_spec, o_spec, pl.BlockSpec((LANE, 256), const2), pl.BlockSpec((1, 256), const2),
                    pl.BlockSpec((1, GLA_V), const2), st_spec]
        args = (pg, o_f, wgk, bgk, nw, s0)
    else:
        in_specs = [p_spec, pl.BlockSpec((LANE, 256), const2), pl.BlockSpec((1, 256), const2), st_spec]
        args = (pg, wgk, bgk, s0)
    return pl.pallas_call(
        functools.partial(_gla_kernel, reverse=reverse, nblk=nblk),
        grid=(bsz // ns, nblk),
        in_specs=in_specs,
        out_specs=[o_spec, st_spec],
        out_shape=[jax.ShapeDtypeStruct((bsz, L, GLA_V), F32),
                   jax.ShapeDtypeStruct((bsz, GLA_DV, 256), F32)],
        scratch_shapes=[pltpu.VMEM((ns, GLA_DV, 256), F32)],
        compiler_params=_cp("arbitrary", "arbitrary"),
        name="gla_bwd" if reverse else "gla_fwd",
    )(*args)


def _expand_heads(t, lo):
    r = t.shape[0]
    lane = lax.broadcasted_iota(jnp.int32, (1, LANE), 1)
    tiles = []
    for j in range(SSD_HEADS // 2):
        a = jnp.broadcast_to(t[:, lo + 2 * j:lo + 2 * j + 1], (r, LANE))
        b = jnp.broadcast_to(t[:, lo + 2 * j + 1:lo + 2 * j + 2], (r, LANE))
        tiles.append(jnp.where(lane < SSD_P, a, b))
    return jnp.concatenate(tiles, axis=1)


def _ssd_prep(cur, prow, nrow, cw, cb, dtb, alog):
    q_ = SSD_CHUNK
    x = cur[:, 384:1280]
    ridx = lax.broadcasted_iota(jnp.int32, (q_, 1), 0)
    xm = jnp.where(ridx == 0, prow, pltpu.roll(x, 1, 0))
    xp = jnp.where(ridx == q_ - 1, nrow, pltpu.roll(x, q_ - 1, 0))
    act = _silu(cw[0:1] * xm + cw[1:2] * x + cw[2:3] * xp + cb)
    dtt = _softplus(cur[:, 1280:1408] + dtb)
    a = -jnp.exp(alog) * dtt
    return act, dtt, a


def _ssd_scans(xs, bmb, cmb, dtt, a, sts, tri, tri_bf, reverse):
    q_ = SSD_CHUNK
    lo = SSD_HEADS if reverse else 0
    seqs = range(len(xs))
    cs = [_dot_sel(tri_bf, t) for t in a]
    cst = [t.T for t in cs]
    dtT = [t.T for t in dtt]
    edge = 0 if reverse else q_ - 1
    cs_last = [t[edge:edge + 1] for t in cs]
    grp = lambda t, g: t[:, g * SSD_STATE:(g + 1) * SSD_STATE]
    cbs = [[_dot_nt(grp(cmb[s], g), grp(bmb[s], g)) for g in range(SSD_GROUPS)] for s in seqs]
    ms = [[] for _ in seqs]
    for h in range(SSD_HEADS):
        l = lo + h
        for s in seqs:
            seg = cs[s][:, l:l + 1] - cst[s][l:l + 1, :]
            dec = jnp.exp(jnp.where(tri, seg, -jnp.inf))
            ms[s].append(_bf(cbs[s][h // SSD_HG] * dec * dtT[s][l:l + 1, :]))
    mst = [jnp.concatenate(t, axis=0) for t in ms]
    xsb = [_bf(t) for t in xs]
    ra = [_dot(mst[s][:4 * q_], xsb[s][:, 0:256]) for s in seqs]
    rb = [_dot(mst[s][4 * q_:], xsb[s][:, 256:384]) for s in seqs]
    lane = lax.broadcasted_iota(jnp.int32, (1, 256), 1)
    hl = lane // SSD_P
    cs_x = [_expand_heads(t, lo) for t in cs]
    csl_x = [_expand_heads(t, lo) for t in cs_last]
    dt_x = [_expand_heads(t, lo) for t in dtt]
    ystate = [_dot(cmb[s], _bf(sts[s])) for s in seqs]
    xw = [_bf(xs[s] * (jnp.exp(csl_x[s] - cs_x[s]) * dt_x[s])) for s in seqs]
    full = [_dot_tn(bmb[s], xw[s]) for s in seqs]
    r2 = lax.broadcasted_iota(jnp.int32, (2 * SSD_STATE, SSD_INNER), 0) // SSD_STATE
    c2 = lax.broadcasted_iota(jnp.int32, (2 * SSD_STATE, SSD_INNER), 1) // (SSD_HG * SSD_P)
    ys, new_sts = [], []
    for s in seqs:
        ya = jnp.zeros((q_, 256), F32)
        for h in range(4):
            ya = ya + jnp.where(hl == h, ra[s][h * q_:(h + 1) * q_], 0.0)
        yb = jnp.where(hl[:, :LANE] == 0, rb[s][:q_], rb[s][q_:])
        ys.append(jnp.concatenate([ya, yb], axis=1) + jnp.exp(cs_x[s]) * ystate[s])
        new_sts.append(jnp.exp(csl_x[s]) * sts[s] + jnp.where(r2 == c2, full[s], 0.0))
    return ys, new_sts


def _ssd_kernel(*refs, reverse, nchunk):
    if reverse:
        (z_ref, act_ref, dta_ref, yf_ref, dx_ref, nw_ref, s0_ref, y_ref, sf_ref, st_scr) = refs
    else:
        (cur_ref, prev_ref, next_ref, cw_ref, cb_ref, dtb_ref, alog_ref, s0_ref,
         yf_ref, act_ref, dta_ref, sf_ref, st_scr) = refs
    i = pl.program_id(1)
    c = (nchunk - 1 - i) if reverse else i
    q_ = SSD_CHUNK

    @pl.when(i == 0)
    def _():
        st_scr[...] = s0_ref[...]

    ri = lax.broadcasted_iota(jnp.int32, (q_, q_), 0)
    ci = lax.broadcasted_iota(jnp.int32, (q_, q_), 1)
    tri = (ci >= ri) if reverse else (ci <= ri)
    tri_bf = jnp.where(tri, 1.0, 0.0).astype(BF16)

    seqs = range(st_scr.shape[0])
    if reverse:
        actb = [act_ref[s_, 0] for s_ in seqs]
        xs = [t[:, 0:384].astype(F32) for t in actb]
        dtt = [dta_ref[s_, 0][:, :LANE] for s_ in seqs]
        a = [dta_ref[s_, 0][:, LANE:] for s_ in seqs]
    else:
        has_prev = (c > 0).astype(F32)
        has_next = (c < nchunk - 1).astype(F32)
        actb, xs, dtt, a = [], [], [], []
        for s_ in seqs:
            prow = prev_ref[s_, 0][7:8, 384:1280] * has_prev
            nrow = next_ref[s_, 0][0:1, 384:1280] * has_next
            act, dt_, a_ = _ssd_prep(cur_ref[s_, 0], prow, nrow, cw_ref, cb_ref[...], dtb_ref[...], alog_ref[...])
            actb.append(_bf(act))
            act_ref[s_, 0] = actb[s_]
            dta_ref[s_, 0] = jnp.concatenate([dt_, a_], axis=1)
            xs.append(act[:, 0:384])
            dtt.append(dt_)
            a.append(a_)
    ys, sts = _ssd_scans(xs, [t[:, 384:640] for t in actb], [t[:, 640:896] for t in actb], dtt, a,
                         [st_scr[s_] for s_ in seqs], tri, tri_bf, reverse)
    for s_ in seqs:
        st_scr[s_] = sts[s_]

    @pl.when(i == nchunk - 1)
    def _():
        for s_ in seqs:
            sf_ref[s_] = sts[s_]

    for s_ in seqs:
        if not reverse:
            yf_ref[s_, 0] = ys[s_]
        else:
            yt = yf_ref[s_, 0] + ys[s_] + dx_ref[...] * xs[s_]
            yz = yt * _silu(z_ref[s_, 0])
            l384 = lax.broadcasted_iota(jnp.int32, (1, SSD_INNER), 1)
            g0 = l384 < (SSD_INNER // SSD_GROUPS)
            sq = yz * yz
            m0 = jnp.sum(jnp.where(g0, sq, 0.0), axis=-1, keepdims=True)
            m1 = jnp.sum(jnp.where(g0, 0.0, sq), axis=-1, keepdims=True)
            msq = jnp.where(g0, m0, m1) * (1.0 / (SSD_INNER // SSD_GROUPS))
            y_ref[s_, 0] = yz * lax.rsqrt(msq + EPS) * nw_ref[...]


def _ssd_fwd_call(ps, cw, cb, dtb, alog, s0):
    bsz, nchunk, _, _ = ps.shape
    ns = SEQ_PER_STEP
    cur_map = lambda b, i: (b, i, 0, 0)
    prev_map = lambda b, i: (b, jnp.maximum(i - 1, 0), SSD_CHUNK // 8 - 1, 0)
    next_map = lambda b, i: (b, jnp.minimum(i + 1, nchunk - 1), 0, 0)
    const2 = lambda b, i: (0, 0)
    chunk = lambda w: pl.BlockSpec((ns, 1, SSD_CHUNK, w), cur_map)
    st_spec = pl.BlockSpec((ns, 2 * SSD_STATE, SSD_INNER), lambda b, i: (b, 0, 0))
    return pl.pallas_call(
        functools.partial(_ssd_kernel, reverse=False, nchunk=nchunk),
        grid=(bsz // ns, nchunk),
        in_specs=[chunk(PS_W), pl.BlockSpec((ns, 1, 8, PS_W), prev_map), pl.BlockSpec((ns, 1, 8, PS_W), next_map),
                  pl.BlockSpec((8, SSD_CONV_DIM), const2), pl.BlockSpec((1, SSD_CONV_DIM), const2),
                  pl.BlockSpec((1, LANE), const2), pl.BlockSpec((1, LANE), const2), st_spec],
        out_specs=[chunk(SSD_INNER), chunk(SSD_CONV_DIM), chunk(2 * LANE), st_spec],
        out_shape=[jax.ShapeDtypeStruct((bsz, nchunk, SSD_CHUNK, SSD_INNER), F32),
                   jax.ShapeDtypeStruct((bsz, nchunk, SSD_CHUNK, SSD_CONV_DIM), BF16),
                   jax.ShapeDtypeStruct((bsz, nchunk, SSD_CHUNK, 2 * LANE), F32),
                   jax.ShapeDtypeStruct((bsz, 2 * SSD_STATE, SSD_INNER), F32)],
        scratch_shapes=[pltpu.VMEM((ns, 2 * SSD_STATE, SSD_INNER), F32)],
        compiler_params=_cp("arbitrary", "arbitrary"),
        name="ssd_fwd",
    )(ps, ps, ps, cw, cb, dtb, alog, s0)


def _ssd_bwd_call(ps, act, dta, y_f, dx, nw, s0):
    bsz, nchunk, _, _ = ps.shape
    ns = SEQ_PER_STEP
    cur_map = lambda b, i: (b, nchunk - 1 - i, 0, 0)
    const2 = lambda b, i: (0, 0)
    chunk = lambda w: pl.BlockSpec((ns, 1, SSD_CHUNK, w), cur_map)
    st_spec = pl.BlockSpec((ns, 2 * SSD_STATE, SSD_INNER), lambda b, i: (b, 0, 0))
    return pl.pallas_call(
        functools.partial(_ssd_kernel, reverse=True, nchunk=nchunk),
        grid=(bsz // ns, nchunk),
        in_specs=[chunk(SSD_INNER), chunk(SSD_CONV_DIM), chunk(2 * LANE), chunk(SSD_INNER),
                  pl.BlockSpec((1, SSD_INNER), const2), pl.BlockSpec((1, SSD_INNER), const2), st_spec],
        out_specs=[chunk(SSD_INNER), st_spec],
        out_shape=[jax.ShapeDtypeStruct((bsz, nchunk, SSD_CHUNK, SSD_INNER), F32),
                   jax.ShapeDtypeStruct((bsz, 2 * SSD_STATE, SSD_INNER), F32)],
        scratch_shapes=[pltpu.VMEM((ns, 2 * SSD_STATE, SSD_INNER), F32)],
        compiler_params=_cp("arbitrary", "arbitrary"),
        name="ssd_bwd",
    )(ps, act, dta, y_f, dx, nw, s0)


def _hy_filter_kernel(z_ref, w1_ref, b1_ref, w2_ref, b2_ref, w3_ref, fr_ref, dec_ref, h_ref):
    z = z_ref[...]
    fr = fr_ref[...]
    h1 = jnp.sin(fr * (_dot_hp(z, w1_ref[...]) + b1_ref[...]))
    h2 = jnp.sin(fr * (_dot_hp(h1, w2_ref[...]) + b2_ref[...]))
    h = _dot_hp(h2, w3_ref[...])
    win = jnp.exp(-2.0 * jnp.abs(z[:, 0:1]) * dec_ref[...])
    h = h * win
    for hf in range(HY_WIDTH // LANE):
        h_ref[hf] = h[:, hf * LANE:(hf + 1) * LANE]


def _hy_features(L):
    t = jnp.arange(L, dtype=F32)
    rel = (t - (L // 2)) / L
    bands = jnp.linspace(1e-4, HY_BANDS - 1, HY_BANDS, dtype=F32)
    ang = 2.0 * math.pi * rel[:, None] * bands
    z = jnp.concatenate([rel[:, None], jnp.cos(ang), -jnp.sin(ang)], axis=-1)
    return jnp.pad(z, ((0, 0), (0, LANE - HY_EMB)))


def _hy_filter_call(L, w1p, b1p, w2p, b2p, w3p, frp, dec):
    z = _hy_features(L)
    tl = min(1024, L)
    c2 = lambda i: (0, 0)
    return pl.pallas_call(
        _hy_filter_kernel,
        grid=(L // tl,),
        in_specs=[pl.BlockSpec((tl, LANE), lambda i: (i, 0)),
                  pl.BlockSpec((LANE, LANE), c2), pl.BlockSpec((1, LANE), c2),
                  pl.BlockSpec((LANE, LANE), c2), pl.BlockSpec((1, LANE), c2),
                  pl.BlockSpec((LANE, HY_WIDTH), c2), pl.BlockSpec((1, LANE), c2),
                  pl.BlockSpec((1, HY_WIDTH), c2)],
        out_specs=pl.BlockSpec((2, tl, LANE), lambda i: (0, i, 0)),
        out_shape=jax.ShapeDtypeStruct((2, L, LANE), F32),
        compiler_params=_cp("arbitrary"),
        name="hy_filter",
    )(z, w1p, b1p, w2p, b2p, w3p, frp, dec)


@functools.lru_cache(maxsize=None)
def _fft_consts():
    n1 = FFT_N1
    half = n1 // 2
    k = np.arange(n1, dtype=np.float64)
    n2 = k[:, None, None]
    k1 = k[None, :, None]
    nn = np.arange(half, dtype=np.float64)[None, None, :]
    ang = -2.0 * np.pi * (n2 * k1 / FFT_N + nn * k1 / n1)
    mr, mi = np.cos(ang), np.sin(ang)
    m1 = np.concatenate([np.concatenate([mr, -mi], axis=2), np.concatenate([mi, mr], axis=2)], axis=1)
    ang2 = -2.0 * np.pi * np.outer(k, k) / n1
    fr, fi = np.cos(ang2), np.sin(ang2)
    f2 = np.block([[fr, -fi], [fi, fr]])
    f2c = np.block([[fr, fi], [-fi, fr]])
    no = (np.arange(half, dtype=np.float64) + n1 // 4)[None, :, None]
    kk = k[None, None, :]
    ang3 = 2.0 * np.pi * (n2 * kk / FFT_N + no * kk / n1)
    ir, ii = np.cos(ang3) / FFT_N, np.sin(ang3) / FFT_N
    m3 = np.concatenate([np.concatenate([ir, -ii], axis=2), np.concatenate([ii, ir], axis=2)], axis=1)
    return tuple(np.asarray(m, dtype=np.float32) for m in (m1, f2, f2c, m3))


FFT_NB = 16
FFT_HALF = FFT_N1 // 2


def _strided_rows(ref2d, start, n):
    return ref2d[pl.ds(start, n, stride=FFT_NB), :]


def _tok_rows(ref, s, t):
    return jnp.concatenate([ref[s, hf, :, t, :] for hf in range(2)], axis=1)


def _stage_f32(dst, src_ref):
    v = src_ref[0].astype(F32).reshape(FFT_N1 * FFT_NB, HY_WIDTH)
    dst[0] = v[:, :LANE]
    dst[1] = v[:, LANE:]


def _staged_rows(scr, t):
    return jnp.concatenate([scr[hf, pl.ds(t, FFT_N1, stride=FFT_NB), :] for hf in range(2)], axis=1)


def _fft1_kernel(u_ref, m_ref, a_ref, *, nsig):
    for t in range(FFT_NB):
        rhs = _bf(jnp.concatenate([_tok_rows(u_ref, s, t) for s in range(nsig)], axis=0))
        a_ref[0, t] = _bf(_dot(m_ref[t], rhs))


def _fft1_call(u, m1, nsig):
    npair = u.shape[0] // nsig
    return pl.pallas_call(
        functools.partial(_fft1_kernel, nsig=nsig),
        grid=(npair, FFT_N1 // FFT_NB),
        in_specs=[pl.BlockSpec((nsig, 2, FFT_HALF, FFT_NB, LANE), lambda p, j: (p, 0, 0, j, 0)),
                  pl.BlockSpec((FFT_NB, 2 * FFT_N1, FFT_HALF * nsig), lambda p, j: (j, 0, 0))],
        out_specs=pl.BlockSpec((1, FFT_NB, 2 * FFT_N1, HY_WIDTH), lambda p, j: (p, j, 0, 0)),
        out_shape=jax.ShapeDtypeStruct((npair, FFT_N1, 2 * FFT_N1, HY_WIDTH), BF16),
        compiler_params=_cp("arbitrary", "arbitrary"),
        name="hy_fft1",
    )(u, m1)


def _fft2_kernel(*refs, spectrum):
    if spectrum:
        ar_ref, ai_ref, f_ref, o_ref, sr, si = refs
    else:
        ar_ref, ai_ref, f_ref, fc_ref, h_ref, o_ref, sr, si = refs
    _stage_f32(sr, ar_ref)
    _stage_f32(si, ai_ref)
    for t in range(FFT_NB):
        rhs = _bf(jnp.concatenate([_staged_rows(sr, t), _staged_rows(si, t)], axis=0))
        x = _dot(f_ref[...], rhs)
        if spectrum:
            o_ref[t] = x
        else:
            xr, xi = x[:FFT_N1], x[FFT_N1:]
            hr, hi = h_ref[t, :FFT_N1], h_ref[t, FFT_N1:]
            y = jnp.concatenate([xr * hr - xi * hi, xr * hi + xi * hr], axis=0)
            o_ref[0, t] = _bf(_dot(fc_ref[...], _bf(y)))


def _fft2_call(a, f2, f2c, hspec):
    npair = a.shape[0]
    av = a
    nj = FFT_N1 // FFT_NB
    c2 = lambda p, j: (0, 0)
    in_specs = [pl.BlockSpec((1, FFT_N1, FFT_NB, HY_WIDTH), lambda p, j: (p, 0, j, 0)),
                pl.BlockSpec((1, FFT_N1, FFT_NB, HY_WIDTH), lambda p, j: (p, 0, nj + j, 0)),
                pl.BlockSpec((2 * FFT_N1, 2 * FFT_N1), c2)]
    staging = [pltpu.VMEM((2, FFT_N1 * FFT_NB, LANE), F32), pltpu.VMEM((2, FFT_N1 * FFT_NB, LANE), F32)]
    if hspec is None:
        return pl.pallas_call(
            functools.partial(_fft2_kernel, spectrum=True),
            grid=(1, nj),
            in_specs=in_specs,
            out_specs=pl.BlockSpec((FFT_NB, 2 * FFT_N1, HY_WIDTH), lambda p, j: (j, 0, 0)),
            out_shape=jax.ShapeDtypeStruct((FFT_N1, 2 * FFT_N1, HY_WIDTH), F32),
            scratch_shapes=staging,
            compiler_params=_cp("arbitrary", "arbitrary"),
            name="hy_fft2_spec",
        )(av, av, f2)
    in_specs += [pl.BlockSpec((2 * FFT_N1, 2 * FFT_N1), c2),
                 pl.BlockSpec((FFT_NB, 2 * FFT_N1, HY_WIDTH), lambda p, j: (j, 0, 0))]
    return pl.pallas_call(
        functools.partial(_fft2_kernel, spectrum=False),
        grid=(npair, nj),
        in_specs=in_specs,
        out_specs=pl.BlockSpec((1, FFT_NB, 2 * FFT_N1, HY_WIDTH), lambda p, j: (p, j, 0, 0)),
        out_shape=jax.ShapeDtypeStruct((npair, FFT_N1, 2 * FFT_N1, HY_WIDTH), BF16),
        scratch_shapes=staging,
        compiler_params=_cp("arbitrary", "arbitrary"),
        name="hy_fft2",
    )(av, av, f2, f2c, hspec)


def _fft3_kernel(br_ref, bi_ref, m_ref, vg_ref, x0_ref, bias_ref, y_ref, sr, si):
    _stage_f32(sr, br_ref)
    _stage_f32(si, bi_ref)
    for t in range(FFT_NB):
        rhs = _bf(jnp.concatenate([_staged_rows(sr, t), _staged_rows(si, t)], axis=0))
        out = _dot(m_ref[t], rhs)
        for s in range(2):
            conv = out[s * FFT_HALF:(s + 1) * FFT_HALF]
            y = (conv + _tok_rows(vg_ref, s, t) * bias_ref[...]) * _tok_rows(x0_ref, s, t)
            for hf in range(2):
                y_ref[s, hf, :, t, :] = y[:, hf * LANE:(hf + 1) * LANE]


def _fft3_call(bmat, m3, vg, x0, bias):
    npair = bmat.shape[0]
    nj = FFT_N1 // FFT_NB
    tok_spec = pl.BlockSpec((2, 2, FFT_HALF, FFT_NB, LANE), lambda p, j: (p, 0, 0, j, 0))
    return pl.pallas_call(
        _fft3_kernel,
        grid=(npair, nj),
        in_specs=[pl.BlockSpec((1, FFT_N1, FFT_NB, HY_WIDTH), lambda p, j: (p, 0, j, 0)),
                  pl.BlockSpec((1, FFT_N1, FFT_NB, HY_WIDTH), lambda p, j: (p, 0, nj + j, 0)),
                  pl.BlockSpec((FFT_NB, FFT_N1, 2 * FFT_N1), lambda p, j: (j, 0, 0)),
                  tok_spec, tok_spec,
                  pl.BlockSpec((1, HY_WIDTH), lambda p, j: (0, 0))],
        out_specs=tok_spec,
        out_shape=jax.ShapeDtypeStruct(vg.shape, F32),
        scratch_shapes=[pltpu.VMEM((2, FFT_N1 * FFT_NB, LANE), F32), pltpu.VMEM((2, FFT_N1 * FFT_NB, LANE), F32)],
        compiler_params=_cp("arbitrary", "arbitrary"),
        name="hy_fft3",
    )(bmat, bmat, m3, vg, x0, bias)


def _hy_direct_kernel(vg_ref, x0_ref, h_ref, bias_ref, y_ref, pad_scr, sh_scr):
    L = vg_ref.shape[2]
    u = jnp.concatenate([vg_ref[0, 0], vg_ref[0, 1]], axis=1)
    pad_scr[...] = jnp.zeros_like(pad_scr)
    pad_scr[L:2 * L, :] = u
    top = L + L // 2
    acc = jnp.zeros((L, HY_WIDTH), F32)
    for r in range(8):
        sh_scr[...] = pad_scr[r:r + 3 * L - 8, :]
        a_lo = -(-(top - L + 1 - r) // 8)
        a_hi = (top - r) // 8

        def body(a, acc, r=r):
            m = top - (a * 8 + r)
            tap = jnp.concatenate([h_ref[0, pl.ds(m, 1), :], h_ref[1, pl.ds(m, 1), :]], axis=1)
            return acc + tap * sh_scr[pl.ds(pl.multiple_of(a * 8, 8), L), :]

        acc = lax.fori_loop(a_lo, a_hi + 1, body, acc)
    y = (acc + u * bias_ref[...]) * jnp.concatenate([x0_ref[0, 0], x0_ref[0, 1]], axis=1)
    y_ref[0, 0] = y[:, :LANE]
    y_ref[0, 1] = y[:, LANE:]


def _hy_direct_call(vg, x0, h, bias):
    bsz, _, L, _ = vg.shape
    tok = pl.BlockSpec((1, 2, L, LANE), lambda b: (b, 0, 0, 0))
    return pl.pallas_call(
        _hy_direct_kernel,
        grid=(bsz,),
        in_specs=[tok, tok, pl.BlockSpec((2, L, LANE), lambda b: (0, 0, 0)),
                  pl.BlockSpec((1, HY_WIDTH), lambda b: (0, 0))],
        out_specs=tok,
        out_shape=jax.ShapeDtypeStruct(vg.shape, F32),
        scratch_shapes=[pltpu.VMEM((3 * L, HY_WIDTH), F32), pltpu.VMEM((3 * L - 8, HY_WIDTH), F32)],
        compiler_params=_cp("arbitrary"),
        name="hy_direct",
    )(vg, x0, h, bias)


def _out_kernel(x_ref, yg_ref, yh_ref, ys_ref, mod_ref, g2_ref, gf_ref, wo_ref, w1_ref, w3_ref, w2_ref,
                o_ref, *scr, final, col_major):
    m = mod_ref[0]
    if col_major:
        ys_scr, = scr
        for c in range(GRID_W):
            for k in range(SSD_INNER // LANE):
                ys_scr[k, pl.ds(c, ROWS_PER_TILE, stride=GRID_W), :] = ys_ref[0, c, :, k * LANE:(k + 1) * LANE]
        ys = jnp.concatenate([ys_scr[k] for k in range(SSD_INNER // LANE)], axis=1)
    else:
        ys = ys_ref[...]
    yh = jnp.concatenate([yh_ref[0, 0], yh_ref[0, 1]], axis=1)
    tm = x_ref.shape[0]
    nrg = ROW_GROUPS if tm % (8 * ROW_GROUPS) == 0 else 1
    rgs = [slice(r * (tm // nrg), (r + 1) * (tm // nrg)) for r in range(nrg)]
    mix = [(_dot(_bf(yg_ref[r, :]), wo_ref[0:GLA_V])
            + _dot(_bf(yh[r]), wo_ref[GLA_V:GLA_V + HY_WIDTH])
            + _dot(_bf(ys[r]), wo_ref[GLA_V + HY_WIDTH:])) for r in rgs]
    x1 = [x_ref[r, :] + m[2:3] * mx for r, mx in zip(rgs, mix)]
    ms = [jnp.mean(t * t, axis=-1, keepdims=True) for t in x1]
    h = [t * lax.rsqrt(s + EPS) * g2_ref[...] for t, s in zip(x1, ms)]
    h = [_bf(t * (1.0 + m[4:5]) + m[3:4]) for t in h]
    ffn = [None] * nrg
    for j in range(D_FF // FF_TILE):
        cols = slice(j * FF_TILE, (j + 1) * FF_TILE)
        a = [_dot(t, w1_ref[:, cols]) for t in h]
        b = [_dot(t, w3_ref[:, cols]) for t in h]
        part = [_dot(_bf(_silu(u) * v), w2_ref[cols, :]) for u, v in zip(a, b)]
        ffn = [p if f is None else f + p for f, p in zip(ffn, part)]
    for r, t, f in zip(rgs, x1, ffn):
        x2 = t + m[5:6] * f
        if final:
            ms2 = jnp.mean(x2 * x2, axis=-1, keepdims=True)
            x2 = x2 * lax.rsqrt(ms2 + EPS) * gf_ref[...]
        o_ref[r, :] = x2


def _out_call(x2, yg, yh, ys, mods, mod_row, g2, gf, wo, w1, w3, w2, layer, final, col_major):
    t = x2.shape[0]
    seq = yh.shape[2]
    tm = min(TOK_TILE, seq)
    tiles = seq // tm
    c2 = lambda i: (0, 0)
    if col_major:
        assert tm == TOK_TILE
        ys_spec = pl.BlockSpec((1, GRID_W, ROWS_PER_TILE, SSD_INNER), lambda i: (i // tiles, 0, i % tiles, 0))
        extra = [pltpu.VMEM((SSD_INNER // LANE, tm, LANE), F32)]
    else:
        ys = ys.reshape(t, SSD_INNER)
        ys_spec = pl.BlockSpec((tm, SSD_INNER), lambda i: (i, 0))
        extra = []
    resident = lambda shape: pl.BlockSpec((None,) + shape, lambda i: (layer, 0, 0), pipeline_mode=pl.Buffered(1))
    return pl.pallas_call(
        functools.partial(_out_kernel, final=final, col_major=col_major),
        grid=(t // tm,),
        in_specs=[pl.BlockSpec((tm, D_MODEL), lambda i: (i, 0)),
                  pl.BlockSpec((tm, GLA_V), lambda i: (i, 0)),
                  pl.BlockSpec((1, 2, tm, LANE), lambda i: (i // tiles, 0, i % tiles, 0)),
                  ys_spec,
                  pl.BlockSpec((1, 6, D_MODEL), lambda i: (mod_row(i), 0, 0)),
                  pl.BlockSpec((1, D_MODEL), c2),
                  pl.BlockSpec((1, D_MODEL), c2),
                  resident((D_MODEL, D_MODEL)),
                  resident((D_MODEL, D_FF)),
                  resident((D_MODEL, D_FF)),
                  resident((D_FF, D_MODEL))],
        out_specs=pl.BlockSpec((tm, D_MODEL), lambda i: (i, 0)),
        out_shape=jax.ShapeDtypeStruct((t, D_MODEL), F32),
        scratch_shapes=extra,
        compiler_params=_cp("arbitrary"),
        name="out_ffn",
    )(x2, yg, yh, ys, mods, g2.reshape(1, D_MODEL), gf.reshape(1, D_MODEL), wo, w1, w3, w2)


def _pack_w_in(w_in):
    w_in = w_in.astype(BF16)
    z = lambda n: jnp.zeros(w_in.shape[:2] + (n,), w_in.dtype)
    q, k = w_in[..., 0:192], w_in[..., 192:384]
    v, g = w_in[..., 384:768], w_in[..., 768:1152]
    gk = w_in[..., 1152:1184]
    hy = w_in[..., 1184:1952]
    zz = w_in[..., 1952:2336]
    xbc = w_in[..., 2336:3232]
    dt = w_in[..., 3232:3244]
    packed = jnp.concatenate([q, z(64), k, z(64), v, g, gk, z(96), hy, zz, xbc, dt, z(116)], axis=-1)
    return packed.astype(BF16)


def _pad_to(a, shape):
    return jnp.pad(a, [(0, s - d) for d, s in zip(a.shape, shape)])


def _mixers(pg, vg, x0, ps, lw, states):
    g_f0, g_b0, m_f0, m_b0 = states
    o_f, g_f = _gla_call(pg, None, lw['wgk_f'], lw['bgk_f'], None, g_f0, False)
    gla_y, g_b = _gla_call(pg, o_f, lw['wgk_b'], lw['bgk_b'], lw['gla_nw'], g_b0, True)

    L = vg.shape[2]
    h = _hy_filter_call(L, lw['hy_w1'], lw['hy_b1'], lw['hy_w2'], lw['hy_b2'], lw['hy_w3'], lw['hy_freq'],
                        lw['hy_decay'])
    if L == FFT_N // 2:
        m1, m1_real, f2, f2c, m3 = lw['fft']
        tok5 = lambda t: t.reshape(t.shape[0], 2, FFT_HALF, FFT_N1, LANE)
        hspec = _fft2_call(_fft1_call(tok5(h[None]), m1_real, 1), f2, f2c, None)
        vg5 = tok5(vg)
        bmat = _fft2_call(_fft1_call(vg5, m1, 2), f2, f2c, hspec)
        hy_y = _fft3_call(bmat, m3, vg5, tok5(x0), lw['hy_bias']).reshape(vg.shape)
    else:
        hy_y = _hy_direct_call(vg, x0, h, lw['hy_bias'])

    y_f, act, dta, m_f = _ssd_fwd_call(ps, lw['ssd_cw'], lw['ssd_cb'], lw['ssd_dtb'], lw['ssd_alog'], m_f0)
    ssd_y, m_b = _ssd_bwd_call(ps, act, dta, y_f, lw['ssd_dx'], lw['ssd_nw'], m_b0)
    return (gla_y, hy_y, ssd_y), (g_f, g_b, m_f, m_b)


def kernel(x, c, ctx, c_ctx, mod_w, mod_b, norm1_g, norm2_g, w_in, gla_gk_w_f, gla_gk_b_f, gla_gk_w_b, gla_gk_b_b, gla_norm_w, hy_short_w, hy_short_b, hy_w1, hy_b1, hy_w2, hy_b2, hy_w3, hy_freq, hy_decay, hy_bias, ssd_conv_w, ssd_conv_b, ssd_dt_bias_f, ssd_dt_bias_b, ssd_a_log_f, ssd_a_log_b, ssd_d, ssd_norm_w, w_out, ffn_w1, ffn_w3, ffn_w2, final_g):
    bsz, seq, _ = x.shape
    lc = ctx.shape[1]
    c8 = jnp.concatenate([c, c_ctx[None], jnp.zeros((8 - bsz - 1, D_MODEL), F32)], axis=0)
    mods_all = _mod_call(c8, mod_w, mod_b).reshape(DEPTH, 8, 6, D_MODEL)

    w_in_p = _pack_w_in(w_in)
    m1_c, f2_c, f2c_c, m3_c = _fft_consts()
    fft_mats = tuple(_bf(jnp.asarray(m)) for m in (m1_c, np.ascontiguousarray(m1_c[:, :, :FFT_HALF]), f2_c, f2c_c, m3_c))
    w_out_b, w1_b, w3_b, w2_b = _bf(w_out), _bf(ffn_w1), _bf(ffn_w3), _bf(ffn_w2)

    xt = x.reshape(bsz * seq, D_MODEL)
    ct = ctx.reshape(bsz * lc, D_MODEL)
    tiles_per_seq = seq // TOK_TILE
    row_x = lambda i: i // tiles_per_seq
    row_c = lambda i: bsz

    zeros_states = (jnp.zeros((bsz, GLA_DV, 256), F32), jnp.zeros((bsz, GLA_DV, 256), F32),
                    jnp.zeros((bsz, 2 * SSD_STATE, SSD_INNER), F32),
                    jnp.zeros((bsz, 2 * SSD_STATE, SSD_INNER), F32))

    for l in range(DEPTH):
        def gkw(w, off):
            return _bf(_pad_to(jnp.pad(w, ((off, 0), (0, 0))), (LANE, 256)))
        lw = {
            'wgk_f': gkw(gla_gk_w_f[l], 0), 'wgk_b': gkw(gla_gk_w_b[l], GLA_LOWRANK),
            'bgk_f': _pad_to(gla_gk_b_f[l][None], (1, 256)), 'bgk_b': _pad_to(gla_gk_b_b[l][None], (1, 256)),
            'gla_nw': jnp.tile(gla_norm_w[l], GLA_HEADS)[None],
            'hy_sw': _pad_to(hy_short_w[l], (8, PH_W)), 'hy_sb': hy_short_b[l][None],
            'hy_w1': _pad_to(hy_w1[l], (LANE, LANE)), 'hy_b1': _pad_to(hy_b1[l][None], (1, LANE)),
            'hy_w2': _pad_to(hy_w2[l], (LANE, LANE)), 'hy_b2': _pad_to(hy_b2[l][None], (1, LANE)),
            'hy_w3': _pad_to(hy_w3[l], (LANE, HY_WIDTH)), 'hy_freq': _pad_to(hy_freq[l][None], (1, LANE)),
            'hy_decay': hy_decay[l][None], 'hy_bias': hy_bias[l][None],
            'ssd_cw': _pad_to(ssd_conv_w[l], (8, SSD_CONV_DIM)), 'ssd_cb': ssd_conv_b[l][None],
            'ssd_dtb': _pad_to(jnp.concatenate([ssd_dt_bias_f[l], ssd_dt_bias_b[l]])[None], (1, LANE)),
            'ssd_alog': _pad_to(jnp.concatenate([ssd_a_log_f[l], ssd_a_log_b[l]])[None], (1, LANE)),
            'ssd_dx': jnp.repeat(ssd_d[l], SSD_P)[None], 'ssd_nw': ssd_norm_w[l][None],
        }
        lw['fft'] = fft_mats
        mods = mods_all[l]
        in_args = (norm1_g[l], w_in_p, l, lw['hy_sw'], lw['hy_sb'], bsz)
        yc, ctx_states = _mixers(*_in_call(ct, mods, row_c, *in_args, False), lw, zeros_states)
        yx, _ = _mixers(*_in_call(xt, mods, row_x, *in_args, True), lw, ctx_states)
        last = l == DEPTH - 1
        ffn = (w_out_b, w1_b, w3_b, w2_b, l)
        xt = _out_call(xt, yx[0].reshape(-1, GLA_V), yx[1], yx[2],
                       mods, row_x, norm2_g[l], final_g, *ffn, last, True)
        if not last:
            ct = _out_call(ct, yc[0].reshape(-1, GLA_V), yc[1], yc[2],
                           mods, row_c, norm2_g[l], final_g, *ffn, False, False)
    return xt.reshape(bsz, seq, D_MODEL)
```

```python
import functools
import math

import numpy as np
import jax
import jax.numpy as jnp
from jax import lax
from jax.experimental import pallas as pl
from jax.experimental.pallas import tpu as pltpu

F32 = jnp.float32
BF16 = jnp.bfloat16

D_MODEL = 1024
DEPTH = 2
GRID_W = 64
EPS = 1e-6
GLA_V = 384
GLA_DV = 64
GLA_HEADS = 6
GLA_DK = 32
GLA_QK = 192
GLA_LOWRANK = 16
GLA_TAU = 16.0
GLA_CHUNK = 64
GLA_BLOCK = 256
HY_WIDTH = 256
HY_BANDS = 16
HY_EMB = 1 + 2 * HY_BANDS
HY_ORDER = 64
SSD_INNER = 384
SSD_HEADS = 6
SSD_GROUPS = 2
SSD_HG = 3
SSD_P = 64
SSD_STATE = 128
SSD_CONV_DIM = SSD_INNER + 2 * SSD_GROUPS * SSD_STATE
SSD_CHUNK = 128
D_FF = 2816
FF_TILE = 1408
PG_W = 1408
PH_W = 768
PS_W = 1408
LANE = 128
FFT_N1 = 128
FFT_N = FFT_N1 * FFT_N1

VMEM_LIMIT = 56 * 1024 * 1024


def _cp(*sem):
    return pltpu.CompilerParams(dimension_semantics=sem, vmem_limit_bytes=VMEM_LIMIT)


def _bf(x):
    return x.astype(BF16)


def _dot(a, b):
    return jnp.dot(a, b, preferred_element_type=F32)


def _dot_nt(a, b):
    return lax.dot_general(a, b, (((1,), (1,)), ((), ())), preferred_element_type=F32)


def _dot_tn(a, b):
    return lax.dot_general(a, b, (((0,), (0,)), ((), ())), preferred_element_type=F32)


def _split3(x):
    hi = _bf(x)
    r1 = x - hi.astype(F32)
    mid = _bf(r1)
    lo = _bf(r1 - mid.astype(F32))
    return hi, mid, lo


def _dot_sel(sel_bf, x):
    hi, mid, lo = _split3(x)
    return _dot(sel_bf, hi) + _dot(sel_bf, mid) + _dot(sel_bf, lo)


def _dot_sel2(sel_bf, x):
    hi = _bf(x)
    lo = _bf(x - hi.astype(F32))
    return _dot(sel_bf, hi) + _dot(sel_bf, lo)


def _dot_hp(a, b):
    ah = _bf(a)
    al = _bf(a - ah.astype(F32))
    bh = _bf(b)
    bl = _bf(b - bh.astype(F32))
    return _dot(ah, bh) + _dot(ah, bl) + _dot(al, bh)


def _silu(x):
    return x * jax.nn.sigmoid(x)


def _softplus(x):
    return jnp.maximum(x, 0.0) + jnp.log(1.0 + jnp.exp(-jnp.abs(x)))


def _log_sigmoid(x):
    return jnp.minimum(x, 0.0) - jnp.log(1.0 + jnp.exp(-jnp.abs(x)))


def _mod_kernel(c_ref, w_ref, b_ref, o_ref):
    act = _silu(c_ref[...])
    o_ref[0] = _dot(_bf(act), _bf(w_ref[0])) + b_ref[0]


def _mod_call(c8, mod_w, mod_b):
    nt = 1536
    n = mod_w.shape[-1]
    return pl.pallas_call(
        _mod_kernel,
        grid=(DEPTH, n // nt),
        in_specs=[pl.BlockSpec((8, D_MODEL), lambda l, j: (0, 0)),
                  pl.BlockSpec((1, D_MODEL, nt), lambda l, j: (l, 0, j)),
                  pl.BlockSpec((1, 1, nt), lambda l, j: (l, 0, j))],
        out_specs=pl.BlockSpec((1, 8, nt), lambda l, j: (l, 0, j)),
        out_shape=jax.ShapeDtypeStruct((DEPTH, 8, n), F32),
        compiler_params=_cp("arbitrary", "arbitrary"),
        name="adaln_mod",
    )(c8, mod_w, mod_b.reshape(DEPTH, 1, n))


TOK_TILE = 512
ROW_GROUPS = 2
ROWS_PER_TILE = TOK_TILE // GRID_W


def _in_kernel(x_ref, xp_ref, xn_ref, mod_ref, g_ref, w_ref, hw_ref, hb_ref, og_ref, vg_ref, x0_ref, os_ref, *scr,
               col_major, tiles):
    m = mod_ref[0]
    i = pl.program_id(0)

    def modulated(t):
        ms = jnp.mean(t * t, axis=-1, keepdims=True)
        return _bf((t * lax.rsqrt(ms + EPS) * g_ref[...]) * (1.0 + m[1:2]) + m[0:1])

    tm = x_ref.shape[0]
    rgs = [slice(r * (tm // ROW_GROUPS), (r + 1) * (tm // ROW_GROUPS)) for r in range(ROW_GROUPS)]
    h = [modulated(x_ref[r, :]) for r in rgs]
    for r, t in zip(rgs, h):
        og_ref[r, :] = _bf(_dot(t, w_ref[:, 0:PG_W]))
    w_hy = w_ref[:, PG_W:PG_W + PH_W]
    ph = jnp.concatenate([_dot(t, w_hy) for t in h], axis=0)
    ps = jnp.concatenate([_dot(t, w_ref[:, PG_W + PH_W:]) for t in h], axis=0)

    halo = _dot(modulated(jnp.concatenate([xp_ref[...], xn_ref[...]], axis=0)), w_hy)
    prow = halo[7:8] * (i % tiles != 0).astype(F32)
    nrow = halo[8:9] * (i % tiles != tiles - 1).astype(F32)
    ridx = lax.broadcasted_iota(jnp.int32, (tm, 1), 0)
    pm = jnp.where(ridx == 0, prow, pltpu.roll(ph, 1, 0))
    pn = jnp.where(ridx == tm - 1, nrow, pltpu.roll(ph, tm - 1, 0))
    u = hw_ref[0:1] * pm + hw_ref[1:2] * ph + hw_ref[2:3] * pn + hb_ref[...]
    vg = u[:, 2 * HY_WIDTH:] * u[:, HY_WIDTH:2 * HY_WIDTH]
    for hf in range(HY_WIDTH // LANE):
        x0_ref[0, hf] = u[:, hf * LANE:(hf + 1) * LANE]
        vg_ref[0, hf] = vg[:, hf * LANE:(hf + 1) * LANE]

    if col_major:
        ps_scr, = scr
        for k in range(PS_W // LANE):
            for r in range(ROWS_PER_TILE):
                ps_scr[k, pl.ds(r, GRID_W, stride=ROWS_PER_TILE), :] = ps[r * GRID_W:(r + 1) * GRID_W,
                                                                         k * LANE:(k + 1) * LANE]
        for k in range(PS_W // LANE):
            os_ref[0, :, :, k * LANE:(k + 1) * LANE] = ps_scr[k].reshape(GRID_W, ROWS_PER_TILE, LANE)
    else:
        os_ref[...] = ps


def _in_call(x2, mods, mod_row, g, w_all, layer, hy_w8, hy_b, bsz, col_major):
    t = x2.shape[0]
    seq = t // bsz
    tm = min(TOK_TILE, seq)
    tiles = seq // tm
    wtot = PG_W + PH_W + PS_W
    hb = tm // 8
    if col_major:
        assert seq == GRID_W * SSD_CHUNK and ROWS_PER_TILE == 8 and tm == TOK_TILE
        os_spec = pl.BlockSpec((1, GRID_W, ROWS_PER_TILE, PS_W), lambda i: (i // tiles, 0, i % tiles, 0))
        os_shape = jax.ShapeDtypeStruct((bsz, GRID_W, SSD_CHUNK, PS_W), F32)
        scratch = [pltpu.VMEM((PS_W // LANE, tm, LANE), F32)]
    else:
        os_spec = pl.BlockSpec((tm, PS_W), lambda i: (i, 0))
        os_shape = jax.ShapeDtypeStruct((t, PS_W), F32)
        scratch = []
    nblk8 = t // 8
    tok_spec = pl.BlockSpec((1, 2, tm, LANE), lambda i: (i // tiles, 0, i % tiles, 0))
    tok_shape = jax.ShapeDtypeStruct((bsz, 2, seq, LANE), F32)
    pg, vg, x0, ps = pl.pallas_call(
        functools.partial(_in_kernel, col_major=col_major, tiles=tiles),
        grid=(t // tm,),
        in_specs=[pl.BlockSpec((tm, D_MODEL), lambda i: (i, 0)),
                  pl.BlockSpec((8, D_MODEL), lambda i: (jnp.maximum(i * hb - 1, 0), 0)),
                  pl.BlockSpec((8, D_MODEL), lambda i: (jnp.minimum((i + 1) * hb, nblk8 - 1), 0)),
                  pl.BlockSpec((1, 6, D_MODEL), lambda i: (mod_row(i), 0, 0)),
                  pl.BlockSpec((1, D_MODEL), lambda i: (0, 0)),
                  pl.BlockSpec((None, D_MODEL, wtot), lambda i: (layer, 0, 0), pipeline_mode=pl.Buffered(1)),
                  pl.BlockSpec((8, PH_W), lambda i: (0, 0)),
                  pl.BlockSpec((1, PH_W), lambda i: (0, 0))],
        out_specs=[pl.BlockSpec((tm, PG_W), lambda i: (i, 0)), tok_spec, tok_spec, os_spec],
        out_shape=[jax.ShapeDtypeStruct((t, PG_W), BF16), tok_shape, tok_shape, os_shape],
        scratch_shapes=scratch,
        compiler_params=_cp("arbitrary"),
        name="in_proj",
    )(x2, x2, x2, mods, g.reshape(1, D_MODEL), w_all, hy_w8, hy_b)
    return (pg.reshape(bsz, seq, PG_W), vg, x0, ps.reshape(bsz, seq // SSD_CHUNK, SSD_CHUNK, PS_W))


def _gla_kernel(*refs, reverse, nblk):
    if reverse:
        (p_ref, of_ref, wgk_ref, bgk_ref, nw_ref, s0_ref, y_ref, sf_ref, st_scr) = refs
    else:
        (p_ref, wgk_ref, bgk_ref, s0_ref, of_ref, sf_ref, st_scr) = refs
    i = pl.program_id(1)
    tb = GLA_BLOCK

    @pl.when(i == 0)
    def _():
        st_scr[...] = s0_ref[...]

    ri = lax.broadcasted_iota(jnp.int32, (tb, tb), 0)
    ci = lax.broadcasted_iota(jnp.int32, (tb, tb), 1)
    same = (ri // GLA_CHUNK) == (ci // GLA_CHUNK)
    tri = same & ((ci >= ri) if reverse else (ci <= ri))
    tri_bf = jnp.where(tri, 1.0, 0.0).astype(BF16)
    lane = lax.broadcasted_iota(jnp.int32, (1, 256), 1)
    hms = [(lane // GLA_DK) == h for h in range(GLA_HEADS)]

    nseq = p_ref.shape[0]
    seqs = range(nseq)
    ps = [p_ref[s_] for s_ in seqs]
    os_, sts = _gla_blocks(ps, [st_scr[s_] for s_ in seqs], wgk_ref[...], bgk_ref[...], tri, tri_bf, lane, hms,
                           reverse)
    for s_ in seqs:
        st_scr[s_] = sts[s_]

    @pl.when(i == nblk - 1)
    def _():
        for s_ in seqs:
            sf_ref[s_] = sts[s_]

    if not reverse:
        for s_ in seqs:
            of_ref[s_] = os_[s_]
    else:
        r2 = lax.broadcasted_iota(jnp.int32, (GLA_V, GLA_V), 0) // GLA_DV
        c2 = lax.broadcasted_iota(jnp.int32, (GLA_V, GLA_V), 1) // GLA_DV
        ind = jnp.where(r2 == c2, 1.0, 0.0).astype(BF16)
        ot = [of_ref[s_] + os_[s_] for s_ in seqs]
        sq = [t * t for t in ot]
        sh = [_bf(t) for t in sq]
        sl = [_bf(sq[s_] - sh[s_].astype(F32)) for s_ in seqs]
        ms = [(_dot(sh[s_], ind) + _dot(sl[s_], ind)) * (1.0 / GLA_DV) for s_ in seqs]
        for s_ in seqs:
            g = ps[s_][:, 896:1280].astype(F32)
            y_ref[s_] = ot[s_] * lax.rsqrt(ms[s_] + EPS) * nw_ref[...] * _silu(g)


def _gla_blocks(ps, sts, wgk, bgk, tri, tri_bf, lane, hms, reverse):
    tb = GLA_BLOCK
    nch = tb // GLA_CHUNK
    seqs = range(len(ps))
    k = [p[:, 256:512].astype(F32) for p in ps]
    pre = [_dot(p[:, 1280:1408], wgk) + bgk for p in ps]
    la = [_log_sigmoid(t) * (1.0 / GLA_TAU) for t in pre]
    b = [_dot_sel2(tri_bf, t) for t in la]
    qd = [_bf(ps[s][:, 0:256].astype(F32) * ((GLA_DK ** -0.5) * jnp.exp(b[s]))) for s in seqs]
    ki = [_bf(k[s] * jnp.exp(-b[s])) for s in seqs]
    vb = [p[:, 512:896] for p in ps]

    lhs = [jnp.concatenate([jnp.where(hm, t, jnp.zeros_like(t)) for hm in hms], axis=0) for t in qd]
    sc = [_dot_nt(lhs[s], ki[s]) for s in seqs]
    pm = [jnp.concatenate([_bf(jnp.where(tri, t[h * tb:(h + 1) * tb], 0.0)) for h in range(GLA_HEADS)], axis=0)
          for t in sc]
    ra = [_dot(pm[s][:4 * tb], vb[s][:, 0:256]) for s in seqs]
    rb = [_dot(pm[s][4 * tb:], vb[s][:, 256:384]) for s in seqs]
    hl = lane // GLA_DV
    o_intra = []
    for s in seqs:
        oa = jnp.zeros((tb, 256), F32)
        for h in range(4):
            oa = oa + jnp.where(hl == h, ra[s][h * tb:(h + 1) * tb], 0.0)
        ob = jnp.where(hl[:, :LANE] == 0, rb[s][:tb], rb[s][tb:])
        o_intra.append(jnp.concatenate([oa, ob], axis=1))

    sts = list(sts)
    outs = [[None] * nch for _ in seqs]
    order = range(nch - 1, -1, -1) if reverse else range(nch)
    for c in order:
        r0 = c * GLA_CHUNK
        rows = slice(r0, r0 + GLA_CHUNK)
        edge = r0 if reverse else r0 + GLA_CHUNK - 1
        bl = [t[edge:edge + 1] for t in b]
        kd = [_bf(k[s][rows] * jnp.exp(bl[s] - b[s][rows])) for s in seqs]
        stbd = [_bf(jnp.concatenate([jnp.where(hm, t, 0.0) for hm in hms], axis=0)) for t in sts]
        for s in seqs:
            outs[s][c] = _dot_nt(qd[s][rows], stbd[s])
        full = [_dot_tn(vb[s][rows], kd[s]) for s in seqs]
        for s in seqs:
            ds = jnp.zeros((GLA_DV, 256), F32)
            for h in range(GLA_HEADS):
                ds = ds + jnp.where(hms[h], full[s][h * GLA_DV:(h + 1) * GLA_DV], 0.0)
            sts[s] = jnp.exp(bl[s]) * sts[s] + ds
    return [o_intra[s] + jnp.concatenate(outs[s], axis=0) for s in seqs], sts


SEQ_PER_STEP = 4
GLA_SEQ_PER_STEP = 4


def _gla_call(pg, o_f, wgk, bgk, nw, s0, reverse):
    bsz, L, _ = pg.shape
    nblk = L // GLA_BLOCK
    ns = GLA_SEQ_PER_STEP
    blk = (lambda b, i: (b, nblk - 1 - i, 0)) if reverse else (lambda b, i: (b, i, 0))
    const2 = lambda b, i: (0, 0)
    st_spec = pl.BlockSpec((ns, GLA_DV, 256), lambda b, i: (b, 0, 0))
    p_spec = pl.BlockSpec((ns, GLA_BLOCK, PG_W), blk)
    o_spec = pl.BlockSpec((ns, GLA_BLOCK, GLA_V), blk)
    if reverse:
        in_specs = [p_spec, o_spec, pl.BlockSpec((LANE, 256), const2), pl.BlockSpec((1, 256), const2),
                    pl.BlockSpec((1, GLA_V), const2), st_spec]
        args = (pg, o_f, wgk, bgk, nw, s0)
    else:
        in_specs = [p_spec, pl.BlockSpec((LANE, 256), const2), pl.BlockSpec((1, 256), const2), st_spec]
        args = (pg, wgk, bgk, s0)
    return pl.pallas_call(
        functools.partial(_gla_kernel, reverse=reverse, nblk=nblk),
        grid=(bsz // ns, nblk),
        in_specs=in_specs,
        out_specs=[o_spec, st_spec],
        out_shape=[jax.ShapeDtypeStruct((bsz, L, GLA_V), F32),
                   jax.ShapeDtypeStruct((bsz, GLA_DV, 256), F32)],
        scratch_shapes=[pltpu.VMEM((ns, GLA_DV, 256), F32)],
        compiler_params=_cp("arbitrary", "arbitrary"),
        name="gla_bwd" if reverse else "gla_fwd",
    )(*args)


def _expand_heads(t, lo):
    r = t.shape[0]
    lane = lax.broadcasted_iota(jnp.int32, (1, LANE), 1)
    tiles = []
    for j in range(SSD_HEADS // 2):
        a = jnp.broadcast_to(t[:, lo + 2 * j:lo + 2 * j + 1], (r, LANE))
        b = jnp.broadcast_to(t[:, lo + 2 * j + 1:lo + 2 * j + 2], (r, LANE))
        tiles.append(jnp.where(lane < SSD_P, a, b))
    return jnp.concatenate(tiles, axis=1)


def _ssd_prep(cur, prow, nrow, cw, cb, dtb, alog):
    q_ = SSD_CHUNK
    x = cur[:, 384:1280]
    ridx = lax.broadcasted_iota(jnp.int32, (q_, 1), 0)
    xm = jnp.where(ridx == 0, prow, pltpu.roll(x, 1, 0))
    xp = jnp.where(ridx == q_ - 1, nrow, pltpu.roll(x, q_ - 1, 0))
    act = _silu(cw[0:1] * xm + cw[1:2] * x + cw[2:3] * xp + cb)
    dtt = _softplus(cur[:, 1280:1408] + dtb)
    a = -jnp.exp(alog) * dtt
    return act, dtt, a


def _ssd_scans(xs, bmb, cmb, dtt, a, sts, tri, tri_bf, reverse):
    q_ = SSD_CHUNK
    lo = SSD_HEADS if reverse else 0
    seqs = range(len(xs))
    cs = [_dot_sel(tri_bf, t) for t in a]
    cst = [t.T for t in cs]
    dtT = [t.T for t in dtt]
    edge = 0 if reverse else q_ - 1
    cs_last = [t[edge:edge + 1] for t in cs]
    grp = lambda t, g: t[:, g * SSD_STATE:(g + 1) * SSD_STATE]
    cbs = [[_dot_nt(grp(cmb[s], g), grp(bmb[s], g)) for g in range(SSD_GROUPS)] for s in seqs]
    ms = [[] for _ in seqs]
    for h in range(SSD_HEADS):
        l = lo + h
        for s in seqs:
            seg = cs[s][:, l:l + 1] - cst[s][l:l + 1, :]
            dec = jnp.exp(jnp.where(tri, seg, -jnp.inf))
            ms[s].append(_bf(cbs[s][h // SSD_HG] * dec * dtT[s][l:l + 1, :]))
    mst = [jnp.concatenate(t, axis=0) for t in ms]
    xsb = [_bf(t) for t in xs]
    ra = [_dot(mst[s][:4 * q_], xsb[s][:, 0:256]) for s in seqs]
    rb = [_dot(mst[s][4 * q_:], xsb[s][:, 256:384]) for s in seqs]
    lane = lax.broadcasted_iota(jnp.int32, (1, 256), 1)
    hl = lane // SSD_P
    cs_x = [_expand_heads(t, lo) for t in cs]
    csl_x = [_expand_heads(t, lo) for t in cs_last]
    dt_x = [_expand_heads(t, lo) for t in dtt]
    ystate = [_dot(cmb[s], _bf(sts[s])) for s in seqs]
    xw = [_bf(xs[s] * (jnp.exp(csl_x[s] - cs_x[s]) * dt_x[s])) for s in seqs]
    full = [_dot_tn(bmb[s], xw[s]) for s in seqs]
    r2 = lax.broadcasted_iota(jnp.int32, (2 * SSD_STATE, SSD_INNER), 0) // SSD_STATE
    c2 = lax.broadcasted_iota(jnp.int32, (2 * SSD_STATE, SSD_INNER), 1) // (SSD_HG * SSD_P)
    ys, new_sts = [], []
    for s in seqs:
        ya = jnp.zeros((q_, 256), F32)
        for h in range(4):
            ya = ya + jnp.where(hl == h, ra[s][h * q_:(h + 1) * q_], 0.0)
        yb = jnp.where(hl[:, :LANE] == 0, rb[s][:q_], rb[s][q_:])
        ys.append(jnp.concatenate([ya, yb], axis=1) + jnp.exp(cs_x[s]) * ystate[s])
        new_sts.append(jnp.exp(csl_x[s]) * sts[s] + jnp.where(r2 == c2, full[s], 0.0))
    return ys, new_sts


def _ssd_kernel(*refs, reverse, nchunk):
    if reverse:
        (z_ref, act_ref, dta_ref, yf_ref, dx_ref, nw_ref, s0_ref, y_ref, sf_ref, st_scr) = refs
    else:
        (cur_ref, prev_ref, next_ref, cw_ref, cb_ref, dtb_ref, alog_ref, s0_ref,
         yf_ref, act_ref, dta_ref, sf_ref, st_scr) = refs
    i = pl.program_id(1)
    c = (nchunk - 1 - i) if reverse else i
    q_ = SSD_CHUNK

    @pl.when(i == 0)
    def _():
        st_scr[...] = s0_ref[...]

    ri = lax.broadcasted_iota(jnp.int32, (q_, q_), 0)
    ci = lax.broadcasted_iota(jnp.int32, (q_, q_), 1)
    tri = (ci >= ri) if reverse else (ci <= ri)
    tri_bf = jnp.where(tri, 1.0, 0.0).astype(BF16)

    seqs = range(st_scr.shape[0])
    if reverse:
        actb = [act_ref[s_, 0] for s_ in seqs]
        xs = [t[:, 0:384].astype(F32) for t in actb]
        dtt = [dta_ref[s_, 0][:, :LANE] for s_ in seqs]
        a = [dta_ref[s_, 0][:, LANE:] for s_ in seqs]
    else:
        has_prev = (c > 0).astype(F32)
        has_next = (c < nchunk - 1).astype(F32)
        actb, xs, dtt, a = [], [], [], []
        for s_ in seqs:
            prow = prev_ref[s_, 0][7:8, 384:1280] * has_prev
            nrow = next_ref[s_, 0][0:1, 384:1280] * has_next
            act, dt_, a_ = _ssd_prep(cur_ref[s_, 0], prow, nrow, cw_ref, cb_ref[...], dtb_ref[...], alog_ref[...])
            actb.append(_bf(act))
            act_ref[s_, 0] = actb[s_]
            dta_ref[s_, 0] = jnp.concatenate([dt_, a_], axis=1)
            xs.append(act[:, 0:384])
            dtt.append(dt_)
            a.append(a_)
    ys, sts = _ssd_scans(xs, [t[:, 384:640] for t in actb], [t[:, 640:896] for t in actb], dtt, a,
                         [st_scr[s_] for s_ in seqs], tri, tri_bf, reverse)
    for s_ in seqs:
        st_scr[s_] = sts[s_]

    @pl.when(i == nchunk - 1)
    def _():
        for s_ in seqs:
            sf_ref[s_] = sts[s_]

    for s_ in seqs:
        if not reverse:
            yf_ref[s_, 0] = ys[s_]
        else:
            yt = yf_ref[s_, 0] + ys[s_] + dx_ref[...] * xs[s_]
            yz = yt * _silu(z_ref[s_, 0])
            l384 = lax.broadcasted_iota(jnp.int32, (1, SSD_INNER), 1)
            g0 = l384 < (SSD_INNER // SSD_GROUPS)
            sq = yz * yz
            m0 = jnp.sum(jnp.where(g0, sq, 0.0), axis=-1, keepdims=True)
            m1 = jnp.sum(jnp.where(g0, 0.0, sq), axis=-1, keepdims=True)
            msq = jnp.where(g0, m0, m1) * (1.0 / (SSD_INNER // SSD_GROUPS))
            y_ref[s_, 0] = yz * lax.rsqrt(msq + EPS) * nw_ref[...]


def _ssd_fwd_call(ps, cw, cb, dtb, alog, s0):
    bsz, nchunk, _, _ = ps.shape
    ns = SEQ_PER_STEP
    cur_map = lambda b, i: (b, i, 0, 0)
    prev_map = lambda b, i: (b, jnp.maximum(i - 1, 0), SSD_CHUNK // 8 - 1, 0)
    next_map = lambda b, i: (b, jnp.minimum(i + 1, nchunk - 1), 0, 0)
    const2 = lambda b, i: (0, 0)
    chunk = lambda w: pl.BlockSpec((ns, 1, SSD_CHUNK, w), cur_map)
    st_spec = pl.BlockSpec((ns, 2 * SSD_STATE, SSD_INNER), lambda b, i: (b, 0, 0))
    return pl.pallas_call(
        functools.partial(_ssd_kernel, reverse=False, nchunk=nchunk),
        grid=(bsz // ns, nchunk),
        in_specs=[chunk(PS_W), pl.BlockSpec((ns, 1, 8, PS_W), prev_map), pl.BlockSpec((ns, 1, 8, PS_W), next_map),
                  pl.BlockSpec((8, SSD_CONV_DIM), const2), pl.BlockSpec((1, SSD_CONV_DIM), const2),
                  pl.BlockSpec((1, LANE), const2), pl.BlockSpec((1, LANE), const2), st_spec],
        out_specs=[chunk(SSD_INNER), chunk(SSD_CONV_DIM), chunk(2 * LANE), st_spec],
        out_shape=[jax.ShapeDtypeStruct((bsz, nchunk, SSD_CHUNK, SSD_INNER), F32),
                   jax.ShapeDtypeStruct((bsz, nchunk, SSD_CHUNK, SSD_CONV_DIM), BF16),
                   jax.ShapeDtypeStruct((bsz, nchunk, SSD_CHUNK, 2 * LANE), F32),
                   jax.ShapeDtypeStruct((bsz, 2 * SSD_STATE, SSD_INNER), F32)],
        scratch_shapes=[pltpu.VMEM((ns, 2 * SSD_STATE, SSD_INNER), F32)],
        compiler_params=_cp("arbitrary", "arbitrary"),
        name="ssd_fwd",
    )(ps, ps, ps, cw, cb, dtb, alog, s0)


def _ssd_bwd_call(ps, act, dta, y_f, dx, nw, s0):
    bsz, nchunk, _, _ = ps.shape
    ns = SEQ_PER_STEP
    cur_map = lambda b, i: (b, nchunk - 1 - i, 0, 0)
    const2 = lambda b, i: (0, 0)
    chunk = lambda w: pl.BlockSpec((ns, 1, SSD_CHUNK, w), cur_map)
    st_spec = pl.BlockSpec((ns, 2 * SSD_STATE, SSD_INNER), lambda b, i: (b, 0, 0))
    return pl.pallas_call(
        functools.partial(_ssd_kernel, reverse=True, nchunk=nchunk),
        grid=(bsz // ns, nchunk),
        in_specs=[chunk(SSD_INNER), chunk(SSD_CONV_DIM), chunk(2 * LANE), chunk(SSD_INNER),
                  pl.BlockSpec((1, SSD_INNER), const2), pl.BlockSpec((1, SSD_INNER), const2), st_spec],
        out_specs=[chunk(SSD_INNER), st_spec],
        out_shape=[jax.ShapeDtypeStruct((bsz, nchunk, SSD_CHUNK, SSD_INNER), F32),
                   jax.ShapeDtypeStruct((bsz, 2 * SSD_STATE, SSD_INNER), F32)],
        scratch_shapes=[pltpu.VMEM((ns, 2 * SSD_STATE, SSD_INNER), F32)],
        compiler_params=_cp("arbitrary", "arbitrary"),
        name="ssd_bwd",
    )(ps, act, dta, y_f, dx, nw, s0)


HY_SLOT = LANE // 2


def _hy_filter_kernel(z_ref, w1_ref, b1_ref, w2_ref, b2_ref, w3_ref, fr_ref, dec_ref, h_ref):
    z = z_ref[...]
    half = z.shape[0]
    fr = fr_ref[...]
    h1 = jnp.sin(fr * (_dot_hp(z, w1_ref[...]) + b1_ref[...]))
    h2 = jnp.sin(fr * (_dot_hp(h1, w2_ref[...]) + b2_ref[...]))
    for slot in range(2):
        h = _dot_hp(h2, w3_ref[slot])
        rel = z[:, slot * HY_SLOT:slot * HY_SLOT + 1]
        h = h * jnp.exp(-2.0 * jnp.abs(rel) * dec_ref[...])
        for hf in range(HY_WIDTH // LANE):
            h_ref[hf, slot * half:(slot + 1) * half, :] = h[:, hf * LANE:(hf + 1) * LANE]


def _hy_features(L, tl):
    t = jnp.arange(L, dtype=F32)
    rel = (t - (L // 2)) / L
    bands = jnp.linspace(1e-4, HY_BANDS - 1, HY_BANDS, dtype=F32)
    ang = 2.0 * math.pi * rel[:, None] * bands
    z = jnp.concatenate([rel[:, None], jnp.cos(ang), -jnp.sin(ang)], axis=-1)
    z = jnp.pad(z, ((0, 0), (0, HY_SLOT - HY_EMB)))
    return z.reshape(L // tl, 2, tl // 2, HY_SLOT).transpose(0, 2, 1, 3).reshape(L // 2, LANE)


def _hy_filter_call(L, w1bd, b1t, w2bd, b2t, w3s, frt, dec):
    tl = min(1024, L)
    z = _hy_features(L, tl)
    c2 = lambda i: (0, 0)
    return pl.pallas_call(
        _hy_filter_kernel,
        grid=(L // tl,),
        in_specs=[pl.BlockSpec((tl // 2, LANE), lambda i: (i, 0)),
                  pl.BlockSpec((LANE, LANE), c2), pl.BlockSpec((1, LANE), c2),
                  pl.BlockSpec((LANE, LANE), c2), pl.BlockSpec((1, LANE), c2),
                  pl.BlockSpec((2, LANE, HY_WIDTH), lambda i: (0, 0, 0)), pl.BlockSpec((1, LANE), c2),
                  pl.BlockSpec((1, HY_WIDTH), c2)],
        out_specs=pl.BlockSpec((2, tl, LANE), lambda i: (0, i, 0)),
        out_shape=jax.ShapeDtypeStruct((2, L, LANE), F32),
        compiler_params=_cp("arbitrary"),
        name="hy_filter",
    )(z, w1bd, b1t, w2bd, b2t, w3s, frt, dec)


@functools.lru_cache(maxsize=None)
def _fft_consts():
    n1 = FFT_N1
    half = n1 // 2
    k = np.arange(n1, dtype=np.float64)
    n2 = k[:, None, None]
    k1 = k[None, :, None]
    nn = np.arange(half, dtype=np.float64)[None, None, :]
    ang = -2.0 * np.pi * (n2 * k1 / FFT_N + nn * k1 / n1)
    mr, mi = np.cos(ang), np.sin(ang)
    m1 = np.concatenate([np.concatenate([mr, -mi], axis=2), np.concatenate([mi, mr], axis=2)], axis=1)
    ang2 = -2.0 * np.pi * np.outer(k, k) / n1
    fr, fi = np.cos(ang2), np.sin(ang2)
    f2 = np.block([[fr, -fi], [fi, fr]])
    f2c = np.block([[fr, fi], [-fi, fr]])
    no = (np.arange(half, dtype=np.float64) + n1 // 4)[None, :, None]
    kk = k[None, None, :]
    ang3 = 2.0 * np.pi * (n2 * kk / FFT_N + no * kk / n1)
    ir, ii = np.cos(ang3) / FFT_N, np.sin(ang3) / FFT_N
    m3 = np.concatenate([np.concatenate([ir, -ii], axis=2), np.concatenate([ii, ir], axis=2)], axis=1)
    return tuple(np.asarray(m, dtype=np.float32) for m in (m1, f2, f2c, m3))


FFT_NB = 16
FFT_HALF = FFT_N1 // 2


def _strided_rows(ref2d, start, n):
    return ref2d[pl.ds(start, n, stride=FFT_NB), :]


def _tok_rows(ref, s, t):
    return jnp.concatenate([ref[s, hf, :, t, :] for hf in range(2)], axis=1)


def _stage_f32(dst, src_ref):
    v = src_ref[0].astype(F32).reshape(FFT_N1 * FFT_NB, HY_WIDTH)
    dst[0] = v[:, :LANE]
    dst[1] = v[:, LANE:]


def _staged_rows(scr, t):
    return jnp.concatenate([scr[hf, pl.ds(t, FFT_N1, stride=FFT_NB), :] for hf in range(2)], axis=1)


def _fft1_kernel(u_ref, m_ref, a_ref, *, nsig):
    for t in range(FFT_NB):
        rhs = _bf(jnp.concatenate([_tok_rows(u_ref, s, t) for s in range(nsig)], axis=0))
        a_ref[0, t] = _bf(_dot(m_ref[t], rhs))


def _fft1_call(u, m1, nsig):
    npair = u.shape[0] // nsig
    return pl.pallas_call(
        functools.partial(_fft1_kernel, nsig=nsig),
        grid=(npair, FFT_N1 // FFT_NB),
        in_specs=[pl.BlockSpec((nsig, 2, FFT_HALF, FFT_NB, LANE), lambda p, j: (p, 0, 0, j, 0)),
                  pl.BlockSpec((FFT_NB, 2 * FFT_N1, FFT_HALF * nsig), lambda p, j: (j, 0, 0))],
        out_specs=pl.BlockSpec((1, FFT_NB, 2 * FFT_N1, HY_WIDTH), lambda p, j: (p, j, 0, 0)),
        out_shape=jax.ShapeDtypeStruct((npair, FFT_N1, 2 * FFT_N1, HY_WIDTH), BF16),
        compiler_params=_cp("arbitrary", "arbitrary"),
        name="hy_fft1",
    )(u, m1)


def _fft2_kernel(*refs, spectrum):
    if spectrum:
        ar_ref, ai_ref, f_ref, o_ref, sr, si = refs
    else:
        ar_ref, ai_ref, f_ref, fc_ref, h_ref, o_ref, sr, si = refs
    _stage_f32(sr, ar_ref)
    _stage_f32(si, ai_ref)
    for t in range(FFT_NB):
        rhs = _bf(jnp.concatenate([_staged_rows(sr, t), _staged_rows(si, t)], axis=0))
        x = _dot(f_ref[...], rhs)
        if spectrum:
            o_ref[t] = x
        else:
            xr, xi = x[:FFT_N1], x[FFT_N1:]
            hr, hi = h_ref[t, :FFT_N1], h_ref[t, FFT_N1:]
            y = jnp.concatenate([xr * hr - xi * hi, xr * hi + xi * hr], axis=0)
            o_ref[0, t] = _bf(_dot(fc_ref[...], _bf(y)))


def _fft2_call(a, f2, f2c, hspec):
    npair = a.shape[0]
    av = a
    nj = FFT_N1 // FFT_NB
    c2 = lambda p, j: (0, 0)
    in_specs = [pl.BlockSpec((1, FFT_N1, FFT_NB, HY_WIDTH), lambda p, j: (p, 0, j, 0)),
                pl.BlockSpec((1, FFT_N1, FFT_NB, HY_WIDTH), lambda p, j: (p, 0, nj + j, 0)),
                pl.BlockSpec((2 * FFT_N1, 2 * FFT_N1), c2)]
    staging = [pltpu.VMEM((2, FFT_N1 * FFT_NB, LANE), F32), pltpu.VMEM((2, FFT_N1 * FFT_NB, LANE), F32)]
    if hspec is None:
        return pl.pallas_call(
            functools.partial(_fft2_kernel, spectrum=True),
            grid=(1, nj),
            in_specs=in_specs,
            out_specs=pl.BlockSpec((FFT_NB, 2 * FFT_N1, HY_WIDTH), lambda p, j: (j, 0, 0)),
            out_shape=jax.ShapeDtypeStruct((FFT_N1, 2 * FFT_N1, HY_WIDTH), F32),
            scratch_shapes=staging,
            compiler_params=_cp("arbitrary", "arbitrary"),
            name="hy_fft2_spec",
        )(av, av, f2)
    in_specs += [pl.BlockSpec((2 * FFT_N1, 2 * FFT_N1), c2),
                 pl.BlockSpec((FFT_NB, 2 * FFT_N1, HY_WIDTH), lambda p, j: (j, 0, 0))]
    return pl.pallas_call(
        functools.partial(_fft2_kernel, spectrum=False),
        grid=(npair, nj),
        in_specs=in_specs,
        out_specs=pl.BlockSpec((1, FFT_NB, 2 * FFT_N1, HY_WIDTH), lambda p, j: (p, j, 0, 0)),
        out_shape=jax.ShapeDtypeStruct((npair, FFT_N1, 2 * FFT_N1, HY_WIDTH), BF16),
        scratch_shapes=staging,
        compiler_params=_cp("arbitrary", "arbitrary"),
        name="hy_fft2",
    )(av, av, f2, f2c, hspec)


def _fft3_kernel(br_ref, bi_ref, m_ref, vg_ref, x0_ref, bias_ref, y_ref, sr, si):
    _stage_f32(sr, br_ref)
    _stage_f32(si, bi_ref)
    for t in range(FFT_NB):
        rhs = _bf(jnp.concatenate([_staged_rows(sr, t), _staged_rows(si, t)], axis=0))
        out = _dot(m_ref[t], rhs)
        for s in range(2):
            conv = out[s * FFT_HALF:(s + 1) * FFT_HALF]
            y = (conv + _tok_rows(vg_ref, s, t) * bias_ref[...]) * _tok_rows(x0_ref, s, t)
            for hf in range(2):
                y_ref[s, hf, :, t, :] = y[:, hf * LANE:(hf + 1) * LANE]


def _fft3_call(bmat, m3, vg, x0, bias):
    npair = bmat.shape[0]
    nj = FFT_N1 // FFT_NB
    tok_spec = pl.BlockSpec((2, 2, FFT_HALF, FFT_NB, LANE), lambda p, j: (p, 0, 0, j, 0))
    return pl.pallas_call(
        _fft3_kernel,
        grid=(npair, nj),
        in_specs=[pl.BlockSpec((1, FFT_N1, FFT_NB, HY_WIDTH), lambda p, j: (p, 0, j, 0)),
                  pl.BlockSpec((1, FFT_N1, FFT_NB, HY_WIDTH), lambda p, j: (p, 0, nj + j, 0)),
                  pl.BlockSpec((FFT_NB, FFT_N1, 2 * FFT_N1), lambda p, j: (j, 0, 0)),
                  tok_spec, tok_spec,
                  pl.BlockSpec((1, HY_WIDTH), lambda p, j: (0, 0))],
        out_specs=tok_spec,
        out_shape=jax.ShapeDtypeStruct(vg.shape, F32),
        scratch_shapes=[pltpu.VMEM((2, FFT_N1 * FFT_NB, LANE), F32), pltpu.VMEM((2, FFT_N1 * FFT_NB, LANE), F32)],
        compiler_params=_cp("arbitrary", "arbitrary"),
        name="hy_fft3",
    )(bmat, bmat, m3, vg, x0, bias)


def _hy_direct_kernel(vg_ref, x0_ref, h_ref, bias_ref, y_ref, pad_scr, sh_scr):
    L = vg_ref.shape[2]
    u = jnp.concatenate([vg_ref[0, 0], vg_ref[0, 1]], axis=1)
    pad_scr[...] = jnp.zeros_like(pad_scr)
    pad_scr[L:2 * L, :] = u
    top = L + L // 2
    acc = jnp.zeros((L, HY_WIDTH), F32)
    for r in range(8):
        sh_scr[...] = pad_scr[r:r + 3 * L - 8, :]
        a_lo = -(-(top - L + 1 - r) // 8)
        a_hi = (top - r) // 8

        def body(a, acc, r=r):
            m = top - (a * 8 + r)
            tap = jnp.concatenate([h_ref[0, pl.ds(m, 1), :], h_ref[1, pl.ds(m, 1), :]], axis=1)
            return acc + tap * sh_scr[pl.ds(pl.multiple_of(a * 8, 8), L), :]

        acc = lax.fori_loop(a_lo, a_hi + 1, body, acc, unroll=4)
    y = (acc + u * bias_ref[...]) * jnp.concatenate([x0_ref[0, 0], x0_ref[0, 1]], axis=1)
    y_ref[0, 0] = y[:, :LANE]
    y_ref[0, 1] = y[:, LANE:]


def _hy_direct_call(vg, x0, h, bias):
    bsz, _, L, _ = vg.shape
    tok = pl.BlockSpec((1, 2, L, LANE), lambda b: (b, 0, 0, 0))
    return pl.pallas_call(
        _hy_direct_kernel,
        grid=(bsz,),
        in_specs=[tok, tok, pl.BlockSpec((2, L, LANE), lambda b: (0, 0, 0)),
                  pl.BlockSpec((1, HY_WIDTH), lambda b: (0, 0))],
        out_specs=tok,
        out_shape=jax.ShapeDtypeStruct(vg.shape, F32),
        scratch_shapes=[pltpu.VMEM((3 * L, HY_WIDTH), F32), pltpu.VMEM((3 * L - 8, HY_WIDTH), F32)],
        compiler_params=_cp("arbitrary"),
        name="hy_direct",
    )(vg, x0, h, bias)


def _out_kernel(x_ref, yg_ref, yh_ref, ys_ref, mod_ref, g2_ref, gf_ref, wo_ref, w1_ref, w3_ref, w2_ref,
                o_ref, *scr, final, col_major):
    m = mod_ref[0]
    if col_major:
        ys_scr, = scr
        for c in range(GRID_W):
            for k in range(SSD_INNER // LANE):
                ys_scr[k, pl.ds(c, ROWS_PER_TILE, stride=GRID_W), :] = ys_ref[0, c, :, k * LANE:(k + 1) * LANE]
        ys = jnp.concatenate([ys_scr[k] for k in range(SSD_INNER // LANE)], axis=1)
    else:
        ys = ys_ref[...]
    yh = jnp.concatenate([yh_ref[0, 0], yh_ref[0, 1]], axis=1)
    tm = x_ref.shape[0]
    nrg = ROW_GROUPS if tm % (8 * ROW_GROUPS) == 0 else 1
    rgs = [slice(r * (tm // nrg), (r + 1) * (tm // nrg)) for r in range(nrg)]
    mix = [(_dot(_bf(yg_ref[r, :]), wo_ref[0:GLA_V])
            + _dot(_bf(yh[r]), wo_ref[GLA_V:GLA_V + HY_WIDTH])
            + _dot(_bf(ys[r]), wo_ref[GLA_V + HY_WIDTH:])) for r in rgs]
    x1 = [x_ref[r, :] + m[2:3] * mx for r, mx in zip(rgs, mix)]
    ms = [jnp.mean(t * t, axis=-1, keepdims=True) for t in x1]
    h = [t * lax.rsqrt(s + EPS) * g2_ref[...] for t, s in zip(x1, ms)]
    h = [_bf(t * (1.0 + m[4:5]) + m[3:4]) for t in h]
    ffn = [None] * nrg
    for j in range(D_FF // FF_TILE):
        cols = slice(j * FF_TILE, (j + 1) * FF_TILE)
        a = [_dot(t, w1_ref[:, cols]) for t in h]
        b = [_dot(t, w3_ref[:, cols]) for t in h]
        part = [_dot(_bf(_silu(u) * v), w2_ref[cols, :]) for u, v in zip(a, b)]
        ffn = [p if f is None else f + p for f, p in zip(ffn, part)]
    for r, t, f in zip(rgs, x1, ffn):
        x2 = t + m[5:6] * f
        if final:
            ms2 = jnp.mean(x2 * x2, axis=-1, keepdims=True)
            x2 = x2 * lax.rsqrt(ms2 + EPS) * gf_ref[...]
        o_ref[r, :] = x2


def _out_call(x2, yg, yh, ys, mods, mod_row, g2, gf, wo, w1, w3, w2, layer, final, col_major):
    t = x2.shape[0]
    seq = yh.shape[2]
    tm = min(TOK_TILE, seq)
    tiles = seq // tm
    c2 = lambda i: (0, 0)
    if col_major:
        assert tm == TOK_TILE
        ys_spec = pl.BlockSpec((1, GRID_W, ROWS_PER_TILE, SSD_INNER), lambda i: (i // tiles, 0, i % tiles, 0))
        extra = [pltpu.VMEM((SSD_INNER // LANE, tm, LANE), F32)]
    else:
        ys = ys.reshape(t, SSD_INNER)
        ys_spec = pl.BlockSpec((tm, SSD_INNER), lambda i: (i, 0))
        extra = []
    resident = lambda shape: pl.BlockSpec((None,) + shape, lambda i: (layer, 0, 0), pipeline_mode=pl.Buffered(1))
    return pl.pallas_call(
        functools.partial(_out_kernel, final=final, col_major=col_major),
        grid=(t // tm,),
        in_specs=[pl.BlockSpec((tm, D_MODEL), lambda i: (i, 0)),
                  pl.BlockSpec((tm, GLA_V), lambda i: (i, 0)),
                  pl.BlockSpec((1, 2, tm, LANE), lambda i: (i // tiles, 0, i % tiles, 0)),
                  ys_spec,
                  pl.BlockSpec((1, 6, D_MODEL), lambda i: (mod_row(i), 0, 0)),
                  pl.BlockSpec((1, D_MODEL), c2),
                  pl.BlockSpec((1, D_MODEL), c2),
                  resident((D_MODEL, D_MODEL)),
                  resident((D_MODEL, D_FF)),
                  resident((D_MODEL, D_FF)),
                  resident((D_FF, D_MODEL))],
        out_specs=pl.BlockSpec((tm, D_MODEL), lambda i: (i, 0)),
        out_shape=jax.ShapeDtypeStruct((t, D_MODEL), F32),
        scratch_shapes=extra,
        compiler_params=_cp("arbitrary"),
        name="out_ffn",
    )(x2, yg, yh, ys, mods, g2.reshape(1, D_MODEL), gf.reshape(1, D_MODEL), wo, w1, w3, w2)


W_IN_MOVES = ((0, 192, 0), (192, 384, 256), (384, 1184, 512), (1184, 3244, PG_W))
D_IN = 3244


def _pack_kernel(w_ref, o_ref):
    o_ref[...] = jnp.zeros(o_ref.shape, BF16)
    for src0, src1, dst in W_IN_MOVES:
        o_ref[0, :, dst:dst + src1 - src0] = w_ref[0, :, src0:src1].astype(BF16)


def _pack_w_in(w_in):
    rows = 128
    wtot = PG_W + PH_W + PS_W
    return pl.pallas_call(
        _pack_kernel,
        grid=(DEPTH, D_MODEL // rows),
        in_specs=[pl.BlockSpec((1, rows, D_IN), lambda l, i: (l, i, 0))],
        out_specs=pl.BlockSpec((1, rows, wtot), lambda l, i: (l, i, 0)),
        out_shape=jax.ShapeDtypeStruct((DEPTH, D_MODEL, wtot), BF16),
        compiler_params=_cp("arbitrary", "arbitrary"),
        name="pack_w_in",
    )(w_in)


def _pad_to(a, shape):
    return jnp.pad(a, [(0, s - d) for d, s in zip(a.shape, shape)])


def _mixers(pg, vg, x0, ps, lw, states):
    g_f0, g_b0, m_f0, m_b0 = states
    o_f, g_f = _gla_call(pg, None, lw['wgk_f'], lw['bgk_f'], None, g_f0, False)
    gla_y, g_b = _gla_call(pg, o_f, lw['wgk_b'], lw['bgk_b'], lw['gla_nw'], g_b0, True)

    L = vg.shape[2]
    h = _hy_filter_call(L, lw['hy_w1'], lw['hy_b1'], lw['hy_w2'], lw['hy_b2'], lw['hy_w3'], lw['hy_freq'],
                        lw['hy_decay'])
    if L == FFT_N // 2:
        m1, m1_real, f2, f2c, m3 = lw['fft']
        tok5 = lambda t: t.reshape(t.shape[0], 2, FFT_HALF, FFT_N1, LANE)
        hspec = _fft2_call(_fft1_call(tok5(h[None]), m1_real, 1), f2, f2c, None)
        vg5 = tok5(vg)
        bmat = _fft2_call(_fft1_call(vg5, m1, 2), f2, f2c, hspec)
        hy_y = _fft3_call(bmat, m3, vg5, tok5(x0), lw['hy_bias']).reshape(vg.shape)
    else:
        hy_y = _hy_direct_call(vg, x0, h, lw['hy_bias'])

    y_f, act, dta, m_f = _ssd_fwd_call(ps, lw['ssd_cw'], lw['ssd_cb'], lw['ssd_dtb'], lw['ssd_alog'], m_f0)
    ssd_y, m_b = _ssd_bwd_call(ps, act, dta, y_f, lw['ssd_dx'], lw['ssd_nw'], m_b0)
    return (gla_y, hy_y, ssd_y), (g_f, g_b, m_f, m_b)


def kernel(x, c, ctx, c_ctx, mod_w, mod_b, norm1_g, norm2_g, w_in, gla_gk_w_f, gla_gk_b_f, gla_gk_w_b, gla_gk_b_b, gla_norm_w, hy_short_w, hy_short_b, hy_w1, hy_b1, hy_w2, hy_b2, hy_w3, hy_freq, hy_decay, hy_bias, ssd_conv_w, ssd_conv_b, ssd_dt_bias_f, ssd_dt_bias_b, ssd_a_log_f, ssd_a_log_b, ssd_d, ssd_norm_w, w_out, ffn_w1, ffn_w3, ffn_w2, final_g):
    bsz, seq, _ = x.shape
    lc = ctx.shape[1]
    c8 = jnp.concatenate([c, c_ctx[None], jnp.zeros((8 - bsz - 1, D_MODEL), F32)], axis=0)
    mods_all = _mod_call(c8, mod_w, mod_b).reshape(DEPTH, 8, 6, D_MODEL)

    w_in_p = _pack_w_in(w_in)
    m1_c, f2_c, f2c_c, m3_c = _fft_consts()
    fft_mats = tuple(_bf(jnp.asarray(m)) for m in (m1_c, np.ascontiguousarray(m1_c[:, :, :FFT_HALF]), f2_c, f2c_c, m3_c))
    w_out_b, w1_b, w3_b, w2_b = _bf(w_out), _bf(ffn_w1), _bf(ffn_w3), _bf(ffn_w2)

    xt = x.reshape(bsz * seq, D_MODEL)
    ct = ctx.reshape(bsz * lc, D_MODEL)
    tiles_per_seq = seq // TOK_TILE
    row_x = lambda i: i // tiles_per_seq
    row_c = lambda i: bsz

    zeros_states = (jnp.zeros((bsz, GLA_DV, 256), F32), jnp.zeros((bsz, GLA_DV, 256), F32),
                    jnp.zeros((bsz, 2 * SSD_STATE, SSD_INNER), F32),
                    jnp.zeros((bsz, 2 * SSD_STATE, SSD_INNER), F32))

    for l in range(DEPTH):
        def gkw(w, off):
            return _bf(_pad_to(jnp.pad(w, ((off, 0), (0, 0))), (LANE, 256)))

        def bdiag(w):
            z = jnp.zeros_like(w)
            return jnp.concatenate([jnp.concatenate([w, z], axis=1), jnp.concatenate([z, w], axis=1)], axis=0)
        lw = {
            'wgk_f': gkw(gla_gk_w_f[l], 0), 'wgk_b': gkw(gla_gk_w_b[l], GLA_LOWRANK),
            'bgk_f': _pad_to(gla_gk_b_f[l][None], (1, 256)), 'bgk_b': _pad_to(gla_gk_b_b[l][None], (1, 256)),
            'gla_nw': jnp.tile(gla_norm_w[l], GLA_HEADS)[None],
            'hy_sw': _pad_to(hy_short_w[l], (8, PH_W)), 'hy_sb': hy_short_b[l][None],
            'hy_w1': bdiag(_pad_to(hy_w1[l], (HY_SLOT, HY_SLOT))), 'hy_b1': jnp.tile(hy_b1[l], 2)[None],
            'hy_w2': bdiag(hy_w2[l]), 'hy_b2': jnp.tile(hy_b2[l], 2)[None],
            'hy_w3': jnp.stack([jnp.pad(hy_w3[l], ((0, HY_SLOT), (0, 0))), jnp.pad(hy_w3[l], ((HY_SLOT, 0), (0, 0)))]),
            'hy_freq': jnp.tile(hy_freq[l], 2)[None],
            'hy_decay': hy_decay[l][None], 'hy_bias': hy_bias[l][None],
            'ssd_cw': _pad_to(ssd_conv_w[l], (8, SSD_CONV_DIM)), 'ssd_cb': ssd_conv_b[l][None],
            'ssd_dtb': _pad_to(jnp.concatenate([ssd_dt_bias_f[l], ssd_dt_bias_b[l]])[None], (1, LANE)),
            'ssd_alog': _pad_to(jnp.concatenate([ssd_a_log_f[l], ssd_a_log_b[l]])[None], (1, LANE)),
            'ssd_dx': jnp.repeat(ssd_d[l], SSD_P)[None], 'ssd_nw': ssd_norm_w[l][None],
        }
        lw['fft'] = fft_mats
        mods = mods_all[l]
        in_args = (norm1_g[l], w_in_p, l, lw['hy_sw'], lw['hy_sb'], bsz)
        yc, ctx_states = _mixers(*_in_call(ct, mods, row_c, *in_args, False), lw, zeros_states)
        yx, _ = _mixers(*_in_call(xt, mods, row_x, *in_args, True), lw, ctx_states)
        last = l == DEPTH - 1
        ffn = (w_out_b, w1_b, w3_b, w2_b, l)
        xt = _out_call(xt, yx[0].reshape(-1, GLA_V), yx[1], yx[2],
                       mods, row_x, norm2_g[l], final_g, *ffn, last, True)
        if not last:
            ct = _out_call(ct, yc[0].reshape(-1, GLA_V), yc[1], yc[2],
                           mods, row_c, norm2_g[l], final_g, *ffn, False, False)
    return xt.reshape(bsz, seq, D_MODEL)
```

```python
import functools
import math

import numpy as np
import jax
import jax.numpy as jnp
from jax import lax
from jax.experimental import pallas as pl
from jax.experimental.pallas import tpu as pltpu

F32 = jnp.float32
BF16 = jnp.bfloat16

D_MODEL = 1024
DEPTH = 2
GRID_W = 64
EPS = 1e-6
GLA_V = 384
GLA_DV = 64
GLA_HEADS = 6
GLA_DK = 32
GLA_QK = 192
GLA_LOWRANK = 16
GLA_TAU = 16.0
GLA_CHUNK = 64
GLA_BLOCK = 256
HY_WIDTH = 256
HY_BANDS = 16
HY_EMB = 1 + 2 * HY_BANDS
HY_ORDER = 64
SSD_INNER = 384
SSD_HEADS = 6
SSD_GROUPS = 2
SSD_HG = 3
SSD_P = 64
SSD_STATE = 128
SSD_CONV_DIM = SSD_INNER + 2 * SSD_GROUPS * SSD_STATE
SSD_CHUNK = 128
D_FF = 2816
FF_TILE = 1408
PG_W = 1408
PH_W = 768
PS_W = 1408
LANE = 128
FFT_N1 = 128
FFT_N = FFT_N1 * FFT_N1

VMEM_LIMIT = 56 * 1024 * 1024


def _cp(*sem):
    return pltpu.CompilerParams(dimension_semantics=sem, vmem_limit_bytes=VMEM_LIMIT)


def _bf(x):
    return x.astype(BF16)


def _dot(a, b):
    return jnp.dot(a, b, preferred_element_type=F32)


def _dot_nt(a, b):
    return lax.dot_general(a, b, (((1,), (1,)), ((), ())), preferred_element_type=F32)


def _dot_tn(a, b):
    return lax.dot_general(a, b, (((0,), (0,)), ((), ())), preferred_element_type=F32)


def _split3(x):
    hi = _bf(x)
    r1 = x - hi.astype(F32)
    mid = _bf(r1)
    lo = _bf(r1 - mid.astype(F32))
    return hi, mid, lo


def _dot_sel(sel_bf, x):
    hi, mid, lo = _split3(x)
    return _dot(sel_bf, hi) + _dot(sel_bf, mid) + _dot(sel_bf, lo)


def _dot_sel2(sel_bf, x):
    hi = _bf(x)
    lo = _bf(x - hi.astype(F32))
    return _dot(sel_bf, hi) + _dot(sel_bf, lo)


def _dot_hp(a, b):
    ah = _bf(a)
    al = _bf(a - ah.astype(F32))
    bh = _bf(b)
    bl = _bf(b - bh.astype(F32))
    return _dot(ah, bh) + _dot(ah, bl) + _dot(al, bh)


def _silu(x):
    return x * jax.nn.sigmoid(x)


def _softplus(x):
    return jnp.maximum(x, 0.0) + jnp.log(1.0 + jnp.exp(-jnp.abs(x)))


def _log_sigmoid(x):
    return jnp.minimum(x, 0.0) - jnp.log(1.0 + jnp.exp(-jnp.abs(x)))


def _mod_kernel(c_ref, w_ref, b_ref, o_ref):
    act = _silu(c_ref[...])
    o_ref[0] = _dot(_bf(act), _bf(w_ref[0])) + b_ref[0]


def _mod_call(c8, mod_w, mod_b):
    nt = 1536
    n = mod_w.shape[-1]
    return pl.pallas_call(
        _mod_kernel,
        grid=(DEPTH, n // nt),
        in_specs=[pl.BlockSpec((8, D_MODEL), lambda l, j: (0, 0)),
                  pl.BlockSpec((1, D_MODEL, nt), lambda l, j: (l, 0, j)),
                  pl.BlockSpec((1, 1, nt), lambda l, j: (l, 0, j))],
        out_specs=pl.BlockSpec((1, 8, nt), lambda l, j: (l, 0, j)),
        out_shape=jax.ShapeDtypeStruct((DEPTH, 8, n), F32),
        compiler_params=_cp("arbitrary", "arbitrary"),
        name="adaln_mod",
    )(c8, mod_w, mod_b.reshape(DEPTH, 1, n))


TOK_TILE = 512
ROW_GROUPS = 2
ROWS_PER_TILE = TOK_TILE // GRID_W


def _in_kernel(x_ref, xp_ref, xn_ref, mod_ref, g_ref, w_ref, hw_ref, hb_ref, og_ref, vg_ref, x0_ref, os_ref, *scr,
               col_major, tiles):
    m = mod_ref[0]
    i = pl.program_id(0)

    def modulated(t):
        ms = jnp.mean(t * t, axis=-1, keepdims=True)
        return _bf((t * lax.rsqrt(ms + EPS) * g_ref[...]) * (1.0 + m[1:2]) + m[0:1])

    tm = x_ref.shape[0]
    rgs = [slice(r * (tm // ROW_GROUPS), (r + 1) * (tm // ROW_GROUPS)) for r in range(ROW_GROUPS)]
    h = [modulated(x_ref[r, :]) for r in rgs]
    for r, t in zip(rgs, h):
        og_ref[r, :] = _bf(_dot(t, w_ref[:, 0:PG_W]))
    w_hy = w_ref[:, PG_W:PG_W + PH_W]
    ph = jnp.concatenate([_dot(t, w_hy) for t in h], axis=0)
    ps = jnp.concatenate([_dot(t, w_ref[:, PG_W + PH_W:]) for t in h], axis=0)

    halo = _dot(modulated(jnp.concatenate([xp_ref[...], xn_ref[...]], axis=0)), w_hy)
    prow = halo[7:8] * (i % tiles != 0).astype(F32)
    nrow = halo[8:9] * (i % tiles != tiles - 1).astype(F32)
    ridx = lax.broadcasted_iota(jnp.int32, (tm, 1), 0)
    pm = jnp.where(ridx == 0, prow, pltpu.roll(ph, 1, 0))
    pn = jnp.where(ridx == tm - 1, nrow, pltpu.roll(ph, tm - 1, 0))
    u = hw_ref[0:1] * pm + hw_ref[1:2] * ph + hw_ref[2:3] * pn + hb_ref[...]
    vg = u[:, 2 * HY_WIDTH:] * u[:, HY_WIDTH:2 * HY_WIDTH]
    for hf in range(HY_WIDTH // LANE):
        x0_ref[0, hf] = u[:, hf * LANE:(hf + 1) * LANE]
        vg_ref[0, hf] = vg[:, hf * LANE:(hf + 1) * LANE]

    if col_major:
        ps_scr, = scr
        for k in range(PS_W // LANE):
            for r in range(ROWS_PER_TILE):
                ps_scr[k, pl.ds(r, GRID_W, stride=ROWS_PER_TILE), :] = ps[r * GRID_W:(r + 1) * GRID_W,
                                                                         k * LANE:(k + 1) * LANE]
        for k in range(PS_W // LANE):
            os_ref[0, :, :, k * LANE:(k + 1) * LANE] = ps_scr[k].reshape(GRID_W, ROWS_PER_TILE, LANE)
    else:
        os_ref[...] = ps


def _in_call(x2, mods, mod_row, g, w_all, layer, hy_w8, hy_b, bsz, col_major):
    t = x2.shape[0]
    seq = t // bsz
    tm = min(TOK_TILE, seq)
    tiles = seq // tm
    wtot = PG_W + PH_W + PS_W
    hb = tm // 8
    if col_major:
        assert seq == GRID_W * SSD_CHUNK and ROWS_PER_TILE == 8 and tm == TOK_TILE
        os_spec = pl.BlockSpec((1, GRID_W, ROWS_PER_TILE, PS_W), lambda i: (i // tiles, 0, i % tiles, 0))
        os_shape = jax.ShapeDtypeStruct((bsz, GRID_W, SSD_CHUNK, PS_W), F32)
        scratch = [pltpu.VMEM((PS_W // LANE, tm, LANE), F32)]
    else:
        os_spec = pl.BlockSpec((tm, PS_W), lambda i: (i, 0))
        os_shape = jax.ShapeDtypeStruct((t, PS_W), F32)
        scratch = []
    nblk8 = t // 8
    tok_spec = pl.BlockSpec((1, 2, tm, LANE), lambda i: (i // tiles, 0, i % tiles, 0))
    tok_shape = jax.ShapeDtypeStruct((bsz, 2, seq, LANE), F32)
    pg, vg, x0, ps = pl.pallas_call(
        functools.partial(_in_kernel, col_major=col_major, tiles=tiles),
        grid=(t // tm,),
        in_specs=[pl.BlockSpec((tm, D_MODEL), lambda i: (i, 0)),
                  pl.BlockSpec((8, D_MODEL), lambda i: (jnp.maximum(i * hb - 1, 0), 0)),
                  pl.BlockSpec((8, D_MODEL), lambda i: (jnp.minimum((i + 1) * hb, nblk8 - 1), 0)),
                  pl.BlockSpec((1, 6, D_MODEL), lambda i: (mod_row(i), 0, 0)),
                  pl.BlockSpec((1, D_MODEL), lambda i: (0, 0)),
                  pl.BlockSpec((None, D_MODEL, wtot), lambda i: (layer, 0, 0), pipeline_mode=pl.Buffered(1)),
                  pl.BlockSpec((8, PH_W), lambda i: (0, 0)),
                  pl.BlockSpec((1, PH_W), lambda i: (0, 0))],
        out_specs=[pl.BlockSpec((tm, PG_W), lambda i: (i, 0)), tok_spec, tok_spec, os_spec],
        out_shape=[jax.ShapeDtypeStruct((t, PG_W), BF16), tok_shape, tok_shape, os_shape],
        scratch_shapes=scratch,
        compiler_params=_cp("arbitrary"),
        name="in_proj",
    )(x2, x2, x2, mods, g.reshape(1, D_MODEL), w_all, hy_w8, hy_b)
    return (pg.reshape(bsz, seq, PG_W), vg, x0, ps.reshape(bsz, seq // SSD_CHUNK, SSD_CHUNK, PS_W))


def _gla_kernel(*refs, reverse, nblk):
    if reverse:
        (p_ref, of_ref, wgk_ref, bgk_ref, nw_ref, s0_ref, y_ref, sf_ref, st_scr) = refs
    else:
        (p_ref, wgk_ref, bgk_ref, s0_ref, of_ref, sf_ref, st_scr) = refs
    i = pl.program_id(1)
    tb = GLA_BLOCK

    @pl.when(i == 0)
    def _():
        st_scr[...] = s0_ref[...]

    ri = lax.broadcasted_iota(jnp.int32, (tb, tb), 0)
    ci = lax.broadcasted_iota(jnp.int32, (tb, tb), 1)
    same = (ri // GLA_CHUNK) == (ci // GLA_CHUNK)
    tri = same & ((ci >= ri) if reverse else (ci <= ri))
    tri_bf = jnp.where(tri, 1.0, 0.0).astype(BF16)
    lane = lax.broadcasted_iota(jnp.int32, (1, 256), 1)
    hms = [(lane // GLA_DK) == h for h in range(GLA_HEADS)]

    nseq = p_ref.shape[0]
    seqs = range(nseq)
    ps = [p_ref[s_] for s_ in seqs]
    os_, sts = _gla_blocks(ps, [st_scr[s_] for s_ in seqs], wgk_ref[...], bgk_ref[...], tri, tri_bf, lane, hms,
                           reverse)
    for s_ in seqs:
        st_scr[s_] = sts[s_]

    @pl.when(i == nblk - 1)
    def _():
        for s_ in seqs:
            sf_ref[s_] = sts[s_]

    if not reverse:
        for s_ in seqs:
            of_ref[s_] = os_[s_]
    else:
        r2 = lax.broadcasted_iota(jnp.int32, (GLA_V, GLA_V), 0) // GLA_DV
        c2 = lax.broadcasted_iota(jnp.int32, (GLA_V, GLA_V), 1) // GLA_DV
        ind = jnp.where(r2 == c2, 1.0, 0.0).astype(BF16)
        ot = [of_ref[s_] + os_[s_] for s_ in seqs]
        sq = [t * t for t in ot]
        sh = [_bf(t) for t in sq]
        sl = [_bf(sq[s_] - sh[s_].astype(F32)) for s_ in seqs]
        ms = [(_dot(sh[s_], ind) + _dot(sl[s_], ind)) * (1.0 / GLA_DV) for s_ in seqs]
        for s_ in seqs:
            g = ps[s_][:, 896:1280].astype(F32)
            y_ref[s_] = ot[s_] * lax.rsqrt(ms[s_] + EPS) * nw_ref[...] * _silu(g)


def _gla_blocks(ps, sts, wgk, bgk, tri, tri_bf, lane, hms, reverse):
    tb = GLA_BLOCK
    nch = tb // GLA_CHUNK
    seqs = range(len(ps))
    k = [p[:, 256:512].astype(F32) for p in ps]
    pre = [_dot(p[:, 1280:1408], wgk) + bgk for p in ps]
    la = [_log_sigmoid(t) * (1.0 / GLA_TAU) for t in pre]
    b = [_dot_sel2(tri_bf, t) for t in la]
    qd = [_bf(ps[s][:, 0:256].astype(F32) * ((GLA_DK ** -0.5) * jnp.exp(b[s]))) for s in seqs]
    ki = [_bf(k[s] * jnp.exp(-b[s])) for s in seqs]
    vb = [p[:, 512:896] for p in ps]

    lhs = [jnp.concatenate([jnp.where(hm, t, jnp.zeros_like(t)) for hm in hms], axis=0) for t in qd]
    sc = [_dot_nt(lhs[s], ki[s]) for s in seqs]
    pm = [jnp.concatenate([_bf(jnp.where(tri, t[h * tb:(h + 1) * tb], 0.0)) for h in range(GLA_HEADS)], axis=0)
          for t in sc]
    ra = [_dot(pm[s][:4 * tb], vb[s][:, 0:256]) for s in seqs]
    rb = [_dot(pm[s][4 * tb:], vb[s][:, 256:384]) for s in seqs]
    hl = lane // GLA_DV
    o_intra = []
    for s in seqs:
        oa = jnp.zeros((tb, 256), F32)
        for h in range(4):
            oa = oa + jnp.where(hl == h, ra[s][h * tb:(h + 1) * tb], 0.0)
        ob = jnp.where(hl[:, :LANE] == 0, rb[s][:tb], rb[s][tb:])
        o_intra.append(jnp.concatenate([oa, ob], axis=1))

    sts = list(sts)
    outs = [[None] * nch for _ in seqs]
    order = range(nch - 1, -1, -1) if reverse else range(nch)
    for c in order:
        r0 = c * GLA_CHUNK
        rows = slice(r0, r0 + GLA_CHUNK)
        edge = r0 if reverse else r0 + GLA_CHUNK - 1
        bl = [t[edge:edge + 1] for t in b]
        kd = [_bf(k[s][rows] * jnp.exp(bl[s] - b[s][rows])) for s in seqs]
        stbd = [_bf(jnp.concatenate([jnp.where(hm, t, 0.0) for hm in hms], axis=0)) for t in sts]
        for s in seqs:
            outs[s][c] = _dot_nt(qd[s][rows], stbd[s])
        full = [_dot_tn(vb[s][rows], kd[s]) for s in seqs]
        for s in seqs:
            ds = jnp.zeros((GLA_DV, 256), F32)
            for h in range(GLA_HEADS):
                ds = ds + jnp.where(hms[h], full[s][h * GLA_DV:(h + 1) * GLA_DV], 0.0)
            sts[s] = jnp.exp(bl[s]) * sts[s] + ds
    return [o_intra[s] + jnp.concatenate(outs[s], axis=0) for s in seqs], sts


SEQ_PER_STEP = 4
GLA_SEQ_PER_STEP = 4


def _gla_call(pg, o_f, wgk, bgk, nw, s0, reverse):
    bsz, L, _ = pg.shape
    nblk = L // GLA_BLOCK
    ns = GLA_SEQ_PER_STEP
    blk = (lambda b, i: (b, nblk - 1 - i, 0)) if reverse else (lambda b, i: (b, i, 0))
    const2 = lambda b, i: (0, 0)
    st_spec = pl.BlockSpec((ns, GLA_DV, 256), lambda b, i: (b, 0, 0))
    p_spec = pl.BlockSpec((ns, GLA_BLOCK, PG_W), blk)
    o_spec = pl.BlockSpec((ns, GLA_BLOCK, GLA_V), blk)
    if reverse:
        in_specs = [p_spec, o_spec, pl.BlockSpec((LANE, 256), const2), pl.BlockSpec((1, 256), const2),
                    pl.BlockSpec((1, GLA_V), const2), st_spec]
        args = (pg, o_f, wgk, bgk, nw, s0)
    else:
        in_specs = [p_spec, pl.BlockSpec((LANE, 256), const2), pl.BlockSpec((1, 256), const2), st_spec]
        args = (pg, wgk, bgk, s0)
    return pl.pallas_call(
        functools.partial(_gla_kernel, reverse=reverse, nblk=nblk),
        grid=(bsz // ns, nblk),
        in_specs=in_specs,
        out_specs=[o_spec, st_spec],
        out_shape=[jax.ShapeDtypeStruct((bsz, L, GLA_V), F32),
                   jax.ShapeDtypeStruct((bsz, GLA_DV, 256), F32)],
        scratch_shapes=[pltpu.VMEM((ns, GLA_DV, 256), F32)],
        compiler_params=_cp("arbitrary", "arbitrary"),
        name="gla_bwd" if reverse else "gla_fwd",
    )(*args)


def _expand_heads(t, lo):
    r = t.shape[0]
    lane = lax.broadcasted_iota(jnp.int32, (1, LANE), 1)
    tiles = []
    for j in range(SSD_HEADS // 2):
        a = jnp.broadcast_to(t[:, lo + 2 * j:lo + 2 * j + 1], (r, LANE))
        b = jnp.broadcast_to(t[:, lo + 2 * j + 1:lo + 2 * j + 2], (r, LANE))
        tiles.append(jnp.where(lane < SSD_P, a, b))
    return jnp.concatenate(tiles, axis=1)


def _ssd_prep(cur, prow, nrow, cw, cb, dtb, alog):
    q_ = SSD_CHUNK
    x = cur[:, 384:1280]
    ridx = lax.broadcasted_iota(jnp.int32, (q_, 1), 0)
    xm = jnp.where(ridx == 0, prow, pltpu.roll(x, 1, 0))
    xp = jnp.where(ridx == q_ - 1, nrow, pltpu.roll(x, q_ - 1, 0))
    act = _silu(cw[0:1] * xm + cw[1:2] * x + cw[2:3] * xp + cb)
    dtt = _softplus(cur[:, 1280:1408] + dtb)
    a = -jnp.exp(alog) * dtt
    return act, dtt, a


def _ssd_scans(xs, bmb, cmb, dtt, a, sts, tri, tri_bf, reverse):
    q_ = SSD_CHUNK
    lo = SSD_HEADS if reverse else 0
    seqs = range(len(xs))
    cs = [_dot_sel(tri_bf, t) for t in a]
    cst = [t.T for t in cs]
    dtT = [t.T for t in dtt]
    edge = 0 if reverse else q_ - 1
    cs_last = [t[edge:edge + 1] for t in cs]
    grp = lambda t, g: t[:, g * SSD_STATE:(g + 1) * SSD_STATE]
    cbs = [[_dot_nt(grp(cmb[s], g), grp(bmb[s], g)) for g in range(SSD_GROUPS)] for s in seqs]
    ms = [[] for _ in seqs]
    for h in range(SSD_HEADS):
        l = lo + h
        for s in seqs:
            seg = cs[s][:, l:l + 1] - cst[s][l:l + 1, :]
            dec = jnp.exp(jnp.where(tri, seg, -jnp.inf))
            ms[s].append(_bf(cbs[s][h // SSD_HG] * dec * dtT[s][l:l + 1, :]))
    mst = [jnp.concatenate(t, axis=0) for t in ms]
    xsb = [_bf(t) for t in xs]
    ra = [_dot(mst[s][:4 * q_], xsb[s][:, 0:256]) for s in seqs]
    rb = [_dot(mst[s][4 * q_:], xsb[s][:, 256:384]) for s in seqs]
    lane = lax.broadcasted_iota(jnp.int32, (1, 256), 1)
    hl = lane // SSD_P
    cs_x = [_expand_heads(t, lo) for t in cs]
    csl_x = [_expand_heads(t, lo) for t in cs_last]
    dt_x = [_expand_heads(t, lo) for t in dtt]
    ystate = [_dot(cmb[s], _bf(sts[s])) for s in seqs]
    xw = [_bf(xs[s] * (jnp.exp(csl_x[s] - cs_x[s]) * dt_x[s])) for s in seqs]
    full = [_dot_tn(bmb[s], xw[s]) for s in seqs]
    r2 = lax.broadcasted_iota(jnp.int32, (2 * SSD_STATE, SSD_INNER), 0) // SSD_STATE
    c2 = lax.broadcasted_iota(jnp.int32, (2 * SSD_STATE, SSD_INNER), 1) // (SSD_HG * SSD_P)
    ys, new_sts = [], []
    for s in seqs:
        ya = jnp.zeros((q_, 256), F32)
        for h in range(4):
            ya = ya + jnp.where(hl == h, ra[s][h * q_:(h + 1) * q_], 0.0)
        yb = jnp.where(hl[:, :LANE] == 0, rb[s][:q_], rb[s][q_:])
        ys.append(jnp.concatenate([ya, yb], axis=1) + jnp.exp(cs_x[s]) * ystate[s])
        new_sts.append(jnp.exp(csl_x[s]) * sts[s] + jnp.where(r2 == c2, full[s], 0.0))
    return ys, new_sts


def _ssd_kernel(*refs, reverse, nchunk):
    if reverse:
        (z_ref, act_ref, dta_ref, yf_ref, dx_ref, nw_ref, s0_ref, y_ref, sf_ref, st_scr) = refs
    else:
        (cur_ref, prev_ref, next_ref, cw_ref, cb_ref, dtb_ref, alog_ref, s0_ref,
         yf_ref, act_ref, dta_ref, sf_ref, st_scr) = refs
    i = pl.program_id(1)
    c = (nchunk - 1 - i) if reverse else i
    q_ = SSD_CHUNK

    @pl.when(i == 0)
    def _():
        st_scr[...] = s0_ref[...]

    ri = lax.broadcasted_iota(jnp.int32, (q_, q_), 0)
    ci = lax.broadcasted_iota(jnp.int32, (q_, q_), 1)
    tri = (ci >= ri) if reverse else (ci <= ri)
    tri_bf = jnp.where(tri, 1.0, 0.0).astype(BF16)

    seqs = range(st_scr.shape[0])
    if reverse:
        actb = [act_ref[s_, 0] for s_ in seqs]
        xs = [t[:, 0:384].astype(F32) for t in actb]
        dtt = [dta_ref[s_, 0][:, :LANE] for s_ in seqs]
        a = [dta_ref[s_, 0][:, LANE:] for s_ in seqs]
    else:
        has_prev = (c > 0).astype(F32)
        has_next = (c < nchunk - 1).astype(F32)
        actb, xs, dtt, a = [], [], [], []
        for s_ in seqs:
            prow = prev_ref[s_, 0][7:8, 384:1280] * has_prev
            nrow = next_ref[s_, 0][0:1, 384:1280] * has_next
            act, dt_, a_ = _ssd_prep(cur_ref[s_, 0], prow, nrow, cw_ref, cb_ref[...], dtb_ref[...], alog_ref[...])
            actb.append(_bf(act))
            act_ref[s_, 0] = actb[s_]
            dta_ref[s_, 0] = jnp.concatenate([dt_, a_], axis=1)
            xs.append(act[:, 0:384])
            dtt.append(dt_)
            a.append(a_)
    ys, sts = _ssd_scans(xs, [t[:, 384:640] for t in actb], [t[:, 640:896] for t in actb], dtt, a,
                         [st_scr[s_] for s_ in seqs], tri, tri_bf, reverse)
    for s_ in seqs:
        st_scr[s_] = sts[s_]

    @pl.when(i == nchunk - 1)
    def _():
        for s_ in seqs:
            sf_ref[s_] = sts[s_]

    for s_ in seqs:
        if not reverse:
            yf_ref[s_, 0] = ys[s_]
        else:
            yt = yf_ref[s_, 0] + ys[s_] + dx_ref[...] * xs[s_]
            yz = yt * _silu(z_ref[s_, 0])
            l384 = lax.broadcasted_iota(jnp.int32, (1, SSD_INNER), 1)
            g0 = l384 < (SSD_INNER // SSD_GROUPS)
            sq = yz * yz
            m0 = jnp.sum(jnp.where(g0, sq, 0.0), axis=-1, keepdims=True)
            m1 = jnp.sum(jnp.where(g0, 0.0, sq), axis=-1, keepdims=True)
            msq = jnp.where(g0, m0, m1) * (1.0 / (SSD_INNER // SSD_GROUPS))
            y_ref[s_, 0] = yz * lax.rsqrt(msq + EPS) * nw_ref[...]


def _ssd_fwd_call(ps, cw, cb, dtb, alog, s0):
    bsz, nchunk, _, _ = ps.shape
    ns = SEQ_PER_STEP
    cur_map = lambda b, i: (b, i, 0, 0)
    prev_map = lambda b, i: (b, jnp.maximum(i - 1, 0), SSD_CHUNK // 8 - 1, 0)
    next_map = lambda b, i: (b, jnp.minimum(i + 1, nchunk - 1), 0, 0)
    const2 = lambda b, i: (0, 0)
    chunk = lambda w: pl.BlockSpec((ns, 1, SSD_CHUNK, w), cur_map)
    st_spec = pl.BlockSpec((ns, 2 * SSD_STATE, SSD_INNER), lambda b, i: (b, 0, 0))
    return pl.pallas_call(
        functools.partial(_ssd_kernel, reverse=False, nchunk=nchunk),
        grid=(bsz // ns, nchunk),
        in_specs=[chunk(PS_W), pl.BlockSpec((ns, 1, 8, PS_W), prev_map), pl.BlockSpec((ns, 1, 8, PS_W), next_map),
                  pl.BlockSpec((8, SSD_CONV_DIM), const2), pl.BlockSpec((1, SSD_CONV_DIM), const2),
                  pl.BlockSpec((1, LANE), const2), pl.BlockSpec((1, LANE), const2), st_spec],
        out_specs=[chunk(SSD_INNER), chunk(SSD_CONV_DIM), chunk(2 * LANE), st_spec],
        out_shape=[jax.ShapeDtypeStruct((bsz, nchunk, SSD_CHUNK, SSD_INNER), F32),
                   jax.ShapeDtypeStruct((bsz, nchunk, SSD_CHUNK, SSD_CONV_DIM), BF16),
                   jax.ShapeDtypeStruct((bsz, nchunk, SSD_CHUNK, 2 * LANE), F32),
                   jax.ShapeDtypeStruct((bsz, 2 * SSD_STATE, SSD_INNER), F32)],
        scratch_shapes=[pltpu.VMEM((ns, 2 * SSD_STATE, SSD_INNER), F32)],
        compiler_params=_cp("arbitrary", "arbitrary"),
        name="ssd_fwd",
    )(ps, ps, ps, cw, cb, dtb, alog, s0)


def _ssd_bwd_call(ps, act, dta, y_f, dx, nw, s0):
    bsz, nchunk, _, _ = ps.shape
    ns = SEQ_PER_STEP
    cur_map = lambda b, i: (b, nchunk - 1 - i, 0, 0)
    const2 = lambda b, i: (0, 0)
    chunk = lambda w: pl.BlockSpec((ns, 1, SSD_CHUNK, w), cur_map)
    st_spec = pl.BlockSpec((ns, 2 * SSD_STATE, SSD_INNER), lambda b, i: (b, 0, 0))
    return pl.pallas_call(
        functools.partial(_ssd_kernel, reverse=True, nchunk=nchunk),
        grid=(bsz // ns, nchunk),
        in_specs=[chunk(SSD_INNER), chunk(SSD_CONV_DIM), chunk(2 * LANE), chunk(SSD_INNER),
                  pl.BlockSpec((1, SSD_INNER), const2), pl.BlockSpec((1, SSD_INNER), const2), st_spec],
        out_specs=[chunk(SSD_INNER), st_spec],
        out_shape=[jax.ShapeDtypeStruct((bsz, nchunk, SSD_CHUNK, SSD_INNER), F32),
                   jax.ShapeDtypeStruct((bsz, 2 * SSD_STATE, SSD_INNER), F32)],
        scratch_shapes=[pltpu.VMEM((ns, 2 * SSD_STATE, SSD_INNER), F32)],
        compiler_params=_cp("arbitrary", "arbitrary"),
        name="ssd_bwd",
    )(ps, act, dta, y_f, dx, nw, s0)


HY_SLOT = LANE // 2


def _hy_filter_kernel(z_ref, w1_ref, b1_ref, w2_ref, b2_ref, w3_ref, fr_ref, dec_ref, h_ref):
    z = z_ref[...]
    half = z.shape[0]
    fr = fr_ref[...]
    h1 = jnp.sin(fr * (_dot_hp(z, w1_ref[...]) + b1_ref[...]))
    h2 = jnp.sin(fr * (_dot_hp(h1, w2_ref[...]) + b2_ref[...]))
    for slot in range(2):
        h = _dot_hp(h2, w3_ref[slot])
        rel = z[:, slot * HY_SLOT:slot * HY_SLOT + 1]
        h = h * jnp.exp(-2.0 * jnp.abs(rel) * dec_ref[...])
        for hf in range(HY_WIDTH // LANE):
            h_ref[hf, slot * half:(slot + 1) * half, :] = h[:, hf * LANE:(hf + 1) * LANE]


def _hy_features(L, tl):
    t = jnp.arange(L, dtype=F32)
    rel = (t - (L // 2)) / L
    bands = jnp.linspace(1e-4, HY_BANDS - 1, HY_BANDS, dtype=F32)
    ang = 2.0 * math.pi * rel[:, None] * bands
    z = jnp.concatenate([rel[:, None], jnp.cos(ang), -jnp.sin(ang)], axis=-1)
    z = jnp.pad(z, ((0, 0), (0, HY_SLOT - HY_EMB)))
    return z.reshape(L // tl, 2, tl // 2, HY_SLOT).transpose(0, 2, 1, 3).reshape(L // 2, LANE)


def _hy_filter_call(L, w1bd, b1t, w2bd, b2t, w3s, frt, dec):
    tl = min(1024, L)
    z = _hy_features(L, tl)
    c2 = lambda i: (0, 0)
    return pl.pallas_call(
        _hy_filter_kernel,
        grid=(L // tl,),
        in_specs=[pl.BlockSpec((tl // 2, LANE), lambda i: (i, 0)),
                  pl.BlockSpec((LANE, LANE), c2), pl.BlockSpec((1, LANE), c2),
                  pl.BlockSpec((LANE, LANE), c2), pl.BlockSpec((1, LANE), c2),
                  pl.BlockSpec((2, LANE, HY_WIDTH), lambda i: (0, 0, 0)), pl.BlockSpec((1, LANE), c2),
                  pl.BlockSpec((1, HY_WIDTH), c2)],
        out_specs=pl.BlockSpec((2, tl, LANE), lambda i: (0, i, 0)),
        out_shape=jax.ShapeDtypeStruct((2, L, LANE), F32),
        compiler_params=_cp("arbitrary"),
        name="hy_filter",
    )(z, w1bd, b1t, w2bd, b2t, w3s, frt, dec)


@functools.lru_cache(maxsize=None)
def _fft_consts():
    n1 = FFT_N1
    half = n1 // 2
    k = np.arange(n1, dtype=np.float64)
    n2 = k[:, None, None]
    k1 = k[None, :, None]
    nn = np.arange(half, dtype=np.float64)[None, None, :]
    ang = -2.0 * np.pi * (n2 * k1 / FFT_N + nn * k1 / n1)
    mr, mi = np.cos(ang), np.sin(ang)
    m1 = np.concatenate([np.concatenate([mr, -mi], axis=2), np.concatenate([mi, mr], axis=2)], axis=1)
    ang2 = -2.0 * np.pi * np.outer(k, k) / n1
    fr, fi = np.cos(ang2), np.sin(ang2)
    f2 = np.block([[fr, -fi], [fi, fr]])
    f2c = np.block([[fr, fi], [-fi, fr]])
    no = (np.arange(half, dtype=np.float64) + n1 // 4)[None, :, None]
    kk = k[None, None, :]
    ang3 = 2.0 * np.pi * (n2 * kk / FFT_N + no * kk / n1)
    ir, ii = np.cos(ang3) / FFT_N, np.sin(ang3) / FFT_N
    m3 = np.concatenate([np.concatenate([ir, -ii], axis=2), np.concatenate([ii, ir], axis=2)], axis=1)
    return tuple(np.asarray(m, dtype=np.float32) for m in (m1, f2, f2c, m3))


FFT_NB = 16
FFT_HALF = FFT_N1 // 2


def _strided_rows(ref2d, start, n):
    return ref2d[pl.ds(start, n, stride=FFT_NB), :]


def _tok_flat(ref):
    return ref.reshape(ref.shape[0] * 2 * FFT_HALF * FFT_NB, LANE)


def _tok_row_slice(s, hf, t):
    return pl.ds((2 * s + hf) * FFT_HALF * FFT_NB + t, FFT_HALF, stride=FFT_NB)


def _tok_rows(ref, s, t):
    flat = _tok_flat(ref)
    return jnp.concatenate([flat[_tok_row_slice(s, hf, t), :] for hf in range(2)], axis=1)


def _stage_f32(dst, src_ref):
    v = src_ref[0].astype(F32).reshape(FFT_N1 * FFT_NB, HY_WIDTH)
    dst[0] = v[:, :LANE]
    dst[1] = v[:, LANE:]


def _staged_rows(scr, t):
    return jnp.concatenate([scr[hf, pl.ds(t, FFT_N1, stride=FFT_NB), :] for hf in range(2)], axis=1)


def _fft1_kernel(u_ref, m_ref, a_ref, *, nsig):
    for t in range(FFT_NB):
        rhs = _bf(jnp.concatenate([_tok_rows(u_ref, s, t) for s in range(nsig)], axis=0))
        a_ref[0, t] = _bf(_dot(m_ref[t], rhs))


def _fft1_call(u, m1, nsig):
    npair = u.shape[0] // nsig
    return pl.pallas_call(
        functools.partial(_fft1_kernel, nsig=nsig),
        grid=(npair, FFT_N1 // FFT_NB),
        in_specs=[pl.BlockSpec((nsig, 2, FFT_HALF, FFT_NB, LANE), lambda p, j: (p, 0, 0, j, 0)),
                  pl.BlockSpec((FFT_NB, 2 * FFT_N1, FFT_HALF * nsig), lambda p, j: (j, 0, 0))],
        out_specs=pl.BlockSpec((1, FFT_NB, 2 * FFT_N1, HY_WIDTH), lambda p, j: (p, j, 0, 0)),
        out_shape=jax.ShapeDtypeStruct((npair, FFT_N1, 2 * FFT_N1, HY_WIDTH), BF16),
        compiler_params=_cp("arbitrary", "arbitrary"),
        name="hy_fft1",
    )(u, m1)


def _fft2_kernel(*refs, spectrum):
    if spectrum:
        ar_ref, ai_ref, f_ref, o_ref, sr, si = refs
    else:
        ar_ref, ai_ref, f_ref, fc_ref, h_ref, o_ref, sr, si = refs
    _stage_f32(sr, ar_ref)
    _stage_f32(si, ai_ref)
    for t in range(FFT_NB):
        rhs = _bf(jnp.concatenate([_staged_rows(sr, t), _staged_rows(si, t)], axis=0))
        x = _dot(f_ref[...], rhs)
        if spectrum:
            o_ref[t] = x
        else:
            xr, xi = x[:FFT_N1], x[FFT_N1:]
            hr, hi = h_ref[t, :FFT_N1], h_ref[t, FFT_N1:]
            y = jnp.concatenate([xr * hr - xi * hi, xr * hi + xi * hr], axis=0)
            o_ref[0, t] = _bf(_dot(fc_ref[...], _bf(y)))


def _fft2_call(a, f2, f2c, hspec):
    npair = a.shape[0]
    av = a
    nj = FFT_N1 // FFT_NB
    c2 = lambda p, j: (0, 0)
    in_specs = [pl.BlockSpec((1, FFT_N1, FFT_NB, HY_WIDTH), lambda p, j: (p, 0, j, 0)),
                pl.BlockSpec((1, FFT_N1, FFT_NB, HY_WIDTH), lambda p, j: (p, 0, nj + j, 0)),
                pl.BlockSpec((2 * FFT_N1, 2 * FFT_N1), c2)]
    staging = [pltpu.VMEM((2, FFT_N1 * FFT_NB, LANE), F32), pltpu.VMEM((2, FFT_N1 * FFT_NB, LANE), F32)]
    if hspec is None:
        return pl.pallas_call(
            functools.partial(_fft2_kernel, spectrum=True),
            grid=(1, nj),
            in_specs=in_specs,
            out_specs=pl.BlockSpec((FFT_NB, 2 * FFT_N1, HY_WIDTH), lambda p, j: (j, 0, 0)),
            out_shape=jax.ShapeDtypeStruct((FFT_N1, 2 * FFT_N1, HY_WIDTH), F32),
            scratch_shapes=staging,
            compiler_params=_cp("arbitrary", "arbitrary"),
            name="hy_fft2_spec",
        )(av, av, f2)
    in_specs += [pl.BlockSpec((2 * FFT_N1, 2 * FFT_N1), c2),
                 pl.BlockSpec((FFT_NB, 2 * FFT_N1, HY_WIDTH), lambda p, j: (j, 0, 0))]
    return pl.pallas_call(
        functools.partial(_fft2_kernel, spectrum=False),
        grid=(npair, nj),
        in_specs=in_specs,
        out_specs=pl.BlockSpec((1, FFT_NB, 2 * FFT_N1, HY_WIDTH), lambda p, j: (p, j, 0, 0)),
        out_shape=jax.ShapeDtypeStruct((npair, FFT_N1, 2 * FFT_N1, HY_WIDTH), BF16),
        scratch_shapes=staging,
        compiler_params=_cp("arbitrary", "arbitrary"),
        name="hy_fft2",
    )(av, av, f2, f2c, hspec)


def _fft3_kernel(br_ref, bi_ref, m_ref, vg_ref, x0_ref, bias_ref, y_ref, sr, si):
    _stage_f32(sr, br_ref)
    _stage_f32(si, bi_ref)
    for t in range(FFT_NB):
        rhs = _bf(jnp.concatenate([_staged_rows(sr, t), _staged_rows(si, t)], axis=0))
        out = _dot(m_ref[t], rhs)
        for s in range(2):
            conv = out[s * FFT_HALF:(s + 1) * FFT_HALF]
            y = (conv + _tok_rows(vg_ref, s, t) * bias_ref[...]) * _tok_rows(x0_ref, s, t)
            for hf in range(2):
                _tok_flat(y_ref)[_tok_row_slice(s, hf, t), :] = y[:, hf * LANE:(hf + 1) * LANE]


def _fft3_call(bmat, m3, vg, x0, bias):
    npair = bmat.shape[0]
    nj = FFT_N1 // FFT_NB
    tok_spec = pl.BlockSpec((2, 2, FFT_HALF, FFT_NB, LANE), lambda p, j: (p, 0, 0, j, 0))
    return pl.pallas_call(
        _fft3_kernel,
        grid=(npair, nj),
        in_specs=[pl.BlockSpec((1, FFT_N1, FFT_NB, HY_WIDTH), lambda p, j: (p, 0, j, 0)),
                  pl.BlockSpec((1, FFT_N1, FFT_NB, HY_WIDTH), lambda p, j: (p, 0, nj + j, 0)),
                  pl.BlockSpec((FFT_NB, FFT_N1, 2 * FFT_N1), lambda p, j: (j, 0, 0)),
                  tok_spec, tok_spec,
                  pl.BlockSpec((1, HY_WIDTH), lambda p, j: (0, 0))],
        out_specs=tok_spec,
        out_shape=jax.ShapeDtypeStruct(vg.shape, F32),
        scratch_shapes=[pltpu.VMEM((2, FFT_N1 * FFT_NB, LANE), F32), pltpu.VMEM((2, FFT_N1 * FFT_NB, LANE), F32)],
        compiler_params=_cp("arbitrary", "arbitrary"),
        name="hy_fft3",
    )(bmat, bmat, m3, vg, x0, bias)


def _hy_direct_kernel(vg_ref, x0_ref, h_ref, bias_ref, y_ref, pad_scr, sh_scr):
    L = vg_ref.shape[2]
    u = jnp.concatenate([vg_ref[0, 0], vg_ref[0, 1]], axis=1)
    pad_scr[...] = jnp.zeros_like(pad_scr)
    pad_scr[L:2 * L, :] = u
    top = L + L // 2
    acc = jnp.zeros((L, HY_WIDTH), F32)
    for r in range(8):
        sh_scr[...] = pad_scr[r:r + 3 * L - 8, :]
        a_lo = -(-(top - L + 1 - r) // 8)
        a_hi = (top - r) // 8

        def body(a, acc, r=r):
            m = top - (a * 8 + r)
            tap = jnp.concatenate([h_ref[0, pl.ds(m, 1), :], h_ref[1, pl.ds(m, 1), :]], axis=1)
            return acc + tap * sh_scr[pl.ds(pl.multiple_of(a * 8, 8), L), :]

        acc = lax.fori_loop(a_lo, a_hi + 1, body, acc, unroll=4)
    y = (acc + u * bias_ref[...]) * jnp.concatenate([x0_ref[0, 0], x0_ref[0, 1]], axis=1)
    y_ref[0, 0] = y[:, :LANE]
    y_ref[0, 1] = y[:, LANE:]


def _hy_direct_call(vg, x0, h, bias):
    bsz, _, L, _ = vg.shape
    tok = pl.BlockSpec((1, 2, L, LANE), lambda b: (b, 0, 0, 0))
    return pl.pallas_call(
        _hy_direct_kernel,
        grid=(bsz,),
        in_specs=[tok, tok, pl.BlockSpec((2, L, LANE), lambda b: (0, 0, 0)),
                  pl.BlockSpec((1, HY_WIDTH), lambda b: (0, 0))],
        out_specs=tok,
        out_shape=jax.ShapeDtypeStruct(vg.shape, F32),
        scratch_shapes=[pltpu.VMEM((3 * L, HY_WIDTH), F32), pltpu.VMEM((3 * L - 8, HY_WIDTH), F32)],
        compiler_params=_cp("arbitrary"),
        name="hy_direct",
    )(vg, x0, h, bias)


def _out_kernel(x_ref, yg_ref, yh_ref, ys_ref, mod_ref, g2_ref, gf_ref, wo_ref, w1_ref, w3_ref, w2_ref,
                o_ref, *scr, final, col_major):
    m = mod_ref[0]
    if col_major:
        ys_scr, = scr
        for c in range(GRID_W):
            for k in range(SSD_INNER // LANE):
                ys_scr[k, pl.ds(c, ROWS_PER_TILE, stride=GRID_W), :] = ys_ref[0, c, :, k * LANE:(k + 1) * LANE]
        ys = jnp.concatenate([ys_scr[k] for k in range(SSD_INNER // LANE)], axis=1)
    else:
        ys = ys_ref[...]
    yh = jnp.concatenate([yh_ref[0, 0], yh_ref[0, 1]], axis=1)
    tm = x_ref.shape[0]
    nrg = ROW_GROUPS if tm % (8 * ROW_GROUPS) == 0 else 1
    rgs = [slice(r * (tm // nrg), (r + 1) * (tm // nrg)) for r in range(nrg)]
    mix = [(_dot(_bf(yg_ref[r, :]), wo_ref[0:GLA_V])
            + _dot(_bf(yh[r]), wo_ref[GLA_V:GLA_V + HY_WIDTH])
            + _dot(_bf(ys[r]), wo_ref[GLA_V + HY_WIDTH:])) for r in rgs]
    x1 = [x_ref[r, :] + m[2:3] * mx for r, mx in zip(rgs, mix)]
    ms = [jnp.mean(t * t, axis=-1, keepdims=True) for t in x1]
    h = [t * lax.rsqrt(s + EPS) * g2_ref[...] for t, s in zip(x1, ms)]
    h = [_bf(t * (1.0 + m[4:5]) + m[3:4]) for t in h]
    ffn = [None] * nrg
    for j in range(D_FF // FF_TILE):
        cols = slice(j * FF_TILE, (j + 1) * FF_TILE)
        a = [_dot(t, w1_ref[:, cols]) for t in h]
        b = [_dot(t, w3_ref[:, cols]) for t in h]
        part = [_dot(_bf(_silu(u) * v), w2_ref[cols, :]) for u, v in zip(a, b)]
        ffn = [p if f is None else f + p for f, p in zip(ffn, part)]
    for r, t, f in zip(rgs, x1, ffn):
        x2 = t + m[5:6] * f
        if final:
            ms2 = jnp.mean(x2 * x2, axis=-1, keepdims=True)
            x2 = x2 * lax.rsqrt(ms2 + EPS) * gf_ref[...]
        o_ref[r, :] = x2


def _out_call(x2, yg, yh, ys, mods, mod_row, g2, gf, wo, w1, w3, w2, layer, final, col_major):
    t = x2.shape[0]
    seq = yh.shape[2]
    tm = min(TOK_TILE, seq)
    tiles = seq // tm
    c2 = lambda i: (0, 0)
    if col_major:
        assert tm == TOK_TILE
        ys_spec = pl.BlockSpec((1, GRID_W, ROWS_PER_TILE, SSD_INNER), lambda i: (i // tiles, 0, i % tiles, 0))
        extra = [pltpu.VMEM((SSD_INNER // LANE, tm, LANE), F32)]
    else:
        ys = ys.reshape(t, SSD_INNER)
        ys_spec = pl.BlockSpec((tm, SSD_INNER), lambda i: (i, 0))
        extra = []
    resident = lambda shape: pl.BlockSpec((None,) + shape, lambda i: (layer, 0, 0), pipeline_mode=pl.Buffered(1))
    return pl.pallas_call(
        functools.partial(_out_kernel, final=final, col_major=col_major),
        grid=(t // tm,),
        in_specs=[pl.BlockSpec((tm, D_MODEL), lambda i: (i, 0)),
                  pl.BlockSpec((tm, GLA_V), lambda i: (i, 0)),
                  pl.BlockSpec((1, 2, tm, LANE), lambda i: (i // tiles, 0, i % tiles, 0)),
                  ys_spec,
                  pl.BlockSpec((1, 6, D_MODEL), lambda i: (mod_row(i), 0, 0)),
                  pl.BlockSpec((1, D_MODEL), c2),
                  pl.BlockSpec((1, D_MODEL), c2),
                  resident((D_MODEL, D_MODEL)),
                  resident((D_MODEL, D_FF)),
                  resident((D_MODEL, D_FF)),
                  resident((D_FF, D_MODEL))],
        out_specs=pl.BlockSpec((tm, D_MODEL), lambda i: (i, 0)),
        out_shape=jax.ShapeDtypeStruct((t, D_MODEL), F32),
        scratch_shapes=extra,
        compiler_params=_cp("arbitrary"),
        name="out_ffn",
    )(x2, yg, yh, ys, mods, g2.reshape(1, D_MODEL), gf.reshape(1, D_MODEL), wo, w1, w3, w2)


W_IN_MOVES = ((0, 192, 0), (192, 384, 256), (384, 1184, 512), (1184, 3244, PG_W))
D_IN = 3244


def _pack_kernel(w_ref, o_ref):
    o_ref[...] = jnp.zeros(o_ref.shape, BF16)
    for src0, src1, dst in W_IN_MOVES:
        o_ref[0, :, dst:dst + src1 - src0] = w_ref[0, :, src0:src1].astype(BF16)


def _pack_w_in(w_in):
    rows = 128
    wtot = PG_W + PH_W + PS_W
    return pl.pallas_call(
        _pack_kernel,
        grid=(DEPTH, D_MODEL // rows),
        in_specs=[pl.BlockSpec((1, rows, D_IN), lambda l, i: (l, i, 0))],
        out_specs=pl.BlockSpec((1, rows, wtot), lambda l, i: (l, i, 0)),
        out_shape=jax.ShapeDtypeStruct((DEPTH, D_MODEL, wtot), BF16),
        compiler_params=_cp("arbitrary", "arbitrary"),
        name="pack_w_in",
    )(w_in)


def _pad_to(a, shape):
    return jnp.pad(a, [(0, s - d) for d, s in zip(a.shape, shape)])


def _mixers(pg, vg, x0, ps, lw, states):
    g_f0, g_b0, m_f0, m_b0 = states
    o_f, g_f = _gla_call(pg, None, lw['wgk_f'], lw['bgk_f'], None, g_f0, False)
    gla_y, g_b = _gla_call(pg, o_f, lw['wgk_b'], lw['bgk_b'], lw['gla_nw'], g_b0, True)

    L = vg.shape[2]
    h = _hy_filter_call(L, lw['hy_w1'], lw['hy_b1'], lw['hy_w2'], lw['hy_b2'], lw['hy_w3'], lw['hy_freq'],
                        lw['hy_decay'])
    if L == FFT_N // 2:
        m1, m1_real, f2, f2c, m3 = lw['fft']
        tok5 = lambda t: t.reshape(t.shape[0], 2, FFT_HALF, FFT_N1, LANE)
        hspec = _fft2_call(_fft1_call(tok5(h[None]), m1_real, 1), f2, f2c, None)
        vg5 = tok5(vg)
        bmat = _fft2_call(_fft1_call(vg5, m1, 2), f2, f2c, hspec)
        hy_y = _fft3_call(bmat, m3, vg5, tok5(x0), lw['hy_bias']).reshape(vg.shape)
    else:
        hy_y = _hy_direct_call(vg, x0, h, lw['hy_bias'])

    y_f, act, dta, m_f = _ssd_fwd_call(ps, lw['ssd_cw'], lw['ssd_cb'], lw['ssd_dtb'], lw['ssd_alog'], m_f0)
    ssd_y, m_b = _ssd_bwd_call(ps, act, dta, y_f, lw['ssd_dx'], lw['ssd_nw'], m_b0)
    return (gla_y, hy_y, ssd_y), (g_f, g_b, m_f, m_b)


def kernel(x, c, ctx, c_ctx, mod_w, mod_b, norm1_g, norm2_g, w_in, gla_gk_w_f, gla_gk_b_f, gla_gk_w_b, gla_gk_b_b, gla_norm_w, hy_short_w, hy_short_b, hy_w1, hy_b1, hy_w2, hy_b2, hy_w3, hy_freq, hy_decay, hy_bias, ssd_conv_w, ssd_conv_b, ssd_dt_bias_f, ssd_dt_bias_b, ssd_a_log_f, ssd_a_log_b, ssd_d, ssd_norm_w, w_out, ffn_w1, ffn_w3, ffn_w2, final_g):
    bsz, seq, _ = x.shape
    lc = ctx.shape[1]
    c8 = jnp.concatenate([c, c_ctx[None], jnp.zeros((8 - bsz - 1, D_MODEL), F32)], axis=0)
    mods_all = _mod_call(c8, mod_w, mod_b).reshape(DEPTH, 8, 6, D_MODEL)

    w_in_p = _pack_w_in(w_in)
    m1_c, f2_c, f2c_c, m3_c = _fft_consts()
    fft_mats = tuple(_bf(jnp.asarray(m)) for m in (m1_c, np.ascontiguousarray(m1_c[:, :, :FFT_HALF]), f2_c, f2c_c, m3_c))
    w_out_b, w1_b, w3_b, w2_b = _bf(w_out), _bf(ffn_w1), _bf(ffn_w3), _bf(ffn_w2)

    xt = x.reshape(bsz * seq, D_MODEL)
    ct = ctx.reshape(bsz * lc, D_MODEL)
    tiles_per_seq = seq // TOK_TILE
    row_x = lambda i: i // tiles_per_seq
    row_c = lambda i: bsz

    zeros_states = (jnp.zeros((bsz, GLA_DV, 256), F32), jnp.zeros((bsz, GLA_DV, 256), F32),
                    jnp.zeros((bsz, 2 * SSD_STATE, SSD_INNER), F32),
                    jnp.zeros((bsz, 2 * SSD_STATE, SSD_INNER), F32))

    for l in range(DEPTH):
        def gkw(w, off):
            return _bf(_pad_to(jnp.pad(w, ((off, 0), (0, 0))), (LANE, 256)))

        def bdiag(w):
            z = jnp.zeros_like(w)
            return jnp.concatenate([jnp.concatenate([w, z], axis=1), jnp.concatenate([z, w], axis=1)], axis=0)
        lw = {
            'wgk_f': gkw(gla_gk_w_f[l], 0), 'wgk_b': gkw(gla_gk_w_b[l], GLA_LOWRANK),
            'bgk_f': _pad_to(gla_gk_b_f[l][None], (1, 256)), 'bgk_b': _pad_to(gla_gk_b_b[l][None], (1, 256)),
            'gla_nw': jnp.tile(gla_norm_w[l], GLA_HEADS)[None],
            'hy_sw': _pad_to(hy_short_w[l], (8, PH_W)), 'hy_sb': hy_short_b[l][None],
            'hy_w1': bdiag(_pad_to(hy_w1[l], (HY_SLOT, HY_SLOT))), 'hy_b1': jnp.tile(hy_b1[l], 2)[None],
            'hy_w2': bdiag(hy_w2[l]), 'hy_b2': jnp.tile(hy_b2[l], 2)[None],
            'hy_w3': jnp.stack([jnp.pad(hy_w3[l], ((0, HY_SLOT), (0, 0))), jnp.pad(hy_w3[l], ((HY_SLOT, 0), (0, 0)))]),
            'hy_freq': jnp.tile(hy_freq[l], 2)[None],
            'hy_decay': hy_decay[l][None], 'hy_bias': hy_bias[l][None],
            'ssd_cw': _pad_to(ssd_conv_w[l], (8, SSD_CONV_DIM)), 'ssd_cb': ssd_conv_b[l][None],
            'ssd_dtb': _pad_to(jnp.concatenate([ssd_dt_bias_f[l], ssd_dt_bias_b[l]])[None], (1, LANE)),
            'ssd_alog': _pad_to(jnp.concatenate([ssd_a_log_f[l], ssd_a_log_b[l]])[None], (1, LANE)),
            'ssd_dx': jnp.repeat(ssd_d[l], SSD_P)[None], 'ssd_nw': ssd_norm_w[l][None],
        }
        lw['fft'] = fft_mats
        mods = mods_all[l]
        in_args = (norm1_g[l], w_in_p, l, lw['hy_sw'], lw['hy_sb'], bsz)
        yc, ctx_states = _mixers(*_in_call(ct, mods, row_c, *in_args, False), lw, zeros_states)
        yx, _ = _mixers(*_in_call(xt, mods, row_x, *in_args, True), lw, ctx_states)
        last = l == DEPTH - 1
        ffn = (w_out_b, w1_b, w3_b, w2_b, l)
        xt = _out_call(xt, yx[0].reshape(-1, GLA_V), yx[1], yx[2],
                       mods, row_x, norm2_g[l], final_g, *ffn, last, True)
        if not last:
            ct = _out_call(ct, yc[0].reshape(-1, GLA_V), yc[1], yc[2],
                           mods, row_c, norm2_g[l], final_g, *ffn, False, False)
    return xt.reshape(bsz, seq, D_MODEL)
```

```python
import functools
import math

import numpy as np
import jax
import jax.numpy as jnp
from jax import lax
from jax.experimental import pallas as pl
from jax.experimental.pallas import tpu as pltpu

F32 = jnp.float32
BF16 = jnp.bfloat16

D_MODEL = 1024
DEPTH = 2
GRID_W = 64
EPS = 1e-6
GLA_V = 384
GLA_DV = 64
GLA_HEADS = 6
GLA_DK = 32
GLA_QK = 192
GLA_LOWRANK = 16
GLA_TAU = 16.0
GLA_CHUNK = 64
GLA_BLOCK = 256
HY_WIDTH = 256
HY_BANDS = 16
HY_EMB = 1 + 2 * HY_BANDS
HY_ORDER = 64
SSD_INNER = 384
SSD_HEADS = 6
SSD_GROUPS = 2
SSD_HG = 3
SSD_P = 64
SSD_STATE = 128
SSD_CONV_DIM = SSD_INNER + 2 * SSD_GROUPS * SSD_STATE
SSD_CHUNK = 128
D_FF = 2816
FF_TILE = 1408
PG_W = 1408
PH_W = 768
PS_W = 1408
LANE = 128
FFT_N1 = 128
FFT_N = FFT_N1 * FFT_N1

VMEM_LIMIT = 56 * 1024 * 1024


def _cp(*sem):
    return pltpu.CompilerParams(dimension_semantics=sem, vmem_limit_bytes=VMEM_LIMIT)


def _bf(x):
    return x.astype(BF16)


def _lspec(shape, layer):
    zeros = (0,) * len(shape)
    return pl.BlockSpec((None,) + tuple(shape), lambda *_: (layer,) + zeros)


def _dot(a, b):
    return jnp.dot(a, b, preferred_element_type=F32)


def _dot_nt(a, b):
    return lax.dot_general(a, b, (((1,), (1,)), ((), ())), preferred_element_type=F32)


def _dot_tn(a, b):
    return lax.dot_general(a, b, (((0,), (0,)), ((), ())), preferred_element_type=F32)


def _split3(x):
    hi = _bf(x)
    r1 = x - hi.astype(F32)
    mid = _bf(r1)
    lo = _bf(r1 - mid.astype(F32))
    return hi, mid, lo


def _dot_sel(sel_bf, x):
    hi, mid, lo = _split3(x)
    return _dot(sel_bf, hi) + _dot(sel_bf, mid) + _dot(sel_bf, lo)


def _dot_sel2(sel_bf, x):
    hi = _bf(x)
    lo = _bf(x - hi.astype(F32))
    return _dot(sel_bf, hi) + _dot(sel_bf, lo)


def _dot_hp(a, b):
    ah = _bf(a)
    al = _bf(a - ah.astype(F32))
    bh = _bf(b)
    bl = _bf(b - bh.astype(F32))
    return _dot(ah, bh) + _dot(ah, bl) + _dot(al, bh)


def _silu(x):
    return x * jax.nn.sigmoid(x)


def _softplus(x):
    return jnp.maximum(x, 0.0) + jnp.log(1.0 + jnp.exp(-jnp.abs(x)))


def _log_sigmoid(x):
    return jnp.minimum(x, 0.0) - jnp.log(1.0 + jnp.exp(-jnp.abs(x)))


def _mod_kernel(c_ref, w_ref, b_ref, o_ref):
    act = _silu(c_ref[...])
    o_ref[0] = _dot(_bf(act), _bf(w_ref[0])) + b_ref[0]


def _mod_call(c8, mod_w, mod_b):
    nt = 1536
    n = mod_w.shape[-1]
    return pl.pallas_call(
        _mod_kernel,
        grid=(DEPTH, n // nt),
        in_specs=[pl.BlockSpec((8, D_MODEL), lambda l, j: (0, 0)),
                  pl.BlockSpec((1, D_MODEL, nt), lambda l, j: (l, 0, j)),
                  pl.BlockSpec((1, 1, nt), lambda l, j: (l, 0, j))],
        out_specs=pl.BlockSpec((1, 8, nt), lambda l, j: (l, 0, j)),
        out_shape=jax.ShapeDtypeStruct((DEPTH, 8, n), F32),
        compiler_params=_cp("arbitrary", "arbitrary"),
        name="adaln_mod",
    )(c8, mod_w, mod_b.reshape(DEPTH, 1, n))


TOK_TILE = 512
ROW_GROUPS = 2
ROWS_PER_TILE = TOK_TILE // GRID_W


def _in_kernel(x_ref, xp_ref, xn_ref, mod_ref, g_ref, w_ref, hw_ref, hb_ref, og_ref, vg_ref, x0_ref, os_ref, *scr,
               col_major, tiles):
    m = mod_ref[0]
    i = pl.program_id(0)

    def modulated(t):
        ms = jnp.mean(t * t, axis=-1, keepdims=True)
        return _bf((t * lax.rsqrt(ms + EPS) * g_ref[...]) * (1.0 + m[1:2]) + m[0:1])

    tm = x_ref.shape[0]
    rgs = [slice(r * (tm // ROW_GROUPS), (r + 1) * (tm // ROW_GROUPS)) for r in range(ROW_GROUPS)]
    h = [modulated(x_ref[r, :]) for r in rgs]
    for r, t in zip(rgs, h):
        og_ref[r, :] = _bf(_dot(t, w_ref[:, 0:PG_W]))
    w_hy = w_ref[:, PG_W:PG_W + PH_W]
    ph = jnp.concatenate([_dot(t, w_hy) for t in h], axis=0)
    ps = jnp.concatenate([_dot(t, w_ref[:, PG_W + PH_W:]) for t in h], axis=0)

    halo = _dot(modulated(jnp.concatenate([xp_ref[...], xn_ref[...]], axis=0)), w_hy)
    prow = halo[7:8] * (i % tiles != 0).astype(F32)
    nrow = halo[8:9] * (i % tiles != tiles - 1).astype(F32)
    ridx = lax.broadcasted_iota(jnp.int32, (tm, 1), 0)
    pm = jnp.where(ridx == 0, prow, pltpu.roll(ph, 1, 0))
    pn = jnp.where(ridx == tm - 1, nrow, pltpu.roll(ph, tm - 1, 0))
    u = hw_ref[0:1] * pm + hw_ref[1:2] * ph + hw_ref[2:3] * pn + hb_ref[...]
    vg = u[:, 2 * HY_WIDTH:] * u[:, HY_WIDTH:2 * HY_WIDTH]
    for hf in range(HY_WIDTH // LANE):
        x0_ref[0, hf] = u[:, hf * LANE:(hf + 1) * LANE]
        vg_ref[0, hf] = vg[:, hf * LANE:(hf + 1) * LANE]

    if col_major:
        ps_scr, = scr
        for k in range(PS_W // LANE):
            for r in range(ROWS_PER_TILE):
                ps_scr[k, pl.ds(r, GRID_W, stride=ROWS_PER_TILE), :] = ps[r * GRID_W:(r + 1) * GRID_W,
                                                                         k * LANE:(k + 1) * LANE]
        for k in range(PS_W // LANE):
            os_ref[0, :, :, k * LANE:(k + 1) * LANE] = ps_scr[k].reshape(GRID_W, ROWS_PER_TILE, LANE)
    else:
        os_ref[...] = ps


def _in_call(x2, mods, mod_row, g, w_all, layer, hy_w8, hy_b, bsz, col_major):
    t = x2.shape[0]
    seq = t // bsz
    tm = min(TOK_TILE, seq)
    tiles = seq // tm
    wtot = PG_W + PH_W + PS_W
    hb = tm // 8
    if col_major:
        assert seq == GRID_W * SSD_CHUNK and ROWS_PER_TILE == 8 and tm == TOK_TILE
        os_spec = pl.BlockSpec((1, GRID_W, ROWS_PER_TILE, PS_W), lambda i: (i // tiles, 0, i % tiles, 0))
        os_shape = jax.ShapeDtypeStruct((bsz, GRID_W, SSD_CHUNK, PS_W), F32)
        scratch = [pltpu.VMEM((PS_W // LANE, tm, LANE), F32)]
    else:
        os_spec = pl.BlockSpec((tm, PS_W), lambda i: (i, 0))
        os_shape = jax.ShapeDtypeStruct((t, PS_W), F32)
        scratch = []
    nblk8 = t // 8
    tok_spec = pl.BlockSpec((1, 2, tm, LANE), lambda i: (i // tiles, 0, i % tiles, 0))
    tok_shape = jax.ShapeDtypeStruct((bsz, 2, seq, LANE), F32)
    pg, vg, x0, ps = pl.pallas_call(
        functools.partial(_in_kernel, col_major=col_major, tiles=tiles),
        grid=(t // tm,),
        in_specs=[pl.BlockSpec((tm, D_MODEL), lambda i: (i, 0)),
                  pl.BlockSpec((8, D_MODEL), lambda i: (jnp.maximum(i * hb - 1, 0), 0)),
                  pl.BlockSpec((8, D_MODEL), lambda i: (jnp.minimum((i + 1) * hb, nblk8 - 1), 0)),
                  pl.BlockSpec((None, 1, 6, D_MODEL), lambda i: (layer, mod_row(i), 0, 0)),
                  _lspec((1, D_MODEL), layer),
                  pl.BlockSpec((None, D_MODEL, wtot), lambda i: (layer, 0, 0), pipeline_mode=pl.Buffered(1)),
                  _lspec((8, PH_W), layer),
                  _lspec((1, PH_W), layer)],
        out_specs=[pl.BlockSpec((tm, PG_W), lambda i: (i, 0)), tok_spec, tok_spec, os_spec],
        out_shape=[jax.ShapeDtypeStruct((t, PG_W), BF16), tok_shape, tok_shape, os_shape],
        scratch_shapes=scratch,
        compiler_params=_cp("arbitrary"),
        name="in_proj",
    )(x2, x2, x2, mods, g, w_all, hy_w8, hy_b)
    return (pg.reshape(bsz, seq, PG_W), vg, x0, ps.reshape(bsz, seq // SSD_CHUNK, SSD_CHUNK, PS_W))


def _gla_kernel(*refs, reverse, nblk):
    if reverse:
        (p_ref, of_ref, wgk_ref, bgk_ref, nw_ref, s0_ref, y_ref, sf_ref, st_scr) = refs
    else:
        (p_ref, wgk_ref, bgk_ref, s0_ref, of_ref, sf_ref, st_scr) = refs
    i = pl.program_id(1)
    tb = GLA_BLOCK

    @pl.when(i == 0)
    def _():
        st_scr[...] = s0_ref[...]

    ri = lax.broadcasted_iota(jnp.int32, (tb, tb), 0)
    ci = lax.broadcasted_iota(jnp.int32, (tb, tb), 1)
    same = (ri // GLA_CHUNK) == (ci // GLA_CHUNK)
    tri = same & ((ci >= ri) if reverse else (ci <= ri))
    tri_bf = jnp.where(tri, 1.0, 0.0).astype(BF16)
    lane = lax.broadcasted_iota(jnp.int32, (1, 256), 1)
    hms = [(lane // GLA_DK) == h for h in range(GLA_HEADS)]

    nseq = p_ref.shape[0]
    seqs = range(nseq)
    ps = [p_ref[s_] for s_ in seqs]
    os_, sts = _gla_blocks(ps, [st_scr[s_] for s_ in seqs], wgk_ref[...], bgk_ref[...], tri, tri_bf, lane, hms,
                           reverse)
    for s_ in seqs:
        st_scr[s_] = sts[s_]

    @pl.when(i == nblk - 1)
    def _():
        for s_ in seqs:
            sf_ref[s_] = sts[s_]

    if not reverse:
        for s_ in seqs:
            of_ref[s_] = os_[s_]
    else:
        r2 = lax.broadcasted_iota(jnp.int32, (GLA_V, GLA_V), 0) // GLA_DV
        c2 = lax.broadcasted_iota(jnp.int32, (GLA_V, GLA_V), 1) // GLA_DV
        ind = jnp.where(r2 == c2, 1.0, 0.0).astype(BF16)
        ot = [of_ref[s_] + os_[s_] for s_ in seqs]
        ms = [_dot(_bf(t * t), ind) * (1.0 / GLA_DV) for t in ot]
        for s_ in seqs:
            g = ps[s_][:, 896:1280].astype(F32)
            y_ref[s_] = ot[s_] * lax.rsqrt(ms[s_] + EPS) * nw_ref[...] * _silu(g)


def _gla_blocks(ps, sts, wgk, bgk, tri, tri_bf, lane, hms, reverse):
    tb = GLA_BLOCK
    nch = tb // GLA_CHUNK
    seqs = range(len(ps))
    k = [p[:, 256:512].astype(F32) for p in ps]
    pre = [_dot(p[:, 1280:1408], wgk) + bgk for p in ps]
    la = [_log_sigmoid(t) * (1.0 / GLA_TAU) for t in pre]
    b = [_dot_sel2(tri_bf, t) for t in la]
    qd = [_bf(ps[s][:, 0:256].astype(F32) * ((GLA_DK ** -0.5) * jnp.exp(b[s]))) for s in seqs]
    ki = [_bf(k[s] * jnp.exp(-b[s])) for s in seqs]
    vb = [p[:, 512:896] for p in ps]

    lhs = [jnp.concatenate([jnp.where(hm, t, jnp.zeros_like(t)) for hm in hms], axis=0) for t in qd]
    sc = [_dot_nt(lhs[s], ki[s]) for s in seqs]
    pm = [jnp.concatenate([_bf(jnp.where(tri, t[h * tb:(h + 1) * tb], 0.0)) for h in range(GLA_HEADS)], axis=0)
          for t in sc]
    ra = [_dot(pm[s][:4 * tb], vb[s][:, 0:256]) for s in seqs]
    rb = [_dot(pm[s][4 * tb:], vb[s][:, 256:384]) for s in seqs]
    hl = lane // GLA_DV
    o_intra = []
    for s in seqs:
        oa = jnp.zeros((tb, 256), F32)
        for h in range(4):
            oa = oa + jnp.where(hl == h, ra[s][h * tb:(h + 1) * tb], 0.0)
        ob = jnp.where(hl[:, :LANE] == 0, rb[s][:tb], rb[s][tb:])
        o_intra.append(jnp.concatenate([oa, ob], axis=1))

    sts = list(sts)
    outs = [[None] * nch for _ in seqs]
    order = range(nch - 1, -1, -1) if reverse else range(nch)
    for c in order:
        r0 = c * GLA_CHUNK
        rows = slice(r0, r0 + GLA_CHUNK)
        edge = r0 if reverse else r0 + GLA_CHUNK - 1
        bl = [t[edge:edge + 1] for t in b]
        kd = [_bf(k[s][rows] * jnp.exp(bl[s] - b[s][rows])) for s in seqs]
        stbd = [_bf(jnp.concatenate([jnp.where(hm, t, 0.0) for hm in hms], axis=0)) for t in sts]
        for s in seqs:
            outs[s][c] = _dot_nt(qd[s][rows], stbd[s])
        full = [_dot_tn(vb[s][rows], kd[s]) for s in seqs]
        for s in seqs:
            ds = jnp.zeros((GLA_DV, 256), F32)
            for h in range(GLA_HEADS):
                ds = ds + jnp.where(hms[h], full[s][h * GLA_DV:(h + 1) * GLA_DV], 0.0)
            sts[s] = jnp.exp(bl[s]) * sts[s] + ds
    return [o_intra[s] + jnp.concatenate(outs[s], axis=0) for s in seqs], sts


SEQ_PER_STEP = 4
GLA_SEQ_PER_STEP = 4


def _gla_call(pg, o_f, wgk, bgk, nw, layer, s0, reverse):
    bsz, L, _ = pg.shape
    nblk = L // GLA_BLOCK
    ns = GLA_SEQ_PER_STEP
    blk = (lambda b, i: (b, nblk - 1 - i, 0)) if reverse else (lambda b, i: (b, i, 0))
    st_spec = pl.BlockSpec((ns, GLA_DV, 256), lambda b, i: (b, 0, 0))
    p_spec = pl.BlockSpec((ns, GLA_BLOCK, PG_W), blk)
    o_spec = pl.BlockSpec((ns, GLA_BLOCK, GLA_V), blk)
    if reverse:
        in_specs = [p_spec, o_spec, _lspec((LANE, 256), layer), _lspec((1, 256), layer),
                    _lspec((1, GLA_V), layer), st_spec]
        args = (pg, o_f, wgk, bgk, nw, s0)
    else:
        in_specs = [p_spec, _lspec((LANE, 256), layer), _lspec((1, 256), layer), st_spec]
        args = (pg, wgk, bgk, s0)
    return pl.pallas_call(
        functools.partial(_gla_kernel, reverse=reverse, nblk=nblk),
        grid=(bsz // ns, nblk),
        in_specs=in_specs,
        out_specs=[o_spec, st_spec],
        out_shape=[jax.ShapeDtypeStruct((bsz, L, GLA_V), F32),
                   jax.ShapeDtypeStruct((bsz, GLA_DV, 256), F32)],
        scratch_shapes=[pltpu.VMEM((ns, GLA_DV, 256), F32)],
        compiler_params=_cp("arbitrary", "arbitrary"),
        name="gla_bwd" if reverse else "gla_fwd",
    )(*args)


def _expand_heads(t, lo):
    r = t.shape[0]
    lane = lax.broadcasted_iota(jnp.int32, (1, LANE), 1)
    tiles = []
    for j in range(SSD_HEADS // 2):
        a = jnp.broadcast_to(t[:, lo + 2 * j:lo + 2 * j + 1], (r, LANE))
        b = jnp.broadcast_to(t[:, lo + 2 * j + 1:lo + 2 * j + 2], (r, LANE))
        tiles.append(jnp.where(lane < SSD_P, a, b))
    return jnp.concatenate(tiles, axis=1)


def _ssd_prep(cur, prow, nrow, cw, cb, dtb, alog):
    q_ = SSD_CHUNK
    x = cur[:, 384:1280]
    ridx = lax.broadcasted_iota(jnp.int32, (q_, 1), 0)
    xm = jnp.where(ridx == 0, prow, pltpu.roll(x, 1, 0))
    xp = jnp.where(ridx == q_ - 1, nrow, pltpu.roll(x, q_ - 1, 0))
    act = _silu(cw[0:1] * xm + cw[1:2] * x + cw[2:3] * xp + cb)
    dtt = _softplus(cur[:, 1280:1408] + dtb)
    a = -jnp.exp(alog) * dtt
    return act, dtt, a


def _ssd_scans(xs, bmb, cmb, dtt, a, sts, tri, tri_bf, reverse):
    q_ = SSD_CHUNK
    lo = SSD_HEADS if reverse else 0
    seqs = range(len(xs))
    cs = [_dot_sel(tri_bf, t) for t in a]
    cst = [t.T for t in cs]
    dtT = [t.T for t in dtt]
    edge = 0 if reverse else q_ - 1
    cs_last = [t[edge:edge + 1] for t in cs]
    grp = lambda t, g: t[:, g * SSD_STATE:(g + 1) * SSD_STATE]
    cbs = [[_dot_nt(grp(cmb[s], g), grp(bmb[s], g)) for g in range(SSD_GROUPS)] for s in seqs]
    ms = [[] for _ in seqs]
    for h in range(SSD_HEADS):
        l = lo + h
        for s in seqs:
            seg = cs[s][:, l:l + 1] - cst[s][l:l + 1, :]
            dec = jnp.exp(jnp.where(tri, seg, -jnp.inf))
            ms[s].append(_bf(cbs[s][h // SSD_HG] * dec * dtT[s][l:l + 1, :]))
    mst = [jnp.concatenate(t, axis=0) for t in ms]
    xsb = [_bf(t) for t in xs]
    ra = [_dot(mst[s][:4 * q_], xsb[s][:, 0:256]) for s in seqs]
    rb = [_dot(mst[s][4 * q_:], xsb[s][:, 256:384]) for s in seqs]
    lane = lax.broadcasted_iota(jnp.int32, (1, 256), 1)
    hl = lane // SSD_P
    cs_x = [_expand_heads(t, lo) for t in cs]
    csl_x = [_expand_heads(t, lo) for t in cs_last]
    dt_x = [_expand_heads(t, lo) for t in dtt]
    ystate = [_dot(cmb[s], _bf(sts[s])) for s in seqs]
    xw = [_bf(xs[s] * (jnp.exp(csl_x[s] - cs_x[s]) * dt_x[s])) for s in seqs]
    full = [_dot_tn(bmb[s], xw[s]) for s in seqs]
    r2 = lax.broadcasted_iota(jnp.int32, (2 * SSD_STATE, SSD_INNER), 0) // SSD_STATE
    c2 = lax.broadcasted_iota(jnp.int32, (2 * SSD_STATE, SSD_INNER), 1) // (SSD_HG * SSD_P)
    ys, new_sts = [], []
    for s in seqs:
        ya = jnp.zeros((q_, 256), F32)
        for h in range(4):
            ya = ya + jnp.where(hl == h, ra[s][h * q_:(h + 1) * q_], 0.0)
        yb = jnp.where(hl[:, :LANE] == 0, rb[s][:q_], rb[s][q_:])
        ys.append(jnp.concatenate([ya, yb], axis=1) + jnp.exp(cs_x[s]) * ystate[s])
        new_sts.append(jnp.exp(csl_x[s]) * sts[s] + jnp.where(r2 == c2, full[s], 0.0))
    return ys, new_sts


def _ssd_kernel(*refs, reverse, nchunk):
    if reverse:
        (z_ref, act_ref, dta_ref, yf_ref, dx_ref, nw_ref, s0_ref, y_ref, sf_ref, st_scr) = refs
    else:
        (cur_ref, prev_ref, next_ref, cw_ref, cb_ref, dtb_ref, alog_ref, s0_ref,
         yf_ref, act_ref, dta_ref, sf_ref, st_scr) = refs
    i = pl.program_id(1)
    c = (nchunk - 1 - i) if reverse else i
    q_ = SSD_CHUNK

    @pl.when(i == 0)
    def _():
        st_scr[...] = s0_ref[...]

    ri = lax.broadcasted_iota(jnp.int32, (q_, q_), 0)
    ci = lax.broadcasted_iota(jnp.int32, (q_, q_), 1)
    tri = (ci >= ri) if reverse else (ci <= ri)
    tri_bf = jnp.where(tri, 1.0, 0.0).astype(BF16)

    seqs = range(st_scr.shape[0])
    if reverse:
        actb = [act_ref[s_, 0] for s_ in seqs]
        xs = [t[:, 0:384].astype(F32) for t in actb]
        dtt = [dta_ref[s_, 0][:, :LANE] for s_ in seqs]
        a = [dta_ref[s_, 0][:, LANE:] for s_ in seqs]
    else:
        has_prev = (c > 0).astype(F32)
        has_next = (c < nchunk - 1).astype(F32)
        actb, xs, dtt, a = [], [], [], []
        for s_ in seqs:
            prow = prev_ref[s_, 0][7:8, 384:1280] * has_prev
            nrow = next_ref[s_, 0][0:1, 384:1280] * has_next
            act, dt_, a_ = _ssd_prep(cur_ref[s_, 0], prow, nrow, cw_ref, cb_ref[...], dtb_ref[...], alog_ref[...])
            actb.append(_bf(act))
            act_ref[s_, 0] = actb[s_]
            dta_ref[s_, 0] = jnp.concatenate([dt_, a_], axis=1)
            xs.append(act[:, 0:384])
            dtt.append(dt_)
            a.append(a_)
    ys, sts = _ssd_scans(xs, [t[:, 384:640] for t in actb], [t[:, 640:896] for t in actb], dtt, a,
                         [st_scr[s_] for s_ in seqs], tri, tri_bf, reverse)
    for s_ in seqs:
        st_scr[s_] = sts[s_]

    @pl.when(i == nchunk - 1)
    def _():
        for s_ in seqs:
            sf_ref[s_] = sts[s_]

    for s_ in seqs:
        if not reverse:
            yf_ref[s_, 0] = ys[s_]
        else:
            yt = yf_ref[s_, 0] + ys[s_] + dx_ref[...] * xs[s_]
            yz = yt * _silu(z_ref[s_, 0])
            l384 = lax.broadcasted_iota(jnp.int32, (1, SSD_INNER), 1)
            g0 = l384 < (SSD_INNER // SSD_GROUPS)
            sq = yz * yz
            m0 = jnp.sum(jnp.where(g0, sq, 0.0), axis=-1, keepdims=True)
            m1 = jnp.sum(jnp.where(g0, 0.0, sq), axis=-1, keepdims=True)
            msq = jnp.where(g0, m0, m1) * (1.0 / (SSD_INNER // SSD_GROUPS))
            y_ref[s_, 0] = yz * lax.rsqrt(msq + EPS) * nw_ref[...]


def _ssd_fwd_call(ps, cw, cb, dtb, alog, layer, s0):
    bsz, nchunk, _, _ = ps.shape
    ns = SEQ_PER_STEP
    cur_map = lambda b, i: (b, i, 0, 0)
    prev_map = lambda b, i: (b, jnp.maximum(i - 1, 0), SSD_CHUNK // 8 - 1, 0)
    next_map = lambda b, i: (b, jnp.minimum(i + 1, nchunk - 1), 0, 0)
    const2 = lambda b, i: (0, 0)
    chunk = lambda w: pl.BlockSpec((ns, 1, SSD_CHUNK, w), cur_map)
    st_spec = pl.BlockSpec((ns, 2 * SSD_STATE, SSD_INNER), lambda b, i: (b, 0, 0))
    return pl.pallas_call(
        functools.partial(_ssd_kernel, reverse=False, nchunk=nchunk),
        grid=(bsz // ns, nchunk),
        in_specs=[chunk(PS_W), pl.BlockSpec((ns, 1, 8, PS_W), prev_map), pl.BlockSpec((ns, 1, 8, PS_W), next_map),
                  _lspec((8, SSD_CONV_DIM), layer), _lspec((1, SSD_CONV_DIM), layer),
                  _lspec((1, LANE), layer), _lspec((1, LANE), layer), st_spec],
        out_specs=[chunk(SSD_INNER), chunk(SSD_CONV_DIM), chunk(2 * LANE), st_spec],
        out_shape=[jax.ShapeDtypeStruct((bsz, nchunk, SSD_CHUNK, SSD_INNER), F32),
                   jax.ShapeDtypeStruct((bsz, nchunk, SSD_CHUNK, SSD_CONV_DIM), BF16),
                   jax.ShapeDtypeStruct((bsz, nchunk, SSD_CHUNK, 2 * LANE), F32),
                   jax.ShapeDtypeStruct((bsz, 2 * SSD_STATE, SSD_INNER), F32)],
        scratch_shapes=[pltpu.VMEM((ns, 2 * SSD_STATE, SSD_INNER), F32)],
        compiler_params=_cp("arbitrary", "arbitrary"),
        name="ssd_fwd",
    )(ps, ps, ps, cw, cb, dtb, alog, s0)


def _ssd_bwd_call(ps, act, dta, y_f, dx, nw, layer, s0):
    bsz, nchunk, _, _ = ps.shape
    ns = SEQ_PER_STEP
    cur_map = lambda b, i: (b, nchunk - 1 - i, 0, 0)
    const2 = lambda b, i: (0, 0)
    chunk = lambda w: pl.BlockSpec((ns, 1, SSD_CHUNK, w), cur_map)
    st_spec = pl.BlockSpec((ns, 2 * SSD_STATE, SSD_INNER), lambda b, i: (b, 0, 0))
    return pl.pallas_call(
        functools.partial(_ssd_kernel, reverse=True, nchunk=nchunk),
        grid=(bsz // ns, nchunk),
        in_specs=[chunk(SSD_INNER), chunk(SSD_CONV_DIM), chunk(2 * LANE), chunk(SSD_INNER),
                  _lspec((1, SSD_INNER), layer), _lspec((1, SSD_INNER), layer), st_spec],
        out_specs=[chunk(SSD_INNER), st_spec],
        out_shape=[jax.ShapeDtypeStruct((bsz, nchunk, SSD_CHUNK, SSD_INNER), F32),
                   jax.ShapeDtypeStruct((bsz, 2 * SSD_STATE, SSD_INNER), F32)],
        scratch_shapes=[pltpu.VMEM((ns, 2 * SSD_STATE, SSD_INNER), F32)],
        compiler_params=_cp("arbitrary", "arbitrary"),
        name="ssd_bwd",
    )(ps, act, dta, y_f, dx, nw, s0)


HY_SLOT = LANE // 2


def _hy_filter_kernel(z_ref, w1_ref, b1_ref, w2_ref, b2_ref, w3_ref, fr_ref, dec_ref, h_ref):
    z = z_ref[...]
    half = z.shape[0]
    fr = fr_ref[...]
    h1 = jnp.sin(fr * (_dot_hp(z, w1_ref[...]) + b1_ref[...]))
    h2 = jnp.sin(fr * (_dot_hp(h1, w2_ref[...]) + b2_ref[...]))
    for slot in range(2):
        h = _dot_hp(h2, w3_ref[slot])
        rel = z[:, slot * HY_SLOT:slot * HY_SLOT + 1]
        h = h * jnp.exp(-2.0 * jnp.abs(rel) * dec_ref[...])
        for hf in range(HY_WIDTH // LANE):
            h_ref[hf, slot * half:(slot + 1) * half, :] = h[:, hf * LANE:(hf + 1) * LANE]


def _hy_features(L, tl):
    t = jnp.arange(L, dtype=F32)
    rel = (t - (L // 2)) / L
    bands = jnp.linspace(1e-4, HY_BANDS - 1, HY_BANDS, dtype=F32)
    ang = 2.0 * math.pi * rel[:, None] * bands
    z = jnp.concatenate([rel[:, None], jnp.cos(ang), -jnp.sin(ang)], axis=-1)
    z = jnp.pad(z, ((0, 0), (0, HY_SLOT - HY_EMB)))
    return z.reshape(L // tl, 2, tl // 2, HY_SLOT).transpose(0, 2, 1, 3).reshape(L // 2, LANE)


def _hy_filter_call(L, w1bd, b1t, w2bd, b2t, w3s, frt, dec, layer):
    tl = min(1024, L)
    z = _hy_features(L, tl)
    return pl.pallas_call(
        _hy_filter_kernel,
        grid=(L // tl,),
        in_specs=[pl.BlockSpec((tl // 2, LANE), lambda i: (i, 0)),
                  _lspec((LANE, LANE), layer), _lspec((1, LANE), layer),
                  _lspec((LANE, LANE), layer), _lspec((1, LANE), layer),
                  _lspec((2, LANE, HY_WIDTH), layer), _lspec((1, LANE), layer),
                  _lspec((1, HY_WIDTH), layer)],
        out_specs=pl.BlockSpec((2, tl, LANE), lambda i: (0, i, 0)),
        out_shape=jax.ShapeDtypeStruct((2, L, LANE), F32),
        compiler_params=_cp("arbitrary"),
        name="hy_filter",
    )(z, w1bd, b1t, w2bd, b2t, w3s, frt, dec)


@functools.lru_cache(maxsize=None)
def _fft_consts():
    n1 = FFT_N1
    half = n1 // 2
    k = np.arange(n1, dtype=np.float64)
    n2 = k[:, None, None]
    k1 = k[None, :, None]
    nn = np.arange(half, dtype=np.float64)[None, None, :]
    ang = -2.0 * np.pi * (n2 * k1 / FFT_N + nn * k1 / n1)
    mr, mi = np.cos(ang), np.sin(ang)
    m1 = np.concatenate([np.concatenate([mr, -mi], axis=2), np.concatenate([mi, mr], axis=2)], axis=1)
    ang2 = -2.0 * np.pi * np.outer(k, k) / n1
    fr, fi = np.cos(ang2), np.sin(ang2)
    f2 = np.block([[fr, -fi], [fi, fr]])
    f2c = np.block([[fr, fi], [-fi, fr]])
    no = (np.arange(half, dtype=np.float64) + n1 // 4)[None, :, None]
    kk = k[None, None, :]
    ang3 = 2.0 * np.pi * (n2 * kk / FFT_N + no * kk / n1)
    ir, ii = np.cos(ang3) / FFT_N, np.sin(ang3) / FFT_N
    m3 = np.concatenate([np.concatenate([ir, -ii], axis=2), np.concatenate([ii, ir], axis=2)], axis=1)
    return tuple(np.asarray(m, dtype=np.float32) for m in (m1, f2, f2c, m3))


FFT_NB = 16
FFT_HALF = FFT_N1 // 2


def _strided_rows(ref2d, start, n):
    return ref2d[pl.ds(start, n, stride=FFT_NB), :]


def _tok_flat(ref):
    return ref.reshape(ref.shape[0] * 2 * FFT_HALF * FFT_NB, LANE)


def _tok_row_slice(s, hf, t):
    return pl.ds((2 * s + hf) * FFT_HALF * FFT_NB + t, FFT_HALF, stride=FFT_NB)


def _tok_rows(ref, s, t):
    flat = _tok_flat(ref)
    return jnp.concatenate([flat[_tok_row_slice(s, hf, t), :] for hf in range(2)], axis=1)


def _stage_f32(dst, src_ref):
    v = src_ref[0].astype(F32).reshape(FFT_N1 * FFT_NB, HY_WIDTH)
    dst[0] = v[:, :LANE]
    dst[1] = v[:, LANE:]


def _staged_rows(scr, t):
    return jnp.concatenate([scr[hf, pl.ds(t, FFT_N1, stride=FFT_NB), :] for hf in range(2)], axis=1)


def _fft1_kernel(u_ref, m_ref, a_ref, *, nsig):
    for t in range(FFT_NB):
        rhs = _bf(jnp.concatenate([_tok_rows(u_ref, s, t) for s in range(nsig)], axis=0))
        a_ref[0, t] = _bf(_dot(m_ref[t], rhs))


def _fft1_call(u, m1, nsig):
    npair = u.shape[0] // nsig
    return pl.pallas_call(
        functools.partial(_fft1_kernel, nsig=nsig),
        grid=(npair, FFT_N1 // FFT_NB),
        in_specs=[pl.BlockSpec((nsig, 2, FFT_HALF, FFT_NB, LANE), lambda p, j: (p, 0, 0, j, 0)),
                  pl.BlockSpec((FFT_NB, 2 * FFT_N1, FFT_HALF * nsig), lambda p, j: (j, 0, 0))],
        out_specs=pl.BlockSpec((1, FFT_NB, 2 * FFT_N1, HY_WIDTH), lambda p, j: (p, j, 0, 0)),
        out_shape=jax.ShapeDtypeStruct((npair, FFT_N1, 2 * FFT_N1, HY_WIDTH), BF16),
        compiler_params=_cp("arbitrary", "arbitrary"),
        name="hy_fft1",
    )(u, m1)


def _fft2_kernel(*refs, spectrum):
    if spectrum:
        ar_ref, ai_ref, f_ref, o_ref, sr, si = refs
    else:
        ar_ref, ai_ref, f_ref, fc_ref, h_ref, o_ref, sr, si = refs
    _stage_f32(sr, ar_ref)
    _stage_f32(si, ai_ref)
    for t in range(FFT_NB):
        rhs = _bf(jnp.concatenate([_staged_rows(sr, t), _staged_rows(si, t)], axis=0))
        x = _dot(f_ref[...], rhs)
        if spectrum:
            o_ref[t] = x
        else:
            xr, xi = x[:FFT_N1], x[FFT_N1:]
            hr, hi = h_ref[t, :FFT_N1], h_ref[t, FFT_N1:]
            y = jnp.concatenate([xr * hr - xi * hi, xr * hi + xi * hr], axis=0)
            o_ref[0, t] = _bf(_dot(fc_ref[...], _bf(y)))


def _fft2_call(a, f2, f2c, hspec):
    npair = a.shape[0]
    av = a
    nj = FFT_N1 // FFT_NB
    c2 = lambda p, j: (0, 0)
    in_specs = [pl.BlockSpec((1, FFT_N1, FFT_NB, HY_WIDTH), lambda p, j: (p, 0, j, 0)),
                pl.BlockSpec((1, FFT_N1, FFT_NB, HY_WIDTH), lambda p, j: (p, 0, nj + j, 0)),
                pl.BlockSpec((2 * FFT_N1, 2 * FFT_N1), c2)]
    staging = [pltpu.VMEM((2, FFT_N1 * FFT_NB, LANE), F32), pltpu.VMEM((2, FFT_N1 * FFT_NB, LANE), F32)]
    if hspec is None:
        return pl.pallas_call(
            functools.partial(_fft2_kernel, spectrum=True),
            grid=(1, nj),
            in_specs=in_specs,
            out_specs=pl.BlockSpec((FFT_NB, 2 * FFT_N1, HY_WIDTH), lambda p, j: (j, 0, 0)),
            out_shape=jax.ShapeDtypeStruct((FFT_N1, 2 * FFT_N1, HY_WIDTH), F32),
            scratch_shapes=staging,
            compiler_params=_cp("arbitrary", "arbitrary"),
            name="hy_fft2_spec",
        )(av, av, f2)
    in_specs += [pl.BlockSpec((2 * FFT_N1, 2 * FFT_N1), c2),
                 pl.BlockSpec((FFT_NB, 2 * FFT_N1, HY_WIDTH), lambda p, j: (j, 0, 0))]
    return pl.pallas_call(
        functools.partial(_fft2_kernel, spectrum=False),
        grid=(npair, nj),
        in_specs=in_specs,
        out_specs=pl.BlockSpec((1, FFT_NB, 2 * FFT_N1, HY_WIDTH), lambda p, j: (p, j, 0, 0)),
        out_shape=jax.ShapeDtypeStruct((npair, FFT_N1, 2 * FFT_N1, HY_WIDTH), BF16),
        scratch_shapes=staging,
        compiler_params=_cp("arbitrary", "arbitrary"),
        name="hy_fft2",
    )(av, av, f2, f2c, hspec)


def _fft3_kernel(br_ref, bi_ref, m_ref, vg_ref, x0_ref, bias_ref, y_ref, sr, si):
    _stage_f32(sr, br_ref)
    _stage_f32(si, bi_ref)
    for t in range(FFT_NB):
        rhs = _bf(jnp.concatenate([_staged_rows(sr, t), _staged_rows(si, t)], axis=0))
        out = _dot(m_ref[t], rhs)
        for s in range(2):
            conv = out[s * FFT_HALF:(s + 1) * FFT_HALF]
            y = (conv + _tok_rows(vg_ref, s, t) * bias_ref[...]) * _tok_rows(x0_ref, s, t)
            for hf in range(2):
                _tok_flat(y_ref)[_tok_row_slice(s, hf, t), :] = y[:, hf * LANE:(hf + 1) * LANE]


def _fft3_call(bmat, m3, vg, x0, bias, layer):
    npair = bmat.shape[0]
    nj = FFT_N1 // FFT_NB
    tok_spec = pl.BlockSpec((2, 2, FFT_HALF, FFT_NB, LANE), lambda p, j: (p, 0, 0, j, 0))
    return pl.pallas_call(
        _fft3_kernel,
        grid=(npair, nj),
        in_specs=[pl.BlockSpec((1, FFT_N1, FFT_NB, HY_WIDTH), lambda p, j: (p, 0, j, 0)),
                  pl.BlockSpec((1, FFT_N1, FFT_NB, HY_WIDTH), lambda p, j: (p, 0, nj + j, 0)),
                  pl.BlockSpec((FFT_NB, FFT_N1, 2 * FFT_N1), lambda p, j: (j, 0, 0)),
                  tok_spec, tok_spec,
                  _lspec((1, HY_WIDTH), layer)],
        out_specs=tok_spec,
        out_shape=jax.ShapeDtypeStruct(vg.shape, F32),
        scratch_shapes=[pltpu.VMEM((2, FFT_N1 * FFT_NB, LANE), F32), pltpu.VMEM((2, FFT_N1 * FFT_NB, LANE), F32)],
        compiler_params=_cp("arbitrary", "arbitrary"),
        name="hy_fft3",
    )(bmat, bmat, m3, vg, x0, bias)


def _hy_direct_kernel(vg_ref, x0_ref, h_ref, bias_ref, y_ref, pad_scr, sh_scr):
    L = vg_ref.shape[2]
    u = jnp.concatenate([vg_ref[0, 0], vg_ref[0, 1]], axis=1)
    pad_scr[...] = jnp.zeros_like(pad_scr)
    pad_scr[L:2 * L, :] = u
    top = L + L // 2
    acc = jnp.zeros((L, HY_WIDTH), F32)
    for r in range(8):
        sh_scr[...] = pad_scr[r:r + 3 * L - 8, :]
        a_lo = -(-(top - L + 1 - r) // 8)
        a_hi = (top - r) // 8

        def body(a, acc, r=r):
            m = top - (a * 8 + r)
            tap = jnp.concatenate([h_ref[0, pl.ds(m, 1), :], h_ref[1, pl.ds(m, 1), :]], axis=1)
            return acc + tap * sh_scr[pl.ds(pl.multiple_of(a * 8, 8), L), :]

        acc = lax.fori_loop(a_lo, a_hi + 1, body, acc, unroll=4)
    y = (acc + u * bias_ref[...]) * jnp.concatenate([x0_ref[0, 0], x0_ref[0, 1]], axis=1)
    y_ref[0, 0] = y[:, :LANE]
    y_ref[0, 1] = y[:, LANE:]


def _hy_direct_call(vg, x0, h, bias, layer):
    bsz, _, L, _ = vg.shape
    tok = pl.BlockSpec((1, 2, L, LANE), lambda b: (b, 0, 0, 0))
    return pl.pallas_call(
        _hy_direct_kernel,
        grid=(bsz,),
        in_specs=[tok, tok, pl.BlockSpec((2, L, LANE), lambda b: (0, 0, 0)),
                  _lspec((1, HY_WIDTH), layer)],
        out_specs=tok,
        out_shape=jax.ShapeDtypeStruct(vg.shape, F32),
        scratch_shapes=[pltpu.VMEM((3 * L, HY_WIDTH), F32), pltpu.VMEM((3 * L - 8, HY_WIDTH), F32)],
        compiler_params=_cp("arbitrary"),
        name="hy_direct",
    )(vg, x0, h, bias)


def _out_kernel(x_ref, yg_ref, yh_ref, ys_ref, mod_ref, g2_ref, gf_ref, wo_ref, w1_ref, w3_ref, w2_ref,
                o_ref, *scr, final, col_major):
    m = mod_ref[0]
    if col_major:
        ys_scr, = scr
        for c in range(GRID_W):
            for k in range(SSD_INNER // LANE):
                ys_scr[k, pl.ds(c, ROWS_PER_TILE, stride=GRID_W), :] = ys_ref[0, c, :, k * LANE:(k + 1) * LANE]
        ys = jnp.concatenate([ys_scr[k] for k in range(SSD_INNER // LANE)], axis=1)
    else:
        ys = ys_ref[...]
    yh = jnp.concatenate([yh_ref[0, 0], yh_ref[0, 1]], axis=1)
    tm = x_ref.shape[0]
    nrg = ROW_GROUPS if tm % (8 * ROW_GROUPS) == 0 else 1
    rgs = [slice(r * (tm // nrg), (r + 1) * (tm // nrg)) for r in range(nrg)]
    mix = [(_dot(_bf(yg_ref[r, :]), wo_ref[0:GLA_V])
            + _dot(_bf(yh[r]), wo_ref[GLA_V:GLA_V + HY_WIDTH])
            + _dot(_bf(ys[r]), wo_ref[GLA_V + HY_WIDTH:])) for r in rgs]
    x1 = [x_ref[r, :] + m[2:3] * mx for r, mx in zip(rgs, mix)]
    ms = [jnp.mean(t * t, axis=-1, keepdims=True) for t in x1]
    h = [t * lax.rsqrt(s + EPS) * g2_ref[...] for t, s in zip(x1, ms)]
    h = [_bf(t * (1.0 + m[4:5]) + m[3:4]) for t in h]
    ffn = [None] * nrg
    for j in range(D_FF // FF_TILE):
        cols = slice(j * FF_TILE, (j + 1) * FF_TILE)
        a = [_dot(t, w1_ref[:, cols]) for t in h]
        b = [_dot(t, w3_ref[:, cols]) for t in h]
        part = [_dot(_bf(_silu(u) * v), w2_ref[cols, :]) for u, v in zip(a, b)]
        ffn = [p if f is None else f + p for f, p in zip(ffn, part)]
    for r, t, f in zip(rgs, x1, ffn):
        x2 = t + m[5:6] * f
        if final:
            ms2 = jnp.mean(x2 * x2, axis=-1, keepdims=True)
            x2 = x2 * lax.rsqrt(ms2 + EPS) * gf_ref[...]
        o_ref[r, :] = x2


def _out_call(x2, yg, yh, ys, mods, mod_row, g2, gf, wo, w1, w3, w2, layer, final, col_major):
    t = x2.shape[0]
    seq = yh.shape[2]
    tm = min(TOK_TILE, seq)
    tiles = seq // tm
    c2 = lambda i: (0, 0)
    if col_major:
        assert tm == TOK_TILE
        ys_spec = pl.BlockSpec((1, GRID_W, ROWS_PER_TILE, SSD_INNER), lambda i: (i // tiles, 0, i % tiles, 0))
        extra = [pltpu.VMEM((SSD_INNER // LANE, tm, LANE), F32)]
    else:
        ys = ys.reshape(t, SSD_INNER)
        ys_spec = pl.BlockSpec((tm, SSD_INNER), lambda i: (i, 0))
        extra = []
    resident = lambda shape: pl.BlockSpec((None,) + shape, lambda i: (layer, 0, 0), pipeline_mode=pl.Buffered(1))
    return pl.pallas_call(
        functools.partial(_out_kernel, final=final, col_major=col_major),
        grid=(t // tm,),
        in_specs=[pl.BlockSpec((tm, D_MODEL), lambda i: (i, 0)),
                  pl.BlockSpec((tm, GLA_V), lambda i: (i, 0)),
                  pl.BlockSpec((1, 2, tm, LANE), lambda i: (i // tiles, 0, i % tiles, 0)),
                  ys_spec,
                  pl.BlockSpec((None, 1, 6, D_MODEL), lambda i: (layer, mod_row(i), 0, 0)),
                  _lspec((1, D_MODEL), layer),
                  pl.BlockSpec((1, D_MODEL), c2),
                  resident((D_MODEL, D_MODEL)),
                  resident((D_MODEL, D_FF)),
                  resident((D_MODEL, D_FF)),
                  resident((D_FF, D_MODEL))],
        out_specs=pl.BlockSpec((tm, D_MODEL), lambda i: (i, 0)),
        out_shape=jax.ShapeDtypeStruct((t, D_MODEL), F32),
        scratch_shapes=extra,
        compiler_params=_cp("arbitrary"),
        name="out_ffn",
    )(x2, yg, yh, ys, mods, g2, gf.reshape(1, D_MODEL), wo, w1, w3, w2)


W_IN_MOVES = ((0, 192, 0), (192, 384, 256), (384, 1184, 512), (1184, 3244, PG_W))
D_IN = 3244


def _pack_kernel(w_ref, o_ref):
    o_ref[...] = jnp.zeros(o_ref.shape, BF16)
    for src0, src1, dst in W_IN_MOVES:
        o_ref[0, :, dst:dst + src1 - src0] = w_ref[0, :, src0:src1].astype(BF16)


def _pack_w_in(w_in):
    rows = 128
    wtot = PG_W + PH_W + PS_W
    return pl.pallas_call(
        _pack_kernel,
        grid=(DEPTH, D_MODEL // rows),
        in_specs=[pl.BlockSpec((1, rows, D_IN), lambda l, i: (l, i, 0))],
        out_specs=pl.BlockSpec((1, rows, wtot), lambda l, i: (l, i, 0)),
        out_shape=jax.ShapeDtypeStruct((DEPTH, D_MODEL, wtot), BF16),
        compiler_params=_cp("arbitrary", "arbitrary"),
        name="pack_w_in",
    )(w_in)


def _pad_to(a, shape):
    return jnp.pad(a, [(0, s - d) for d, s in zip(a.shape, shape)])


def _mixers(pg, vg, x0, ps, lw, l, states):
    g_f0, g_b0, m_f0, m_b0 = states
    o_f, g_f = _gla_call(pg, None, lw['wgk_f'], lw['bgk_f'], None, l, g_f0, False)
    gla_y, g_b = _gla_call(pg, o_f, lw['wgk_b'], lw['bgk_b'], lw['gla_nw'], l, g_b0, True)

    L = vg.shape[2]
    h = _hy_filter_call(L, lw['hy_w1'], lw['hy_b1'], lw['hy_w2'], lw['hy_b2'], lw['hy_w3'], lw['hy_freq'],
                        lw['hy_decay'], l)
    if L == FFT_N // 2:
        m1, m1_real, f2, f2c, m3 = lw['fft']
        tok5 = lambda t: t.reshape(t.shape[0], 2, FFT_HALF, FFT_N1, LANE)
        hspec = _fft2_call(_fft1_call(tok5(h[None]), m1_real, 1), f2, f2c, None)
        vg5 = tok5(vg)
        bmat = _fft2_call(_fft1_call(vg5, m1, 2), f2, f2c, hspec)
        hy_y = _fft3_call(bmat, m3, vg5, tok5(x0), lw['hy_bias'], l).reshape(vg.shape)
    else:
        hy_y = _hy_direct_call(vg, x0, h, lw['hy_bias'], l)

    y_f, act, dta, m_f = _ssd_fwd_call(ps, lw['ssd_cw'], lw['ssd_cb'], lw['ssd_dtb'], lw['ssd_alog'], l, m_f0)
    ssd_y, m_b = _ssd_bwd_call(ps, act, dta, y_f, lw['ssd_dx'], lw['ssd_nw'], l, m_b0)
    return (gla_y, hy_y, ssd_y), (g_f, g_b, m_f, m_b)


def kernel(x, c, ctx, c_ctx, mod_w, mod_b, norm1_g, norm2_g, w_in, gla_gk_w_f, gla_gk_b_f, gla_gk_w_b, gla_gk_b_b, gla_norm_w, hy_short_w, hy_short_b, hy_w1, hy_b1, hy_w2, hy_b2, hy_w3, hy_freq, hy_decay, hy_bias, ssd_conv_w, ssd_conv_b, ssd_dt_bias_f, ssd_dt_bias_b, ssd_a_log_f, ssd_a_log_b, ssd_d, ssd_norm_w, w_out, ffn_w1, ffn_w3, ffn_w2, final_g):
    bsz, seq, _ = x.shape
    lc = ctx.shape[1]
    c8 = jnp.concatenate([c, c_ctx[None], jnp.zeros((8 - bsz - 1, D_MODEL), F32)], axis=0)
    mods_all = _mod_call(c8, mod_w, mod_b).reshape(DEPTH, 8, 6, D_MODEL)

    w_in_p = _pack_w_in(w_in)
    m1_c, f2_c, f2c_c, m3_c = _fft_consts()
    fft_mats = tuple(_bf(jnp.asarray(m)) for m in (m1_c, np.ascontiguousarray(m1_c[:, :, :FFT_HALF]), f2_c, f2c_c, m3_c))
    w_out_b, w1_b, w3_b, w2_b = _bf(w_out), _bf(ffn_w1), _bf(ffn_w3), _bf(ffn_w2)

    xt = x.reshape(bsz * seq, D_MODEL)
    ct = ctx.reshape(bsz * lc, D_MODEL)
    tiles_per_seq = seq // TOK_TILE
    row_x = lambda i: i // tiles_per_seq
    row_c = lambda i: bsz

    zeros_states = (jnp.zeros((bsz, GLA_DV, 256), F32), jnp.zeros((bsz, GLA_DV, 256), F32),
                    jnp.zeros((bsz, 2 * SSD_STATE, SSD_INNER), F32),
                    jnp.zeros((bsz, 2 * SSD_STATE, SSD_INNER), F32))

    def gkw(w, off):
        return _bf(_pad_to(jnp.pad(w, ((0, 0), (off, 0), (0, 0))), (DEPTH, LANE, 256)))

    def bdiag(w):
        z = jnp.zeros_like(w)
        return jnp.concatenate([jnp.concatenate([w, z], axis=2), jnp.concatenate([z, w], axis=2)], axis=1)

    row = lambda v: v[:, None, :]
    lw = {
        'wgk_f': gkw(gla_gk_w_f, 0), 'wgk_b': gkw(gla_gk_w_b, GLA_LOWRANK),
        'bgk_f': _pad_to(row(gla_gk_b_f), (DEPTH, 1, 256)), 'bgk_b': _pad_to(row(gla_gk_b_b), (DEPTH, 1, 256)),
        'gla_nw': row(jnp.tile(gla_norm_w, (1, GLA_HEADS))),
        'hy_sw': _pad_to(hy_short_w, (DEPTH, 8, PH_W)), 'hy_sb': row(hy_short_b),
        'hy_w1': bdiag(_pad_to(hy_w1, (DEPTH, HY_SLOT, HY_SLOT))), 'hy_b1': row(jnp.tile(hy_b1, (1, 2))),
        'hy_w2': bdiag(hy_w2), 'hy_b2': row(jnp.tile(hy_b2, (1, 2))),
        'hy_w3': jnp.stack([jnp.pad(hy_w3, ((0, 0), (0, HY_SLOT), (0, 0))),
                            jnp.pad(hy_w3, ((0, 0), (HY_SLOT, 0), (0, 0)))], axis=1),
        'hy_freq': row(jnp.tile(hy_freq, (1, 2))),
        'hy_decay': row(hy_decay), 'hy_bias': row(hy_bias),
        'ssd_cw': _pad_to(ssd_conv_w, (DEPTH, 8, SSD_CONV_DIM)), 'ssd_cb': row(ssd_conv_b),
        'ssd_dtb': _pad_to(row(jnp.concatenate([ssd_dt_bias_f, ssd_dt_bias_b], axis=1)), (DEPTH, 1, LANE)),
        'ssd_alog': _pad_to(row(jnp.concatenate([ssd_a_log_f, ssd_a_log_b], axis=1)), (DEPTH, 1, LANE)),
        'ssd_dx': row(jnp.repeat(ssd_d, SSD_P, axis=1)), 'ssd_nw': row(ssd_norm_w),
        'fft': fft_mats,
    }
    g1, g2 = row(norm1_g), row(norm2_g)

    for l in range(DEPTH):
        in_args = (g1, w_in_p, l, lw['hy_sw'], lw['hy_sb'], bsz)
        yc, ctx_states = _mixers(*_in_call(ct, mods_all, row_c, *in_args, False), lw, l, zeros_states)
        yx, _ = _mixers(*_in_call(xt, mods_all, row_x, *in_args, True), lw, l, ctx_states)
        last = l == DEPTH - 1
        ffn = (w_out_b, w1_b, w3_b, w2_b, l)
        xt = _out_call(xt, yx[0].reshape(-1, GLA_V), yx[1], yx[2],
                       mods_all, row_x, g2, final_g, *ffn, last, True)
        if not last:
            ct = _out_call(ct, yc[0].reshape(-1, GLA_V), yc[1], yc[2],
                           mods_all, row_c, g2, final_g, *ffn, False, False)
    return xt.reshape(bsz, seq, D_MODEL)
```

```python
import functools
import math

import numpy as np
import jax
import jax.numpy as jnp
from jax import lax
from jax.experimental import pallas as pl
from jax.experimental.pallas import tpu as pltpu

F32 = jnp.float32
BF16 = jnp.bfloat16

D_MODEL = 1024
DEPTH = 2
GRID_W = 64
EPS = 1e-6
GLA_V = 384
GLA_DV = 64
GLA_HEADS = 6
GLA_DK = 32
GLA_QK = 192
GLA_LOWRANK = 16
GLA_TAU = 16.0
GLA_CHUNK = 64
GLA_BLOCK = 256
HY_WIDTH = 256
HY_BANDS = 16
HY_EMB = 1 + 2 * HY_BANDS
HY_ORDER = 64
SSD_INNER = 384
SSD_HEADS = 6
SSD_GROUPS = 2
SSD_HG = 3
SSD_P = 64
SSD_STATE = 128
SSD_CONV_DIM = SSD_INNER + 2 * SSD_GROUPS * SSD_STATE
SSD_CHUNK = 128
D_FF = 2816
PG_W = 1408
PH_W = 768
PS_W = 1408
LANE = 128
FFT_N1 = 128
FFT_N = FFT_N1 * FFT_N1

VMEM_LIMIT = 56 * 1024 * 1024


def _cp(*sem):
    return pltpu.CompilerParams(dimension_semantics=sem, vmem_limit_bytes=VMEM_LIMIT)


def _bf(x):
    return x.astype(BF16)


def _lspec(shape, layer):
    zeros = (0,) * len(shape)
    return pl.BlockSpec((None,) + tuple(shape), lambda *_: (layer,) + zeros)


def _dot(a, b):
    return jnp.dot(a, b, preferred_element_type=F32)


def _dot_nt(a, b):
    return lax.dot_general(a, b, (((1,), (1,)), ((), ())), preferred_element_type=F32)


def _dot_tn(a, b):
    return lax.dot_general(a, b, (((0,), (0,)), ((), ())), preferred_element_type=F32)


def _split3(x):
    hi = _bf(x)
    r1 = x - hi.astype(F32)
    mid = _bf(r1)
    lo = _bf(r1 - mid.astype(F32))
    return hi, mid, lo


def _dot_sel(sel_bf, x):
    hi, mid, lo = _split3(x)
    return _dot(sel_bf, hi) + _dot(sel_bf, mid) + _dot(sel_bf, lo)


def _dot_sel2(sel_bf, x):
    hi = _bf(x)
    lo = _bf(x - hi.astype(F32))
    return _dot(sel_bf, hi) + _dot(sel_bf, lo)


def _dot_hp(a, b):
    ah = _bf(a)
    al = _bf(a - ah.astype(F32))
    bh = _bf(b)
    bl = _bf(b - bh.astype(F32))
    return _dot(ah, bh) + _dot(ah, bl) + _dot(al, bh)


def _silu(x):
    return x * jax.nn.sigmoid(x)


def _softplus(x):
    return jnp.maximum(x, 0.0) + jnp.log(1.0 + jnp.exp(-jnp.abs(x)))


def _log_sigmoid(x):
    return jnp.minimum(x, 0.0) - jnp.log(1.0 + jnp.exp(-jnp.abs(x)))


def _mod_kernel(c_ref, w_ref, b_ref, o_ref):
    act = _silu(c_ref[...])
    o_ref[0] = _dot(_bf(act), _bf(w_ref[0])) + b_ref[0]


def _mod_call(c8, mod_w, mod_b):
    nt = 1536
    n = mod_w.shape[-1]
    return pl.pallas_call(
        _mod_kernel,
        grid=(DEPTH, n // nt),
        in_specs=[pl.BlockSpec((8, D_MODEL), lambda l, j: (0, 0)),
                  pl.BlockSpec((1, D_MODEL, nt), lambda l, j: (l, 0, j)),
                  pl.BlockSpec((1, 1, nt), lambda l, j: (l, 0, j))],
        out_specs=pl.BlockSpec((1, 8, nt), lambda l, j: (l, 0, j)),
        out_shape=jax.ShapeDtypeStruct((DEPTH, 8, n), F32),
        compiler_params=_cp("arbitrary", "arbitrary"),
        name="adaln_mod",
    )(c8, mod_w, mod_b.reshape(DEPTH, 1, n))


TOK_TILE = 512
ROW_GROUPS = 2
ROWS_PER_TILE = TOK_TILE // GRID_W


def _in_kernel(x_ref, xp_ref, xn_ref, mod_ref, g_ref, w_ref, hw_ref, hb_ref, og_ref, vg_ref, x0_ref, os_ref, *scr,
               col_major, tiles):
    m = mod_ref[0]
    i = pl.program_id(0)

    def modulated(t):
        ms = jnp.mean(t * t, axis=-1, keepdims=True)
        return _bf((t * lax.rsqrt(ms + EPS) * g_ref[...]) * (1.0 + m[1:2]) + m[0:1])

    tm = x_ref.shape[0]
    rgs = [slice(r * (tm // ROW_GROUPS), (r + 1) * (tm // ROW_GROUPS)) for r in range(ROW_GROUPS)]
    h = [modulated(x_ref[r, :]) for r in rgs]
    p = [_dot(t, w_ref[...]) for t in h]
    for r, t in zip(rgs, p):
        og_ref[r, :] = _bf(t[:, 0:PG_W])
    w_hy = w_ref[:, PG_W:PG_W + PH_W]
    ph = jnp.concatenate([t[:, PG_W:PG_W + PH_W] for t in p], axis=0)
    ps = jnp.concatenate([t[:, PG_W + PH_W:] for t in p], axis=0)

    halo = _dot(modulated(jnp.concatenate([xp_ref[...], xn_ref[...]], axis=0)), w_hy)
    prow = halo[7:8] * (i % tiles != 0).astype(F32)
    nrow = halo[8:9] * (i % tiles != tiles - 1).astype(F32)
    ridx = lax.broadcasted_iota(jnp.int32, (tm, 1), 0)
    pm = jnp.where(ridx == 0, prow, pltpu.roll(ph, 1, 0))
    pn = jnp.where(ridx == tm - 1, nrow, pltpu.roll(ph, tm - 1, 0))
    u = hw_ref[0:1] * pm + hw_ref[1:2] * ph + hw_ref[2:3] * pn + hb_ref[...]
    vg = u[:, 2 * HY_WIDTH:] * u[:, HY_WIDTH:2 * HY_WIDTH]
    for hf in range(HY_WIDTH // LANE):
        x0_ref[0, hf] = u[:, hf * LANE:(hf + 1) * LANE]
        vg_ref[0, hf] = vg[:, hf * LANE:(hf + 1) * LANE]

    if col_major:
        ps_scr, = scr
        for k in range(PS_W // LANE):
            for r in range(ROWS_PER_TILE):
                ps_scr[k, pl.ds(r, GRID_W, stride=ROWS_PER_TILE), :] = ps[r * GRID_W:(r + 1) * GRID_W,
                                                                         k * LANE:(k + 1) * LANE]
        for k in range(PS_W // LANE):
            os_ref[0, :, :, k * LANE:(k + 1) * LANE] = ps_scr[k].reshape(GRID_W, ROWS_PER_TILE, LANE)
    else:
        os_ref[...] = ps


def _in_call(x2, mods, mod_row, g, w_all, layer, hy_w8, hy_b, bsz, col_major):
    t = x2.shape[0]
    seq = t // bsz
    tm = min(TOK_TILE, seq)
    tiles = seq // tm
    wtot = PG_W + PH_W + PS_W
    hb = tm // 8
    if col_major:
        assert seq == GRID_W * SSD_CHUNK and ROWS_PER_TILE == 8 and tm == TOK_TILE
        os_spec = pl.BlockSpec((1, GRID_W, ROWS_PER_TILE, PS_W), lambda i: (i // tiles, 0, i % tiles, 0))
        os_shape = jax.ShapeDtypeStruct((bsz, GRID_W, SSD_CHUNK, PS_W), F32)
        scratch = [pltpu.VMEM((PS_W // LANE, tm, LANE), F32)]
    else:
        os_spec = pl.BlockSpec((tm, PS_W), lambda i: (i, 0))
        os_shape = jax.ShapeDtypeStruct((t, PS_W), F32)
        scratch = []
    nblk8 = t // 8
    tok_spec = pl.BlockSpec((1, 2, tm, LANE), lambda i: (i // tiles, 0, i % tiles, 0))
    tok_shape = jax.ShapeDtypeStruct((bsz, 2, seq, LANE), F32)
    pg, vg, x0, ps = pl.pallas_call(
        functools.partial(_in_kernel, col_major=col_major, tiles=tiles),
        grid=(t // tm,),
        in_specs=[pl.BlockSpec((tm, D_MODEL), lambda i: (i, 0)),
                  pl.BlockSpec((8, D_MODEL), lambda i: (jnp.maximum(i * hb - 1, 0), 0)),
                  pl.BlockSpec((8, D_MODEL), lambda i: (jnp.minimum((i + 1) * hb, nblk8 - 1), 0)),
                  pl.BlockSpec((None, 1, 6, D_MODEL), lambda i: (layer, mod_row(i), 0, 0)),
                  _lspec((1, D_MODEL), layer),
                  pl.BlockSpec((None, D_MODEL, wtot), lambda i: (layer, 0, 0), pipeline_mode=pl.Buffered(1)),
                  _lspec((8, PH_W), layer),
                  _lspec((1, PH_W), layer)],
        out_specs=[pl.BlockSpec((tm, PG_W), lambda i: (i, 0)), tok_spec, tok_spec, os_spec],
        out_shape=[jax.ShapeDtypeStruct((t, PG_W), BF16), tok_shape, tok_shape, os_shape],
        scratch_shapes=scratch,
        compiler_params=_cp("arbitrary"),
        name="in_proj",
    )(x2, x2, x2, mods, g, w_all, hy_w8, hy_b)
    return (pg.reshape(bsz, seq, PG_W), vg, x0, ps.reshape(bsz, seq // SSD_CHUNK, SSD_CHUNK, PS_W))


def _gla_kernel(*refs, reverse, nblk):
    if reverse:
        (p_ref, of_ref, wgk_ref, bgk_ref, nw_ref, s0_ref, y_ref, sf_ref, st_scr) = refs
    else:
        (p_ref, wgk_ref, bgk_ref, s0_ref, of_ref, sf_ref, st_scr) = refs
    i = pl.program_id(1)
    tb = GLA_BLOCK

    @pl.when(i == 0)
    def _():
        st_scr[...] = s0_ref[...]

    ri = lax.broadcasted_iota(jnp.int32, (tb, tb), 0)
    ci = lax.broadcasted_iota(jnp.int32, (tb, tb), 1)
    same = (ri // GLA_CHUNK) == (ci // GLA_CHUNK)
    tri = same & ((ci >= ri) if reverse else (ci <= ri))
    tri_bf = jnp.where(tri, 1.0, 0.0).astype(BF16)
    lane = lax.broadcasted_iota(jnp.int32, (1, 256), 1)
    hms = [(lane // GLA_DK) == h for h in range(GLA_HEADS)]

    nseq = p_ref.shape[0]
    seqs = range(nseq)
    ps = [p_ref[s_] for s_ in seqs]
    os_, sts = _gla_blocks(ps, [st_scr[s_] for s_ in seqs], wgk_ref[...], bgk_ref[...], tri, tri_bf, lane, hms,
                           reverse)
    for s_ in seqs:
        st_scr[s_] = sts[s_]

    @pl.when(i == nblk - 1)
    def _():
        for s_ in seqs:
            sf_ref[s_] = sts[s_]

    if not reverse:
        for s_ in seqs:
            of_ref[s_] = os_[s_]
    else:
        r2 = lax.broadcasted_iota(jnp.int32, (GLA_V, GLA_V), 0) // GLA_DV
        c2 = lax.broadcasted_iota(jnp.int32, (GLA_V, GLA_V), 1) // GLA_DV
        ind = jnp.where(r2 == c2, 1.0, 0.0).astype(BF16)
        ot = [of_ref[s_] + os_[s_] for s_ in seqs]
        ms = [_dot(_bf(t * t), ind) * (1.0 / GLA_DV) for t in ot]
        for s_ in seqs:
            g = ps[s_][:, 896:1280].astype(F32)
            y_ref[s_] = ot[s_] * lax.rsqrt(ms[s_] + EPS) * nw_ref[...] * _silu(g)


def _gla_blocks(ps, sts, wgk, bgk, tri, tri_bf, lane, hms, reverse):
    tb = GLA_BLOCK
    nch = tb // GLA_CHUNK
    seqs = range(len(ps))
    k = [p[:, 256:512].astype(F32) for p in ps]
    pre = [_dot(p[:, 1280:1408], wgk) + bgk for p in ps]
    la = [_log_sigmoid(t) * (1.0 / GLA_TAU) for t in pre]
    b = [_dot_sel2(tri_bf, t) for t in la]
    qd = [_bf(ps[s][:, 0:256].astype(F32) * ((GLA_DK ** -0.5) * jnp.exp(b[s]))) for s in seqs]
    ki = [_bf(k[s] * jnp.exp(-b[s])) for s in seqs]
    vb = [p[:, 512:896] for p in ps]

    lhs = [jnp.concatenate([jnp.where(hm, t, jnp.zeros_like(t)) for hm in hms], axis=0) for t in qd]
    sc = [_dot_nt(lhs[s], ki[s]) for s in seqs]
    pm = [jnp.concatenate([_bf(jnp.where(tri, t[h * tb:(h + 1) * tb], 0.0)) for h in range(GLA_HEADS)], axis=0)
          for t in sc]
    ra = [_dot(pm[s][:4 * tb], vb[s][:, 0:256]) for s in seqs]
    rb = [_dot(pm[s][4 * tb:], vb[s][:, 256:384]) for s in seqs]
    hl = lane // GLA_DV
    o_intra = []
    for s in seqs:
        oa = jnp.zeros((tb, 256), F32)
        for h in range(4):
            oa = oa + jnp.where(hl == h, ra[s][h * tb:(h + 1) * tb], 0.0)
        ob = jnp.where(hl[:, :LANE] == 0, rb[s][:tb], rb[s][tb:])
        o_intra.append(jnp.concatenate([oa, ob], axis=1))

    sts = list(sts)
    outs = [[None] * nch for _ in seqs]
    order = range(nch - 1, -1, -1) if reverse else range(nch)
    for c in order:
        r0 = c * GLA_CHUNK
        rows = slice(r0, r0 + GLA_CHUNK)
        edge = r0 if reverse else r0 + GLA_CHUNK - 1
        bl = [t[edge:edge + 1] for t in b]
        kd = [_bf(k[s][rows] * jnp.exp(bl[s] - b[s][rows])) for s in seqs]
        stbd = [_bf(jnp.concatenate([jnp.where(hm, t, 0.0) for hm in hms], axis=0)) for t in sts]
        for s in seqs:
            outs[s][c] = _dot_nt(qd[s][rows], stbd[s])
        full = [_dot_tn(vb[s][rows], kd[s]) for s in seqs]
        for s in seqs:
            ds = jnp.zeros((GLA_DV, 256), F32)
            for h in range(GLA_HEADS):
                ds = ds + jnp.where(hms[h], full[s][h * GLA_DV:(h + 1) * GLA_DV], 0.0)
            sts[s] = jnp.exp(bl[s]) * sts[s] + ds
    return [o_intra[s] + jnp.concatenate(outs[s], axis=0) for s in seqs], sts


SEQ_PER_STEP = 4
GLA_SEQ_PER_STEP = 4


def _gla_call(pg, o_f, wgk, bgk, nw, layer, s0, reverse):
    bsz, L, _ = pg.shape
    nblk = L // GLA_BLOCK
    ns = GLA_SEQ_PER_STEP
    blk = (lambda b, i: (b, nblk - 1 - i, 0)) if reverse else (lambda b, i: (b, i, 0))
    st_spec = pl.BlockSpec((ns, GLA_DV, 256), lambda b, i: (b, 0, 0))
    p_spec = pl.BlockSpec((ns, GLA_BLOCK, PG_W), blk)
    o_spec = pl.BlockSpec((ns, GLA_BLOCK, GLA_V), blk)
    if reverse:
        in_specs = [p_spec, o_spec, _lspec((LANE, 256), layer), _lspec((1, 256), layer),
                    _lspec((1, GLA_V), layer), st_spec]
        args = (pg, o_f, wgk, bgk, nw, s0)
    else:
        in_specs = [p_spec, _lspec((LANE, 256), layer), _lspec((1, 256), layer), st_spec]
        args = (pg, wgk, bgk, s0)
    return pl.pallas_call(
        functools.partial(_gla_kernel, reverse=reverse, nblk=nblk),
        grid=(bsz // ns, nblk),
        in_specs=in_specs,
        out_specs=[o_spec, st_spec],
        out_shape=[jax.ShapeDtypeStruct((bsz, L, GLA_V), F32),
                   jax.ShapeDtypeStruct((bsz, GLA_DV, 256), F32)],
        scratch_shapes=[pltpu.VMEM((ns, GLA_DV, 256), F32)],
        compiler_params=_cp("arbitrary", "arbitrary"),
        name="gla_bwd" if reverse else "gla_fwd",
    )(*args)


def _expand_heads(t, lo):
    r = t.shape[0]
    lane = lax.broadcasted_iota(jnp.int32, (1, LANE), 1)
    tiles = []
    for j in range(SSD_HEADS // 2):
        a = jnp.broadcast_to(t[:, lo + 2 * j:lo + 2 * j + 1], (r, LANE))
        b = jnp.broadcast_to(t[:, lo + 2 * j + 1:lo + 2 * j + 2], (r, LANE))
        tiles.append(jnp.where(lane < SSD_P, a, b))
    return jnp.concatenate(tiles, axis=1)


def _ssd_prep(cur, prow, nrow, cw, cb, dtb, alog):
    q_ = SSD_CHUNK
    x = cur[:, 384:1280]
    ridx = lax.broadcasted_iota(jnp.int32, (q_, 1), 0)
    xm = jnp.where(ridx == 0, prow, pltpu.roll(x, 1, 0))
    xp = jnp.where(ridx == q_ - 1, nrow, pltpu.roll(x, q_ - 1, 0))
    act = _silu(cw[0:1] * xm + cw[1:2] * x + cw[2:3] * xp + cb)
    dtt = _softplus(cur[:, 1280:1408] + dtb)
    a = -jnp.exp(alog) * dtt
    return act, dtt, a


def _ssd_scans(xs, bmb, cmb, dtt, a, sts, tri, tri_bf, reverse):
    q_ = SSD_CHUNK
    lo = SSD_HEADS if reverse else 0
    seqs = range(len(xs))
    cs = [_dot_sel(tri_bf, t) for t in a]
    cst = [t.T for t in cs]
    dtT = [t.T for t in dtt]
    edge = 0 if reverse else q_ - 1
    cs_last = [t[edge:edge + 1] for t in cs]
    grp = lambda t, g: t[:, g * SSD_STATE:(g + 1) * SSD_STATE]
    cbs = [[_dot_nt(grp(cmb[s], g), grp(bmb[s], g)) for g in range(SSD_GROUPS)] for s in seqs]
    ms = [[] for _ in seqs]
    for h in range(SSD_HEADS):
        l = lo + h
        for s in seqs:
            seg = cs[s][:, l:l + 1] - cst[s][l:l + 1, :]
            dec = jnp.exp(jnp.where(tri, seg, -jnp.inf))
            ms[s].append(_bf(cbs[s][h // SSD_HG] * dec * dtT[s][l:l + 1, :]))
    mst = [jnp.concatenate(t, axis=0) for t in ms]
    xsb = [_bf(t) for t in xs]
    ra = [_dot(mst[s][:4 * q_], xsb[s][:, 0:256]) for s in seqs]
    rb = [_dot(mst[s][4 * q_:], xsb[s][:, 256:384]) for s in seqs]
    lane = lax.broadcasted_iota(jnp.int32, (1, 256), 1)
    hl = lane // SSD_P
    cs_x = [_expand_heads(t, lo) for t in cs]
    csl_x = [_expand_heads(t, lo) for t in cs_last]
    dt_x = [_expand_heads(t, lo) for t in dtt]
    ystate = [_dot(cmb[s], _bf(sts[s])) for s in seqs]
    xw = [_bf(xs[s] * (jnp.exp(csl_x[s] - cs_x[s]) * dt_x[s])) for s in seqs]
    full = [_dot_tn(bmb[s], xw[s]) for s in seqs]
    r2 = lax.broadcasted_iota(jnp.int32, (2 * SSD_STATE, SSD_INNER), 0) // SSD_STATE
    c2 = lax.broadcasted_iota(jnp.int32, (2 * SSD_STATE, SSD_INNER), 1) // (SSD_HG * SSD_P)
    ys, new_sts = [], []
    for s in seqs:
        ya = jnp.zeros((q_, 256), F32)
        for h in range(4):
            ya = ya + jnp.where(hl == h, ra[s][h * q_:(h + 1) * q_], 0.0)
        yb = jnp.where(hl[:, :LANE] == 0, rb[s][:q_], rb[s][q_:])
        ys.append(jnp.concatenate([ya, yb], axis=1) + jnp.exp(cs_x[s]) * ystate[s])
        new_sts.append(jnp.exp(csl_x[s]) * sts[s] + jnp.where(r2 == c2, full[s], 0.0))
    return ys, new_sts


def _ssd_kernel(*refs, reverse, nchunk):
    if reverse:
        (z_ref, act_ref, dta_ref, yf_ref, dx_ref, nw_ref, s0_ref, y_ref, sf_ref, st_scr) = refs
    else:
        (cur_ref, prev_ref, next_ref, cw_ref, cb_ref, dtb_ref, alog_ref, s0_ref,
         yf_ref, act_ref, dta_ref, sf_ref, st_scr) = refs
    i = pl.program_id(1)
    c = (nchunk - 1 - i) if reverse else i
    q_ = SSD_CHUNK

    @pl.when(i == 0)
    def _():
        st_scr[...] = s0_ref[...]

    ri = lax.broadcasted_iota(jnp.int32, (q_, q_), 0)
    ci = lax.broadcasted_iota(jnp.int32, (q_, q_), 1)
    tri = (ci >= ri) if reverse else (ci <= ri)
    tri_bf = jnp.where(tri, 1.0, 0.0).astype(BF16)

    seqs = range(st_scr.shape[0])
    if reverse:
        actb = [act_ref[s_, 0] for s_ in seqs]
        xs = [t[:, 0:384].astype(F32) for t in actb]
        dtt = [dta_ref[s_, 0][:, :LANE] for s_ in seqs]
        a = [dta_ref[s_, 0][:, LANE:] for s_ in seqs]
    else:
        has_prev = (c > 0).astype(F32)
        has_next = (c < nchunk - 1).astype(F32)
        actb, xs, dtt, a = [], [], [], []
        for s_ in seqs:
            prow = prev_ref[s_, 0][7:8, 384:1280] * has_prev
            nrow = next_ref[s_, 0][0:1, 384:1280] * has_next
            act, dt_, a_ = _ssd_prep(cur_ref[s_, 0], prow, nrow, cw_ref, cb_ref[...], dtb_ref[...], alog_ref[...])
            actb.append(_bf(act))
            act_ref[s_, 0] = actb[s_]
            dta_ref[s_, 0] = jnp.concatenate([dt_, a_], axis=1)
            xs.append(act[:, 0:384])
            dtt.append(dt_)
            a.append(a_)
    ys, sts = _ssd_scans(xs, [t[:, 384:640] for t in actb], [t[:, 640:896] for t in actb], dtt, a,
                         [st_scr[s_] for s_ in seqs], tri, tri_bf, reverse)
    for s_ in seqs:
        st_scr[s_] = sts[s_]

    @pl.when(i == nchunk - 1)
    def _():
        for s_ in seqs:
            sf_ref[s_] = sts[s_]

    for s_ in seqs:
        if not reverse:
            yf_ref[s_, 0] = ys[s_]
        else:
            yt = yf_ref[s_, 0] + ys[s_] + dx_ref[...] * xs[s_]
            yz = yt * _silu(z_ref[s_, 0])
            l384 = lax.broadcasted_iota(jnp.int32, (1, SSD_INNER), 1)
            g0 = l384 < (SSD_INNER // SSD_GROUPS)
            sq = yz * yz
            m0 = jnp.sum(jnp.where(g0, sq, 0.0), axis=-1, keepdims=True)
            m1 = jnp.sum(jnp.where(g0, 0.0, sq), axis=-1, keepdims=True)
            msq = jnp.where(g0, m0, m1) * (1.0 / (SSD_INNER // SSD_GROUPS))
            y_ref[s_, 0] = yz * lax.rsqrt(msq + EPS) * nw_ref[...]


def _ssd_fwd_call(ps, cw, cb, dtb, alog, layer, s0):
    bsz, nchunk, _, _ = ps.shape
    ns = SEQ_PER_STEP
    cur_map = lambda b, i: (b, i, 0, 0)
    prev_map = lambda b, i: (b, jnp.maximum(i - 1, 0), SSD_CHUNK // 8 - 1, 0)
    next_map = lambda b, i: (b, jnp.minimum(i + 1, nchunk - 1), 0, 0)
    const2 = lambda b, i: (0, 0)
    chunk = lambda w: pl.BlockSpec((ns, 1, SSD_CHUNK, w), cur_map)
    st_spec = pl.BlockSpec((ns, 2 * SSD_STATE, SSD_INNER), lambda b, i: (b, 0, 0))
    return pl.pallas_call(
        functools.partial(_ssd_kernel, reverse=False, nchunk=nchunk),
        grid=(bsz // ns, nchunk),
        in_specs=[chunk(PS_W), pl.BlockSpec((ns, 1, 8, PS_W), prev_map), pl.BlockSpec((ns, 1, 8, PS_W), next_map),
                  _lspec((8, SSD_CONV_DIM), layer), _lspec((1, SSD_CONV_DIM), layer),
                  _lspec((1, LANE), layer), _lspec((1, LANE), layer), st_spec],
        out_specs=[chunk(SSD_INNER), chunk(SSD_CONV_DIM), chunk(2 * LANE), st_spec],
        out_shape=[jax.ShapeDtypeStruct((bsz, nchunk, SSD_CHUNK, SSD_INNER), F32),
                   jax.ShapeDtypeStruct((bsz, nchunk, SSD_CHUNK, SSD_CONV_DIM), BF16),
                   jax.ShapeDtypeStruct((bsz, nchunk, SSD_CHUNK, 2 * LANE), F32),
                   jax.ShapeDtypeStruct((bsz, 2 * SSD_STATE, SSD_INNER), F32)],
        scratch_shapes=[pltpu.VMEM((ns, 2 * SSD_STATE, SSD_INNER), F32)],
        compiler_params=_cp("arbitrary", "arbitrary"),
        name="ssd_fwd",
    )(ps, ps, ps, cw, cb, dtb, alog, s0)


def _ssd_bwd_call(ps, act, dta, y_f, dx, nw, layer, s0):
    bsz, nchunk, _, _ = ps.shape
    ns = SEQ_PER_STEP
    cur_map = lambda b, i: (b, nchunk - 1 - i, 0, 0)
    const2 = lambda b, i: (0, 0)
    chunk = lambda w: pl.BlockSpec((ns, 1, SSD_CHUNK, w), cur_map)
    st_spec = pl.BlockSpec((ns, 2 * SSD_STATE, SSD_INNER), lambda b, i: (b, 0, 0))
    return pl.pallas_call(
        functools.partial(_ssd_kernel, reverse=True, nchunk=nchunk),
        grid=(bsz // ns, nchunk),
        in_specs=[chunk(SSD_INNER), chunk(SSD_CONV_DIM), chunk(2 * LANE), chunk(SSD_INNER),
                  _lspec((1, SSD_INNER), layer), _lspec((1, SSD_INNER), layer), st_spec],
        out_specs=[chunk(SSD_INNER), st_spec],
        out_shape=[jax.ShapeDtypeStruct((bsz, nchunk, SSD_CHUNK, SSD_INNER), F32),
                   jax.ShapeDtypeStruct((bsz, 2 * SSD_STATE, SSD_INNER), F32)],
        scratch_shapes=[pltpu.VMEM((ns, 2 * SSD_STATE, SSD_INNER), F32)],
        compiler_params=_cp("arbitrary", "arbitrary"),
        name="ssd_bwd",
    )(ps, act, dta, y_f, dx, nw, s0)


HY_SLOT = LANE // 2


def _hy_filter_kernel(z_ref, w1_ref, b1_ref, w2_ref, b2_ref, w3_ref, fr_ref, dec_ref, h_ref):
    z = z_ref[...]
    half = z.shape[0]
    fr = fr_ref[...]
    h1 = jnp.sin(fr * (_dot_hp(z, w1_ref[...]) + b1_ref[...]))
    h2 = jnp.sin(fr * (_dot_hp(h1, w2_ref[...]) + b2_ref[...]))
    for slot in range(2):
        h = _dot_hp(h2, w3_ref[slot])
        rel = z[:, slot * HY_SLOT:slot * HY_SLOT + 1]
        h = h * jnp.exp(-2.0 * jnp.abs(rel) * dec_ref[...])
        for hf in range(HY_WIDTH // LANE):
            h_ref[hf, slot * half:(slot + 1) * half, :] = h[:, hf * LANE:(hf + 1) * LANE]


def _hy_features(L, tl):
    t = jnp.arange(L, dtype=F32)
    rel = (t - (L // 2)) / L
    bands = jnp.linspace(1e-4, HY_BANDS - 1, HY_BANDS, dtype=F32)
    ang = 2.0 * math.pi * rel[:, None] * bands
    z = jnp.concatenate([rel[:, None], jnp.cos(ang), -jnp.sin(ang)], axis=-1)
    z = jnp.pad(z, ((0, 0), (0, HY_SLOT - HY_EMB)))
    return z.reshape(L // tl, 2, tl // 2, HY_SLOT).transpose(0, 2, 1, 3).reshape(L // 2, LANE)


def _hy_filter_call(L, w1bd, b1t, w2bd, b2t, w3s, frt, dec, layer):
    tl = min(1024, L)
    z = _hy_features(L, tl)
    return pl.pallas_call(
        _hy_filter_kernel,
        grid=(L // tl,),
        in_specs=[pl.BlockSpec((tl // 2, LANE), lambda i: (i, 0)),
                  _lspec((LANE, LANE), layer), _lspec((1, LANE), layer),
                  _lspec((LANE, LANE), layer), _lspec((1, LANE), layer),
                  _lspec((2, LANE, HY_WIDTH), layer), _lspec((1, LANE), layer),
                  _lspec((1, HY_WIDTH), layer)],
        out_specs=pl.BlockSpec((2, tl, LANE), lambda i: (0, i, 0)),
        out_shape=jax.ShapeDtypeStruct((2, L, LANE), F32),
        compiler_params=_cp("arbitrary"),
        name="hy_filter",
    )(z, w1bd, b1t, w2bd, b2t, w3s, frt, dec)


@functools.lru_cache(maxsize=None)
def _fft_consts():
    n1 = FFT_N1
    half = n1 // 2
    k = np.arange(n1, dtype=np.float64)
    n2 = k[:, None, None]
    k1 = k[None, :, None]
    nn = np.arange(half, dtype=np.float64)[None, None, :]
    ang = -2.0 * np.pi * (n2 * k1 / FFT_N + nn * k1 / n1)
    mr, mi = np.cos(ang), np.sin(ang)
    m1 = np.concatenate([np.concatenate([mr, -mi], axis=2), np.concatenate([mi, mr], axis=2)], axis=1)
    ang2 = -2.0 * np.pi * np.outer(k, k) / n1
    fr, fi = np.cos(ang2), np.sin(ang2)
    f2 = np.block([[fr, -fi], [fi, fr]])
    f2c = np.block([[fr, fi], [-fi, fr]])
    no = (np.arange(half, dtype=np.float64) + n1 // 4)[None, :, None]
    kk = k[None, None, :]
    ang3 = 2.0 * np.pi * (n2 * kk / FFT_N + no * kk / n1)
    ir, ii = np.cos(ang3) / FFT_N, np.sin(ang3) / FFT_N
    m3 = np.concatenate([np.concatenate([ir, -ii], axis=2), np.concatenate([ii, ir], axis=2)], axis=1)
    return tuple(np.asarray(m, dtype=np.float32) for m in (m1, f2, f2c, m3))


FFT_NB = 16
FFT_HALF = FFT_N1 // 2


def _strided_rows(ref2d, start, n):
    return ref2d[pl.ds(start, n, stride=FFT_NB), :]


def _tok_flat(ref):
    return ref.reshape(ref.shape[0] * 2 * FFT_HALF * FFT_NB, LANE)


def _tok_row_slice(s, hf, t):
    return pl.ds((2 * s + hf) * FFT_HALF * FFT_NB + t, FFT_HALF, stride=FFT_NB)


def _tok_rows(ref, s, t):
    flat = _tok_flat(ref)
    return jnp.concatenate([flat[_tok_row_slice(s, hf, t), :] for hf in range(2)], axis=1)


def _stage_f32(dst, src_ref):
    v = src_ref[0].astype(F32).reshape(FFT_N1 * FFT_NB, HY_WIDTH)
    dst[0] = v[:, :LANE]
    dst[1] = v[:, LANE:]


def _staged_rows(scr, t):
    return jnp.concatenate([scr[hf, pl.ds(t, FFT_N1, stride=FFT_NB), :] for hf in range(2)], axis=1)


def _fft1_kernel(u_ref, m_ref, a_ref, *, nsig):
    for t in range(FFT_NB):
        rhs = _bf(jnp.concatenate([_tok_rows(u_ref, s, t) for s in range(nsig)], axis=0))
        a_ref[0, t] = _bf(_dot(m_ref[t], rhs))


def _fft1_call(u, m1, nsig):
    npair = u.shape[0] // nsig
    return pl.pallas_call(
        functools.partial(_fft1_kernel, nsig=nsig),
        grid=(npair, FFT_N1 // FFT_NB),
        in_specs=[pl.BlockSpec((nsig, 2, FFT_HALF, FFT_NB, LANE), lambda p, j: (p, 0, 0, j, 0)),
                  pl.BlockSpec((FFT_NB, 2 * FFT_N1, FFT_HALF * nsig), lambda p, j: (j, 0, 0))],
        out_specs=pl.BlockSpec((1, FFT_NB, 2 * FFT_N1, HY_WIDTH), lambda p, j: (p, j, 0, 0)),
        out_shape=jax.ShapeDtypeStruct((npair, FFT_N1, 2 * FFT_N1, HY_WIDTH), BF16),
        compiler_params=_cp("arbitrary", "arbitrary"),
        name="hy_fft1",
    )(u, m1)


def _fft2_kernel(*refs, spectrum):
    if spectrum:
        ar_ref, ai_ref, f_ref, o_ref, sr, si = refs
    else:
        ar_ref, ai_ref, f_ref, fc_ref, h_ref, o_ref, sr, si = refs
    _stage_f32(sr, ar_ref)
    _stage_f32(si, ai_ref)
    for t in range(FFT_NB):
        rhs = _bf(jnp.concatenate([_staged_rows(sr, t), _staged_rows(si, t)], axis=0))
        x = _dot(f_ref[...], rhs)
        if spectrum:
            o_ref[t] = x
        else:
            xr, xi = x[:FFT_N1], x[FFT_N1:]
            hr, hi = h_ref[t, :FFT_N1], h_ref[t, FFT_N1:]
            y = jnp.concatenate([xr * hr - xi * hi, xr * hi + xi * hr], axis=0)
            o_ref[0, t] = _bf(_dot(fc_ref[...], _bf(y)))


def _fft2_call(a, f2, f2c, hspec):
    npair = a.shape[0]
    av = a
    nj = FFT_N1 // FFT_NB
    c2 = lambda p, j: (0, 0)
    in_specs = [pl.BlockSpec((1, FFT_N1, FFT_NB, HY_WIDTH), lambda p, j: (p, 0, j, 0)),
                pl.BlockSpec((1, FFT_N1, FFT_NB, HY_WIDTH), lambda p, j: (p, 0, nj + j, 0)),
                pl.BlockSpec((2 * FFT_N1, 2 * FFT_N1), c2)]
    staging = [pltpu.VMEM((2, FFT_N1 * FFT_NB, LANE), F32), pltpu.VMEM((2, FFT_N1 * FFT_NB, LANE), F32)]
    if hspec is None:
        return pl.pallas_call(
            functools.partial(_fft2_kernel, spectrum=True),
            grid=(1, nj),
            in_specs=in_specs,
            out_specs=pl.BlockSpec((FFT_NB, 2 * FFT_N1, HY_WIDTH), lambda p, j: (j, 0, 0)),
            out_shape=jax.ShapeDtypeStruct((FFT_N1, 2 * FFT_N1, HY_WIDTH), F32),
            scratch_shapes=staging,
            compiler_params=_cp("arbitrary", "arbitrary"),
            name="hy_fft2_spec",
        )(av, av, f2)
    in_specs += [pl.BlockSpec((2 * FFT_N1, 2 * FFT_N1), c2),
                 pl.BlockSpec((FFT_NB, 2 * FFT_N1, HY_WIDTH), lambda p, j: (j, 0, 0))]
    return pl.pallas_call(
        functools.partial(_fft2_kernel, spectrum=False),
        grid=(npair, nj),
        in_specs=in_specs,
        out_specs=pl.BlockSpec((1, FFT_NB, 2 * FFT_N1, HY_WIDTH), lambda p, j: (p, j, 0, 0)),
        out_shape=jax.ShapeDtypeStruct((npair, FFT_N1, 2 * FFT_N1, HY_WIDTH), BF16),
        scratch_shapes=staging,
        compiler_params=_cp("arbitrary", "arbitrary"),
        name="hy_fft2",
    )(av, av, f2, f2c, hspec)


def _fft3_kernel(br_ref, bi_ref, m_ref, vg_ref, x0_ref, bias_ref, y_ref, sr, si):
    _stage_f32(sr, br_ref)
    _stage_f32(si, bi_ref)
    for t in range(FFT_NB):
        rhs = _bf(jnp.concatenate([_staged_rows(sr, t), _staged_rows(si, t)], axis=0))
        out = _dot(m_ref[t], rhs)
        for s in range(2):
            conv = out[s * FFT_HALF:(s + 1) * FFT_HALF]
            y = (conv + _tok_rows(vg_ref, s, t) * bias_ref[...]) * _tok_rows(x0_ref, s, t)
            for hf in range(2):
                _tok_flat(y_ref)[_tok_row_slice(s, hf, t), :] = y[:, hf * LANE:(hf + 1) * LANE]


def _fft3_call(bmat, m3, vg, x0, bias, layer):
    npair = bmat.shape[0]
    nj = FFT_N1 // FFT_NB
    tok_spec = pl.BlockSpec((2, 2, FFT_HALF, FFT_NB, LANE), lambda p, j: (p, 0, 0, j, 0))
    return pl.pallas_call(
        _fft3_kernel,
        grid=(npair, nj),
        in_specs=[pl.BlockSpec((1, FFT_N1, FFT_NB, HY_WIDTH), lambda p, j: (p, 0, j, 0)),
                  pl.BlockSpec((1, FFT_N1, FFT_NB, HY_WIDTH), lambda p, j: (p, 0, nj + j, 0)),
                  pl.BlockSpec((FFT_NB, FFT_N1, 2 * FFT_N1), lambda p, j: (j, 0, 0)),
                  tok_spec, tok_spec,
                  _lspec((1, HY_WIDTH), layer)],
        out_specs=tok_spec,
        out_shape=jax.ShapeDtypeStruct(vg.shape, F32),
        scratch_shapes=[pltpu.VMEM((2, FFT_N1 * FFT_NB, LANE), F32), pltpu.VMEM((2, FFT_N1 * FFT_NB, LANE), F32)],
        compiler_params=_cp("arbitrary", "arbitrary"),
        name="hy_fft3",
    )(bmat, bmat, m3, vg, x0, bias)


def _hy_direct_kernel(vg_ref, x0_ref, h_ref, bias_ref, y_ref, pad_scr, sh_scr):
    L = vg_ref.shape[2]
    u = jnp.concatenate([vg_ref[0, 0], vg_ref[0, 1]], axis=1)
    pad_scr[...] = jnp.zeros_like(pad_scr)
    pad_scr[L:2 * L, :] = u
    top = L + L // 2
    acc = jnp.zeros((L, HY_WIDTH), F32)
    for r in range(8):
        sh_scr[...] = pad_scr[r:r + 3 * L - 8, :]
        a_lo = -(-(top - L + 1 - r) // 8)
        a_hi = (top - r) // 8

        def body(a, acc, r=r):
            m = top - (a * 8 + r)
            tap = jnp.concatenate([h_ref[0, pl.ds(m, 1), :], h_ref[1, pl.ds(m, 1), :]], axis=1)
            return acc + tap * sh_scr[pl.ds(pl.multiple_of(a * 8, 8), L), :]

        acc = lax.fori_loop(a_lo, a_hi + 1, body, acc, unroll=4)
    y = (acc + u * bias_ref[...]) * jnp.concatenate([x0_ref[0, 0], x0_ref[0, 1]], axis=1)
    y_ref[0, 0] = y[:, :LANE]
    y_ref[0, 1] = y[:, LANE:]


def _hy_direct_call(vg, x0, h, bias, layer):
    bsz, _, L, _ = vg.shape
    tok = pl.BlockSpec((1, 2, L, LANE), lambda b: (b, 0, 0, 0))
    return pl.pallas_call(
        _hy_direct_kernel,
        grid=(bsz,),
        in_specs=[tok, tok, pl.BlockSpec((2, L, LANE), lambda b: (0, 0, 0)),
                  _lspec((1, HY_WIDTH), layer)],
        out_specs=tok,
        out_shape=jax.ShapeDtypeStruct(vg.shape, F32),
        scratch_shapes=[pltpu.VMEM((3 * L, HY_WIDTH), F32), pltpu.VMEM((3 * L - 8, HY_WIDTH), F32)],
        compiler_params=_cp("arbitrary"),
        name="hy_direct",
    )(vg, x0, h, bias)


def _out_kernel(x_ref, yg_ref, yh_ref, ys_ref, mod_ref, g2_ref, gf_ref, wo_ref, w13_ref, w2_ref,
                o_ref, *scr, final, col_major):
    m = mod_ref[0]
    if col_major:
        ys_scr, = scr
        for c in range(GRID_W):
            for k in range(SSD_INNER // LANE):
                ys_scr[k, pl.ds(c, ROWS_PER_TILE, stride=GRID_W), :] = ys_ref[0, c, :, k * LANE:(k + 1) * LANE]
        ys = jnp.concatenate([ys_scr[k] for k in range(SSD_INNER // LANE)], axis=1)
    else:
        ys = ys_ref[...]
    yh = jnp.concatenate([yh_ref[0, 0], yh_ref[0, 1]], axis=1)
    tm = x_ref.shape[0]
    nrg = ROW_GROUPS if tm % (8 * ROW_GROUPS) == 0 else 1
    rgs = [slice(r * (tm // nrg), (r + 1) * (tm // nrg)) for r in range(nrg)]
    ycat = [jnp.concatenate([_bf(yg_ref[r, :]), _bf(yh[r]), _bf(ys[r])], axis=1) for r in rgs]
    mix = [_dot(t, wo_ref[...]) for t in ycat]
    x1 = [x_ref[r, :] + m[2:3] * mx for r, mx in zip(rgs, mix)]
    ms = [jnp.mean(t * t, axis=-1, keepdims=True) for t in x1]
    h = [t * lax.rsqrt(s + EPS) * g2_ref[...] for t, s in zip(x1, ms)]
    h = [_bf(t * (1.0 + m[4:5]) + m[3:4]) for t in h]
    ab = [_dot(t, w13_ref[...]) for t in h]
    ffn = [_dot(_bf(_silu(t[:, :D_FF]) * t[:, D_FF:]), w2_ref[...]) for t in ab]
    for r, t, f in zip(rgs, x1, ffn):
        x2 = t + m[5:6] * f
        if final:
            ms2 = jnp.mean(x2 * x2, axis=-1, keepdims=True)
            x2 = x2 * lax.rsqrt(ms2 + EPS) * gf_ref[...]
        o_ref[r, :] = x2


def _out_call(x2, yg, yh, ys, mods, mod_row, g2, gf, wo, w13, w2, layer, final, col_major):
    t = x2.shape[0]
    seq = yh.shape[2]
    tm = min(TOK_TILE, seq)
    tiles = seq // tm
    c2 = lambda i: (0, 0)
    if col_major:
        assert tm == TOK_TILE
        ys_spec = pl.BlockSpec((1, GRID_W, ROWS_PER_TILE, SSD_INNER), lambda i: (i // tiles, 0, i % tiles, 0))
        extra = [pltpu.VMEM((SSD_INNER // LANE, tm, LANE), F32)]
    else:
        ys = ys.reshape(t, SSD_INNER)
        ys_spec = pl.BlockSpec((tm, SSD_INNER), lambda i: (i, 0))
        extra = []
    resident = lambda shape: pl.BlockSpec((None,) + shape, lambda i: (layer, 0, 0), pipeline_mode=pl.Buffered(1))
    return pl.pallas_call(
        functools.partial(_out_kernel, final=final, col_major=col_major),
        grid=(t // tm,),
        in_specs=[pl.BlockSpec((tm, D_MODEL), lambda i: (i, 0)),
                  pl.BlockSpec((tm, GLA_V), lambda i: (i, 0)),
                  pl.BlockSpec((1, 2, tm, LANE), lambda i: (i // tiles, 0, i % tiles, 0)),
                  ys_spec,
                  pl.BlockSpec((None, 1, 6, D_MODEL), lambda i: (layer, mod_row(i), 0, 0)),
                  _lspec((1, D_MODEL), layer),
                  pl.BlockSpec((1, D_MODEL), c2),
                  resident((D_MODEL, D_MODEL)),
                  resident((D_MODEL, 2 * D_FF)),
                  resident((D_FF, D_MODEL))],
        out_specs=pl.BlockSpec((tm, D_MODEL), lambda i: (i, 0)),
        out_shape=jax.ShapeDtypeStruct((t, D_MODEL), F32),
        scratch_shapes=extra,
        compiler_params=_cp("arbitrary"),
        name="out_ffn",
    )(x2, yg, yh, ys, mods, g2, gf.reshape(1, D_MODEL), wo, w13, w2)


W_IN_MOVES = ((0, 192, 0), (192, 384, 256), (384, 1184, 512), (1184, 3244, PG_W))
D_IN = 3244


def _pack_kernel(w_ref, o_ref):
    o_ref[...] = jnp.zeros(o_ref.shape, BF16)
    for src0, src1, dst in W_IN_MOVES:
        o_ref[0, :, dst:dst + src1 - src0] = w_ref[0, :, src0:src1].astype(BF16)


def _pack_w_in(w_in):
    rows = 128
    wtot = PG_W + PH_W + PS_W
    return pl.pallas_call(
        _pack_kernel,
        grid=(DEPTH, D_MODEL // rows),
        in_specs=[pl.BlockSpec((1, rows, D_IN), lambda l, i: (l, i, 0))],
        out_specs=pl.BlockSpec((1, rows, wtot), lambda l, i: (l, i, 0)),
        out_shape=jax.ShapeDtypeStruct((DEPTH, D_MODEL, wtot), BF16),
        compiler_params=_cp("arbitrary", "arbitrary"),
        name="pack_w_in",
    )(w_in)


def _pad_to(a, shape):
    return jnp.pad(a, [(0, s - d) for d, s in zip(a.shape, shape)])


def _mixers(pg, vg, x0, ps, lw, l, states):
    g_f0, g_b0, m_f0, m_b0 = states
    o_f, g_f = _gla_call(pg, None, lw['wgk_f'], lw['bgk_f'], None, l, g_f0, False)
    gla_y, g_b = _gla_call(pg, o_f, lw['wgk_b'], lw['bgk_b'], lw['gla_nw'], l, g_b0, True)

    L = vg.shape[2]
    h = _hy_filter_call(L, lw['hy_w1'], lw['hy_b1'], lw['hy_w2'], lw['hy_b2'], lw['hy_w3'], lw['hy_freq'],
                        lw['hy_decay'], l)
    if L == FFT_N // 2:
        m1, m1_real, f2, f2c, m3 = lw['fft']
        tok5 = lambda t: t.reshape(t.shape[0], 2, FFT_HALF, FFT_N1, LANE)
        hspec = _fft2_call(_fft1_call(tok5(h[None]), m1_real, 1), f2, f2c, None)
        vg5 = tok5(vg)
        bmat = _fft2_call(_fft1_call(vg5, m1, 2), f2, f2c, hspec)
        hy_y = _fft3_call(bmat, m3, vg5, tok5(x0), lw['hy_bias'], l).reshape(vg.shape)
    else:
        hy_y = _hy_direct_call(vg, x0, h, lw['hy_bias'], l)

    y_f, act, dta, m_f = _ssd_fwd_call(ps, lw['ssd_cw'], lw['ssd_cb'], lw['ssd_dtb'], lw['ssd_alog'], l, m_f0)
    ssd_y, m_b = _ssd_bwd_call(ps, act, dta, y_f, lw['ssd_dx'], lw['ssd_nw'], l, m_b0)
    return (gla_y, hy_y, ssd_y), (g_f, g_b, m_f, m_b)


def kernel(x, c, ctx, c_ctx, mod_w, mod_b, norm1_g, norm2_g, w_in, gla_gk_w_f, gla_gk_b_f, gla_gk_w_b, gla_gk_b_b, gla_norm_w, hy_short_w, hy_short_b, hy_w1, hy_b1, hy_w2, hy_b2, hy_w3, hy_freq, hy_decay, hy_bias, ssd_conv_w, ssd_conv_b, ssd_dt_bias_f, ssd_dt_bias_b, ssd_a_log_f, ssd_a_log_b, ssd_d, ssd_norm_w, w_out, ffn_w1, ffn_w3, ffn_w2, final_g):
    bsz, seq, _ = x.shape
    lc = ctx.shape[1]
    c8 = jnp.concatenate([c, c_ctx[None], jnp.zeros((8 - bsz - 1, D_MODEL), F32)], axis=0)
    mods_all = _mod_call(c8, mod_w, mod_b).reshape(DEPTH, 8, 6, D_MODEL)

    w_in_p = _pack_w_in(w_in)
    m1_c, f2_c, f2c_c, m3_c = _fft_consts()
    fft_mats = tuple(_bf(jnp.asarray(m)) for m in (m1_c, np.ascontiguousarray(m1_c[:, :, :FFT_HALF]), f2_c, f2c_c, m3_c))
    w_out_b, w13_b, w2_b = _bf(w_out), _bf(jnp.concatenate([ffn_w1, ffn_w3], axis=-1)), _bf(ffn_w2)

    xt = x.reshape(bsz * seq, D_MODEL)
    ct = ctx.reshape(bsz * lc, D_MODEL)
    tiles_per_seq = seq // TOK_TILE
    row_x = lambda i: i // tiles_per_seq
    row_c = lambda i: bsz

    zeros_states = (jnp.zeros((bsz, GLA_DV, 256), F32), jnp.zeros((bsz, GLA_DV, 256), F32),
                    jnp.zeros((bsz, 2 * SSD_STATE, SSD_INNER), F32),
                    jnp.zeros((bsz, 2 * SSD_STATE, SSD_INNER), F32))

    def gkw(w, off):
        return _bf(_pad_to(jnp.pad(w, ((0, 0), (off, 0), (0, 0))), (DEPTH, LANE, 256)))

    def bdiag(w):
        z = jnp.zeros_like(w)
        return jnp.concatenate([jnp.concatenate([w, z], axis=2), jnp.concatenate([z, w], axis=2)], axis=1)

    row = lambda v: v[:, None, :]
    lw = {
        'wgk_f': gkw(gla_gk_w_f, 0), 'wgk_b': gkw(gla_gk_w_b, GLA_LOWRANK),
        'bgk_f': _pad_to(row(gla_gk_b_f), (DEPTH, 1, 256)), 'bgk_b': _pad_to(row(gla_gk_b_b), (DEPTH, 1, 256)),
        'gla_nw': row(jnp.tile(gla_norm_w, (1, GLA_HEADS))),
        'hy_sw': _pad_to(hy_short_w, (DEPTH, 8, PH_W)), 'hy_sb': row(hy_short_b),
        'hy_w1': bdiag(_pad_to(hy_w1, (DEPTH, HY_SLOT, HY_SLOT))), 'hy_b1': row(jnp.tile(hy_b1, (1, 2))),
        'hy_w2': bdiag(hy_w2), 'hy_b2': row(jnp.tile(hy_b2, (1, 2))),
        'hy_w3': jnp.stack([jnp.pad(hy_w3, ((0, 0), (0, HY_SLOT), (0, 0))),
                            jnp.pad(hy_w3, ((0, 0), (HY_SLOT, 0), (0, 0)))], axis=1),
        'hy_freq': row(jnp.tile(hy_freq, (1, 2))),
        'hy_decay': row(hy_decay), 'hy_bias': row(hy_bias),
        'ssd_cw': _pad_to(ssd_conv_w, (DEPTH, 8, SSD_CONV_DIM)), 'ssd_cb': row(ssd_conv_b),
        'ssd_dtb': _pad_to(row(jnp.concatenate([ssd_dt_bias_f, ssd_dt_bias_b], axis=1)), (DEPTH, 1, LANE)),
        'ssd_alog': _pad_to(row(jnp.concatenate([ssd_a_log_f, ssd_a_log_b], axis=1)), (DEPTH, 1, LANE)),
        'ssd_dx': row(jnp.repeat(ssd_d, SSD_P, axis=1)), 'ssd_nw': row(ssd_norm_w),
        'fft': fft_mats,
    }
    g1, g2 = row(norm1_g), row(norm2_g)

    for l in range(DEPTH):
        in_args = (g1, w_in_p, l, lw['hy_sw'], lw['hy_sb'], bsz)
        yc, ctx_states = _mixers(*_in_call(ct, mods_all, row_c, *in_args, False), lw, l, zeros_states)
        yx, _ = _mixers(*_in_call(xt, mods_all, row_x, *in_args, True), lw, l, ctx_states)
        last = l == DEPTH - 1
        ffn = (w_out_b, w13_b, w2_b, l)
        xt = _out_call(xt, yx[0].reshape(-1, GLA_V), yx[1], yx[2],
                       mods_all, row_x, g2, final_g, *ffn, last, True)
        if not last:
            ct = _out_call(ct, yc[0].reshape(-1, GLA_V), yc[1], yc[2],
                           mods_all, row_c, g2, final_g, *ffn, False, False)
    return xt.reshape(bsz, seq, D_MODEL)
```

```python
import functools
import math

import numpy as np
import jax
import jax.numpy as jnp
from jax import lax
from jax.experimental import pallas as pl
from jax.experimental.pallas import tpu as pltpu

F32 = jnp.float32
BF16 = jnp.bfloat16

D_MODEL = 1024
DEPTH = 2
GRID_W = 64
EPS = 1e-6
GLA_V = 384
GLA_DV = 64
GLA_HEADS = 6
GLA_DK = 32
GLA_QK = 192
GLA_LOWRANK = 16
GLA_TAU = 16.0
GLA_CHUNK = 64
GLA_BLOCK = 256
HY_WIDTH = 256
HY_BANDS = 16
HY_EMB = 1 + 2 * HY_BANDS
HY_ORDER = 64
SSD_INNER = 384
SSD_HEADS = 6
SSD_GROUPS = 2
SSD_HG = 3
SSD_P = 64
SSD_STATE = 128
SSD_CONV_DIM = SSD_INNER + 2 * SSD_GROUPS * SSD_STATE
SSD_CHUNK = 128
D_FF = 2816
PG_W = 1408
PH_W = 768
PS_W = 1408
LANE = 128
FFT_N1 = 128
FFT_N = FFT_N1 * FFT_N1

VMEM_LIMIT = 56 * 1024 * 1024


def _cp(*sem):
    return pltpu.CompilerParams(dimension_semantics=sem, vmem_limit_bytes=VMEM_LIMIT)


def _bf(x):
    return x.astype(BF16)


def _lspec(shape, layer):
    zeros = (0,) * len(shape)
    return pl.BlockSpec((None,) + tuple(shape), lambda *_: (layer,) + zeros)


def _dot(a, b):
    return jnp.dot(a, b, preferred_element_type=F32)


def _dot_nt(a, b):
    return lax.dot_general(a, b, (((1,), (1,)), ((), ())), preferred_element_type=F32)


def _dot_tn(a, b):
    return lax.dot_general(a, b, (((0,), (0,)), ((), ())), preferred_element_type=F32)


def _split3(x):
    hi = _bf(x)
    r1 = x - hi.astype(F32)
    mid = _bf(r1)
    lo = _bf(r1 - mid.astype(F32))
    return hi, mid, lo


def _dot_sel(sel_bf, x):
    hi, mid, lo = _split3(x)
    return _dot(sel_bf, hi) + _dot(sel_bf, mid) + _dot(sel_bf, lo)


def _dot_sel2(sel_bf, x):
    hi = _bf(x)
    lo = _bf(x - hi.astype(F32))
    return _dot(sel_bf, hi) + _dot(sel_bf, lo)


def _dot_hp(a, b):
    ah = _bf(a)
    al = _bf(a - ah.astype(F32))
    bh = _bf(b)
    bl = _bf(b - bh.astype(F32))
    return _dot(ah, bh) + _dot(ah, bl) + _dot(al, bh)


def _silu(x):
    return x * jax.nn.sigmoid(x)


def _softplus(x):
    return jnp.maximum(x, 0.0) + jnp.log(1.0 + jnp.exp(-jnp.abs(x)))


def _log_sigmoid(x):
    return jnp.minimum(x, 0.0) - jnp.log(1.0 + jnp.exp(-jnp.abs(x)))


def _mod_kernel(c_ref, w_ref, b_ref, o_ref):
    act = _silu(c_ref[...])
    o_ref[0] = _dot(_bf(act), _bf(w_ref[0])) + b_ref[0]


def _mod_call(c8, mod_w, mod_b):
    nt = 1536
    n = mod_w.shape[-1]
    return pl.pallas_call(
        _mod_kernel,
        grid=(DEPTH, n // nt),
        in_specs=[pl.BlockSpec((8, D_MODEL), lambda l, j: (0, 0)),
                  pl.BlockSpec((1, D_MODEL, nt), lambda l, j: (l, 0, j)),
                  pl.BlockSpec((1, 1, nt), lambda l, j: (l, 0, j))],
        out_specs=pl.BlockSpec((1, 8, nt), lambda l, j: (l, 0, j)),
        out_shape=jax.ShapeDtypeStruct((DEPTH, 8, n), F32),
        compiler_params=_cp("arbitrary", "arbitrary"),
        name="adaln_mod",
    )(c8, mod_w, mod_b.reshape(DEPTH, 1, n))


TOK_TILE = 512
ROW_GROUPS = 2
ROWS_PER_TILE = TOK_TILE // GRID_W


def _in_kernel(x_ref, xp_ref, xn_ref, mod_ref, g_ref, w_ref, hw_ref, hb_ref, og_ref, vg_ref, x0_ref, os_ref, *scr,
               col_major, tiles):
    m = mod_ref[0]
    i = pl.program_id(0)

    def modulated(t):
        ms = jnp.mean(t * t, axis=-1, keepdims=True)
        return _bf((t * lax.rsqrt(ms + EPS) * g_ref[...]) * (1.0 + m[1:2]) + m[0:1])

    tm = x_ref.shape[0]
    rgs = [slice(r * (tm // ROW_GROUPS), (r + 1) * (tm // ROW_GROUPS)) for r in range(ROW_GROUPS)]
    h = [modulated(x_ref[r, :]) for r in rgs]
    p = [_dot(t, w_ref[...]) for t in h]
    for r, t in zip(rgs, p):
        og_ref[r, :] = _bf(t[:, 0:PG_W])
    w_hy = w_ref[:, PG_W:PG_W + PH_W]
    ph = jnp.concatenate([t[:, PG_W:PG_W + PH_W] for t in p], axis=0)
    ps = jnp.concatenate([t[:, PG_W + PH_W:] for t in p], axis=0)

    halo = _dot(modulated(jnp.concatenate([xp_ref[...], xn_ref[...]], axis=0)), w_hy)
    prow = halo[7:8] * (i % tiles != 0).astype(F32)
    nrow = halo[8:9] * (i % tiles != tiles - 1).astype(F32)
    ridx = lax.broadcasted_iota(jnp.int32, (tm, 1), 0)
    pm = jnp.where(ridx == 0, prow, pltpu.roll(ph, 1, 0))
    pn = jnp.where(ridx == tm - 1, nrow, pltpu.roll(ph, tm - 1, 0))
    u = hw_ref[0:1] * pm + hw_ref[1:2] * ph + hw_ref[2:3] * pn + hb_ref[...]
    vg = u[:, 2 * HY_WIDTH:] * u[:, HY_WIDTH:2 * HY_WIDTH]
    for hf in range(HY_WIDTH // LANE):
        x0_ref[0, hf] = u[:, hf * LANE:(hf + 1) * LANE]
        vg_ref[0, hf] = vg[:, hf * LANE:(hf + 1) * LANE]

    if col_major:
        ps_scr, = scr
        for k in range(PS_W // LANE):
            for r in range(ROWS_PER_TILE):
                ps_scr[k, pl.ds(r, GRID_W, stride=ROWS_PER_TILE), :] = ps[r * GRID_W:(r + 1) * GRID_W,
                                                                         k * LANE:(k + 1) * LANE]
        for k in range(PS_W // LANE):
            os_ref[0, :, :, k * LANE:(k + 1) * LANE] = ps_scr[k].reshape(GRID_W, ROWS_PER_TILE, LANE)
    else:
        os_ref[...] = ps


def _in_call(x2, mods, mod_row, g, w_all, layer, hy_w8, hy_b, bsz, col_major):
    t = x2.shape[0]
    seq = t // bsz
    tm = min(TOK_TILE, seq)
    tiles = seq // tm
    wtot = PG_W + PH_W + PS_W
    hb = tm // 8
    if col_major:
        assert seq == GRID_W * SSD_CHUNK and ROWS_PER_TILE == 8 and tm == TOK_TILE
        os_spec = pl.BlockSpec((1, GRID_W, ROWS_PER_TILE, PS_W), lambda i: (i // tiles, 0, i % tiles, 0))
        os_shape = jax.ShapeDtypeStruct((bsz, GRID_W, SSD_CHUNK, PS_W), F32)
        scratch = [pltpu.VMEM((PS_W // LANE, tm, LANE), F32)]
    else:
        os_spec = pl.BlockSpec((tm, PS_W), lambda i: (i, 0))
        os_shape = jax.ShapeDtypeStruct((t, PS_W), F32)
        scratch = []
    nblk8 = t // 8
    tok_spec = pl.BlockSpec((1, 2, tm, LANE), lambda i: (i // tiles, 0, i % tiles, 0))
    tok_shape = jax.ShapeDtypeStruct((bsz, 2, seq, LANE), F32)
    pg, vg, x0, ps = pl.pallas_call(
        functools.partial(_in_kernel, col_major=col_major, tiles=tiles),
        grid=(t // tm,),
        in_specs=[pl.BlockSpec((tm, D_MODEL), lambda i: (i, 0)),
                  pl.BlockSpec((8, D_MODEL), lambda i: (jnp.maximum(i * hb - 1, 0), 0)),
                  pl.BlockSpec((8, D_MODEL), lambda i: (jnp.minimum((i + 1) * hb, nblk8 - 1), 0)),
                  pl.BlockSpec((None, 1, 6, D_MODEL), lambda i: (layer, mod_row(i), 0, 0)),
                  _lspec((1, D_MODEL), layer),
                  pl.BlockSpec((None, D_MODEL, wtot), lambda i: (layer, 0, 0), pipeline_mode=pl.Buffered(1)),
                  _lspec((8, PH_W), layer),
                  _lspec((1, PH_W), layer)],
        out_specs=[pl.BlockSpec((tm, PG_W), lambda i: (i, 0)), tok_spec, tok_spec, os_spec],
        out_shape=[jax.ShapeDtypeStruct((t, PG_W), BF16), tok_shape, tok_shape, os_shape],
        scratch_shapes=scratch,
        compiler_params=_cp("arbitrary"),
        name="in_proj",
    )(x2, x2, x2, mods, g, w_all, hy_w8, hy_b)
    return (pg.reshape(bsz, seq, PG_W), vg, x0, ps.reshape(bsz, seq // SSD_CHUNK, SSD_CHUNK, PS_W))


def _gla_kernel(*refs, reverse, nblk):
    if reverse:
        (p_ref, of_ref, wgk_ref, bgk_ref, nw_ref, s0_ref, y_ref, sf_ref, st_scr) = refs
    else:
        (p_ref, wgk_ref, bgk_ref, s0_ref, of_ref, sf_ref, st_scr) = refs
    i = pl.program_id(1)
    tb = GLA_BLOCK

    @pl.when(i == 0)
    def _():
        st_scr[...] = s0_ref[...]

    ri = lax.broadcasted_iota(jnp.int32, (tb, tb), 0)
    ci = lax.broadcasted_iota(jnp.int32, (tb, tb), 1)
    same = (ri // GLA_CHUNK) == (ci // GLA_CHUNK)
    tri = same & ((ci >= ri) if reverse else (ci <= ri))
    tri_bf = jnp.where(tri, 1.0, 0.0).astype(BF16)
    lane = lax.broadcasted_iota(jnp.int32, (1, 256), 1)
    hms = [(lane // GLA_DK) == h for h in range(GLA_HEADS)]

    nseq = p_ref.shape[0]
    seqs = range(nseq)
    ps = [p_ref[s_] for s_ in seqs]
    os_, sts = _gla_blocks(ps, [st_scr[s_] for s_ in seqs], wgk_ref[...], bgk_ref[...], tri, tri_bf, lane, hms,
                           reverse)
    for s_ in seqs:
        st_scr[s_] = sts[s_]

    @pl.when(i == nblk - 1)
    def _():
        for s_ in seqs:
            sf_ref[s_] = sts[s_]

    if not reverse:
        for s_ in seqs:
            of_ref[s_] = os_[s_]
    else:
        r2 = lax.broadcasted_iota(jnp.int32, (GLA_V, GLA_V), 0) // GLA_DV
        c2 = lax.broadcasted_iota(jnp.int32, (GLA_V, GLA_V), 1) // GLA_DV
        ind = jnp.where(r2 == c2, 1.0, 0.0).astype(BF16)
        ot = [of_ref[s_] + os_[s_] for s_ in seqs]
        ms = [_dot(_bf(t * t), ind) * (1.0 / GLA_DV) for t in ot]
        for s_ in seqs:
            g = ps[s_][:, 896:1280].astype(F32)
            y_ref[s_] = ot[s_] * lax.rsqrt(ms[s_] + EPS) * nw_ref[...] * _silu(g)


def _gla_blocks(ps, sts, wgk, bgk, tri, tri_bf, lane, hms, reverse):
    tb = GLA_BLOCK
    nch = tb // GLA_CHUNK
    seqs = range(len(ps))
    k = [p[:, 256:512].astype(F32) for p in ps]
    pre = [_dot(p[:, 1280:1408], wgk) + bgk for p in ps]
    la = [_log_sigmoid(t) * (1.0 / GLA_TAU) for t in pre]
    b = [_dot_sel2(tri_bf, t) for t in la]
    qd = [_bf(ps[s][:, 0:256].astype(F32) * ((GLA_DK ** -0.5) * jnp.exp(b[s]))) for s in seqs]
    ki = [_bf(k[s] * jnp.exp(-b[s])) for s in seqs]
    vb = [p[:, 512:896] for p in ps]

    lhs = [jnp.concatenate([jnp.where(hm, t, jnp.zeros_like(t)) for hm in hms], axis=0) for t in qd]
    sc = [_dot_nt(lhs[s], ki[s]) for s in seqs]
    pm = [jnp.concatenate([_bf(jnp.where(tri, t[h * tb:(h + 1) * tb], 0.0)) for h in range(GLA_HEADS)], axis=0)
          for t in sc]
    ra = [_dot(pm[s][:4 * tb], vb[s][:, 0:256]) for s in seqs]
    rb = [_dot(pm[s][4 * tb:], vb[s][:, 256:384]) for s in seqs]
    hl = lane // GLA_DV
    o_intra = []
    for s in seqs:
        oa = jnp.zeros((tb, 256), F32)
        for h in range(4):
            oa = oa + jnp.where(hl == h, ra[s][h * tb:(h + 1) * tb], 0.0)
        ob = jnp.where(hl[:, :LANE] == 0, rb[s][:tb], rb[s][tb:])
        o_intra.append(jnp.concatenate([oa, ob], axis=1))

    sts = list(sts)
    outs = [[None] * nch for _ in seqs]
    order = range(nch - 1, -1, -1) if reverse else range(nch)
    for c in order:
        r0 = c * GLA_CHUNK
        rows = slice(r0, r0 + GLA_CHUNK)
        edge = r0 if reverse else r0 + GLA_CHUNK - 1
        bl = [t[edge:edge + 1] for t in b]
        kd = [_bf(k[s][rows] * jnp.exp(bl[s] - b[s][rows])) for s in seqs]
        stbd = [_bf(jnp.concatenate([jnp.where(hm, t, 0.0) for hm in hms], axis=0)) for t in sts]
        for s in seqs:
            outs[s][c] = _dot_nt(qd[s][rows], stbd[s])
        full = [_dot_tn(vb[s][rows], kd[s]) for s in seqs]
        for s in seqs:
            ds = jnp.zeros((GLA_DV, 256), F32)
            for h in range(GLA_HEADS):
                ds = ds + jnp.where(hms[h], full[s][h * GLA_DV:(h + 1) * GLA_DV], 0.0)
            sts[s] = jnp.exp(bl[s]) * sts[s] + ds
    return [o_intra[s] + jnp.concatenate(outs[s], axis=0) for s in seqs], sts


SEQ_PER_STEP = 4
GLA_SEQ_PER_STEP = 4


def _gla_call(pg, o_f, wgk, bgk, nw, layer, s0, reverse):
    bsz, L, _ = pg.shape
    nblk = L // GLA_BLOCK
    ns = GLA_SEQ_PER_STEP
    blk = (lambda b, i: (b, nblk - 1 - i, 0)) if reverse else (lambda b, i: (b, i, 0))
    st_spec = pl.BlockSpec((ns, GLA_DV, 256), lambda b, i: (b, 0, 0))
    p_spec = pl.BlockSpec((ns, GLA_BLOCK, PG_W), blk)
    o_spec = pl.BlockSpec((ns, GLA_BLOCK, GLA_V), blk)
    if reverse:
        in_specs = [p_spec, o_spec, _lspec((LANE, 256), layer), _lspec((1, 256), layer),
                    _lspec((1, GLA_V), layer), st_spec]
        args = (pg, o_f, wgk, bgk, nw, s0)
    else:
        in_specs = [p_spec, _lspec((LANE, 256), layer), _lspec((1, 256), layer), st_spec]
        args = (pg, wgk, bgk, s0)
    return pl.pallas_call(
        functools.partial(_gla_kernel, reverse=reverse, nblk=nblk),
        grid=(bsz // ns, nblk),
        in_specs=in_specs,
        out_specs=[o_spec, st_spec],
        out_shape=[jax.ShapeDtypeStruct((bsz, L, GLA_V), F32),
                   jax.ShapeDtypeStruct((bsz, GLA_DV, 256), F32)],
        scratch_shapes=[pltpu.VMEM((ns, GLA_DV, 256), F32)],
        compiler_params=_cp("arbitrary", "arbitrary"),
        name="gla_bwd" if reverse else "gla_fwd",
    )(*args)


def _expand_heads(t, lo):
    r = t.shape[0]
    lane = lax.broadcasted_iota(jnp.int32, (1, LANE), 1)
    tiles = []
    for j in range(SSD_HEADS // 2):
        a = jnp.broadcast_to(t[:, lo + 2 * j:lo + 2 * j + 1], (r, LANE))
        b = jnp.broadcast_to(t[:, lo + 2 * j + 1:lo + 2 * j + 2], (r, LANE))
        tiles.append(jnp.where(lane < SSD_P, a, b))
    return jnp.concatenate(tiles, axis=1)


def _ssd_prep(cur, prow, nrow, cw, cb, dtb, alog):
    q_ = SSD_CHUNK
    x = cur[:, 384:1280]
    ridx = lax.broadcasted_iota(jnp.int32, (q_, 1), 0)
    xm = jnp.where(ridx == 0, prow, pltpu.roll(x, 1, 0))
    xp = jnp.where(ridx == q_ - 1, nrow, pltpu.roll(x, q_ - 1, 0))
    act = _silu(cw[0:1] * xm + cw[1:2] * x + cw[2:3] * xp + cb)
    dtt = _softplus(cur[:, 1280:1408] + dtb)
    a = -jnp.exp(alog) * dtt
    return act, dtt, a


def _ssd_scans(xs, bmb, cmb, dtt, a, sts, tri, tri_bf, reverse):
    q_ = SSD_CHUNK
    lo = SSD_HEADS if reverse else 0
    seqs = range(len(xs))
    cs = [_dot_sel(tri_bf, t) for t in a]
    cst = [t.T for t in cs]
    dtT = [t.T for t in dtt]
    edge = 0 if reverse else q_ - 1
    cs_last = [t[edge:edge + 1] for t in cs]
    grp = lambda t, g: t[:, g * SSD_STATE:(g + 1) * SSD_STATE]
    cbs = [[_dot_nt(grp(cmb[s], g), grp(bmb[s], g)) for g in range(SSD_GROUPS)] for s in seqs]
    ms = [[] for _ in seqs]
    for h in range(SSD_HEADS):
        l = lo + h
        for s in seqs:
            seg = cs[s][:, l:l + 1] - cst[s][l:l + 1, :]
            dec = jnp.exp(jnp.where(tri, seg, -jnp.inf))
            ms[s].append(_bf(cbs[s][h // SSD_HG] * dec * dtT[s][l:l + 1, :]))
    mst = [jnp.concatenate(t, axis=0) for t in ms]
    xsb = [_bf(t) for t in xs]
    ra = [_dot(mst[s][:4 * q_], xsb[s][:, 0:256]) for s in seqs]
    rb = [_dot(mst[s][4 * q_:], xsb[s][:, 256:384]) for s in seqs]
    lane = lax.broadcasted_iota(jnp.int32, (1, 256), 1)
    hl = lane // SSD_P
    cs_x = [_expand_heads(t, lo) for t in cs]
    csl_x = [_expand_heads(t, lo) for t in cs_last]
    dt_x = [_expand_heads(t, lo) for t in dtt]
    ystate = [_dot(cmb[s], _bf(sts[s])) for s in seqs]
    xw = [_bf(xs[s] * (jnp.exp(csl_x[s] - cs_x[s]) * dt_x[s])) for s in seqs]
    full = [_dot_tn(bmb[s], xw[s]) for s in seqs]
    r2 = lax.broadcasted_iota(jnp.int32, (2 * SSD_STATE, SSD_INNER), 0) // SSD_STATE
    c2 = lax.broadcasted_iota(jnp.int32, (2 * SSD_STATE, SSD_INNER), 1) // (SSD_HG * SSD_P)
    ys, new_sts = [], []
    for s in seqs:
        ya = jnp.zeros((q_, 256), F32)
        for h in range(4):
            ya = ya + jnp.where(hl == h, ra[s][h * q_:(h + 1) * q_], 0.0)
        yb = jnp.where(hl[:, :LANE] == 0, rb[s][:q_], rb[s][q_:])
        ys.append(jnp.concatenate([ya, yb], axis=1) + jnp.exp(cs_x[s]) * ystate[s])
        new_sts.append(jnp.exp(csl_x[s]) * sts[s] + jnp.where(r2 == c2, full[s], 0.0))
    return ys, new_sts


def _ssd_kernel(*refs, reverse, nchunk):
    if reverse:
        (z_ref, act_ref, dta_ref, yf_ref, dx_ref, nw_ref, s0_ref, y_ref, sf_ref, st_scr) = refs
    else:
        (cur_ref, prev_ref, next_ref, cw_ref, cb_ref, dtb_ref, alog_ref, s0_ref,
         yf_ref, act_ref, dta_ref, sf_ref, st_scr) = refs
    i = pl.program_id(1)
    c = (nchunk - 1 - i) if reverse else i
    q_ = SSD_CHUNK

    @pl.when(i == 0)
    def _():
        st_scr[...] = s0_ref[...]

    ri = lax.broadcasted_iota(jnp.int32, (q_, q_), 0)
    ci = lax.broadcasted_iota(jnp.int32, (q_, q_), 1)
    tri = (ci >= ri) if reverse else (ci <= ri)
    tri_bf = jnp.where(tri, 1.0, 0.0).astype(BF16)

    seqs = range(st_scr.shape[0])
    if reverse:
        actb = [act_ref[s_, 0] for s_ in seqs]
        xs = [t[:, 0:384].astype(F32) for t in actb]
        dtt = [dta_ref[s_, 0][:, :LANE] for s_ in seqs]
        a = [dta_ref[s_, 0][:, LANE:] for s_ in seqs]
    else:
        has_prev = (c > 0).astype(F32)
        has_next = (c < nchunk - 1).astype(F32)
        actb, xs, dtt, a = [], [], [], []
        for s_ in seqs:
            prow = prev_ref[s_, 0][7:8, 384:1280] * has_prev
            nrow = next_ref[s_, 0][0:1, 384:1280] * has_next
            act, dt_, a_ = _ssd_prep(cur_ref[s_, 0], prow, nrow, cw_ref, cb_ref[...], dtb_ref[...], alog_ref[...])
            actb.append(_bf(act))
            act_ref[s_, 0] = actb[s_]
            dta_ref[s_, 0] = jnp.concatenate([dt_, a_], axis=1)
            xs.append(act[:, 0:384])
            dtt.append(dt_)
            a.append(a_)
    ys, sts = _ssd_scans(xs, [t[:, 384:640] for t in actb], [t[:, 640:896] for t in actb], dtt, a,
                         [st_scr[s_] for s_ in seqs], tri, tri_bf, reverse)
    for s_ in seqs:
        st_scr[s_] = sts[s_]

    @pl.when(i == nchunk - 1)
    def _():
        for s_ in seqs:
            sf_ref[s_] = sts[s_]

    for s_ in seqs:
        if not reverse:
            yf_ref[s_, 0] = ys[s_]
        else:
            yt = yf_ref[s_, 0] + ys[s_] + dx_ref[...] * xs[s_]
            yz = yt * _silu(z_ref[s_, 0])
            l384 = lax.broadcasted_iota(jnp.int32, (1, SSD_INNER), 1)
            g0 = l384 < (SSD_INNER // SSD_GROUPS)
            sq = yz * yz
            m0 = jnp.sum(jnp.where(g0, sq, 0.0), axis=-1, keepdims=True)
            m1 = jnp.sum(jnp.where(g0, 0.0, sq), axis=-1, keepdims=True)
            msq = jnp.where(g0, m0, m1) * (1.0 / (SSD_INNER // SSD_GROUPS))
            y_ref[s_, 0] = yz * lax.rsqrt(msq + EPS) * nw_ref[...]


def _ssd_fwd_call(ps, cw, cb, dtb, alog, layer, s0):
    bsz, nchunk, _, _ = ps.shape
    ns = SEQ_PER_STEP
    cur_map = lambda b, i: (b, i, 0, 0)
    prev_map = lambda b, i: (b, jnp.maximum(i - 1, 0), SSD_CHUNK // 8 - 1, 0)
    next_map = lambda b, i: (b, jnp.minimum(i + 1, nchunk - 1), 0, 0)
    const2 = lambda b, i: (0, 0)
    chunk = lambda w: pl.BlockSpec((ns, 1, SSD_CHUNK, w), cur_map)
    st_spec = pl.BlockSpec((ns, 2 * SSD_STATE, SSD_INNER), lambda b, i: (b, 0, 0))
    return pl.pallas_call(
        functools.partial(_ssd_kernel, reverse=False, nchunk=nchunk),
        grid=(bsz // ns, nchunk),
        in_specs=[chunk(PS_W), pl.BlockSpec((ns, 1, 8, PS_W), prev_map), pl.BlockSpec((ns, 1, 8, PS_W), next_map),
                  _lspec((8, SSD_CONV_DIM), layer), _lspec((1, SSD_CONV_DIM), layer),
                  _lspec((1, LANE), layer), _lspec((1, LANE), layer), st_spec],
        out_specs=[chunk(SSD_INNER), chunk(SSD_CONV_DIM), chunk(2 * LANE), st_spec],
        out_shape=[jax.ShapeDtypeStruct((bsz, nchunk, SSD_CHUNK, SSD_INNER), F32),
                   jax.ShapeDtypeStruct((bsz, nchunk, SSD_CHUNK, SSD_CONV_DIM), BF16),
                   jax.ShapeDtypeStruct((bsz, nchunk, SSD_CHUNK, 2 * LANE), F32),
                   jax.ShapeDtypeStruct((bsz, 2 * SSD_STATE, SSD_INNER), F32)],
        scratch_shapes=[pltpu.VMEM((ns, 2 * SSD_STATE, SSD_INNER), F32)],
        compiler_params=_cp("arbitrary", "arbitrary"),
        name="ssd_fwd",
    )(ps, ps, ps, cw, cb, dtb, alog, s0)


def _ssd_bwd_call(ps, act, dta, y_f, dx, nw, layer, s0):
    bsz, nchunk, _, _ = ps.shape
    ns = SEQ_PER_STEP
    cur_map = lambda b, i: (b, nchunk - 1 - i, 0, 0)
    const2 = lambda b, i: (0, 0)
    chunk = lambda w: pl.BlockSpec((ns, 1, SSD_CHUNK, w), cur_map)
    st_spec = pl.BlockSpec((ns, 2 * SSD_STATE, SSD_INNER), lambda b, i: (b, 0, 0))
    return pl.pallas_call(
        functools.partial(_ssd_kernel, reverse=True, nchunk=nchunk),
        grid=(bsz // ns, nchunk),
        in_specs=[chunk(SSD_INNER), chunk(SSD_CONV_DIM), chunk(2 * LANE), chunk(SSD_INNER),
                  _lspec((1, SSD_INNER), layer), _lspec((1, SSD_INNER), layer), st_spec],
        out_specs=[chunk(SSD_INNER), st_spec],
        out_shape=[jax.ShapeDtypeStruct((bsz, nchunk, SSD_CHUNK, SSD_INNER), F32),
                   jax.ShapeDtypeStruct((bsz, 2 * SSD_STATE, SSD_INNER), F32)],
        scratch_shapes=[pltpu.VMEM((ns, 2 * SSD_STATE, SSD_INNER), F32)],
        compiler_params=_cp("arbitrary", "arbitrary"),
        name="ssd_bwd",
    )(ps, act, dta, y_f, dx, nw, s0)


HY_SLOT = LANE // 2


def _hy_filter_kernel(z_ref, w1_ref, b1_ref, w2_ref, b2_ref, w3_ref, fr_ref, dec_ref, h_ref):
    z = z_ref[...]
    half = z.shape[0]
    fr = fr_ref[...]
    h1 = jnp.sin(fr * (_dot_hp(z, w1_ref[...]) + b1_ref[...]))
    h2 = jnp.sin(fr * (_dot_hp(h1, w2_ref[...]) + b2_ref[...]))
    for slot in range(2):
        h = _dot_hp(h2, w3_ref[slot])
        rel = z[:, slot * HY_SLOT:slot * HY_SLOT + 1]
        h = h * jnp.exp(-2.0 * jnp.abs(rel) * dec_ref[...])
        for hf in range(HY_WIDTH // LANE):
            h_ref[hf, slot * half:(slot + 1) * half, :] = h[:, hf * LANE:(hf + 1) * LANE]


def _hy_features(L, tl):
    t = jnp.arange(L, dtype=F32)
    rel = (t - (L // 2)) / L
    bands = jnp.linspace(1e-4, HY_BANDS - 1, HY_BANDS, dtype=F32)
    ang = 2.0 * math.pi * rel[:, None] * bands
    z = jnp.concatenate([rel[:, None], jnp.cos(ang), -jnp.sin(ang)], axis=-1)
    z = jnp.pad(z, ((0, 0), (0, HY_SLOT - HY_EMB)))
    return z.reshape(L // tl, 2, tl // 2, HY_SLOT).transpose(0, 2, 1, 3).reshape(L // 2, LANE)


def _hy_filter_call(L, w1bd, b1t, w2bd, b2t, w3s, frt, dec, layer):
    tl = min(1024, L)
    z = _hy_features(L, tl)
    return pl.pallas_call(
        _hy_filter_kernel,
        grid=(L // tl,),
        in_specs=[pl.BlockSpec((tl // 2, LANE), lambda i: (i, 0)),
                  _lspec((LANE, LANE), layer), _lspec((1, LANE), layer),
                  _lspec((LANE, LANE), layer), _lspec((1, LANE), layer),
                  _lspec((2, LANE, HY_WIDTH), layer), _lspec((1, LANE), layer),
                  _lspec((1, HY_WIDTH), layer)],
        out_specs=pl.BlockSpec((2, tl, LANE), lambda i: (0, i, 0)),
        out_shape=jax.ShapeDtypeStruct((2, L, LANE), F32),
        compiler_params=_cp("arbitrary"),
        name="hy_filter",
    )(z, w1bd, b1t, w2bd, b2t, w3s, frt, dec)


@functools.lru_cache(maxsize=None)
def _fft_consts():
    n1 = FFT_N1
    half = n1 // 2
    k = np.arange(n1, dtype=np.float64)
    n2 = k[:, None, None]
    k1 = k[None, :, None]
    nn = np.arange(half, dtype=np.float64)[None, None, :]
    ang = -2.0 * np.pi * (n2 * k1 / FFT_N + nn * k1 / n1)
    mr, mi = np.cos(ang), np.sin(ang)
    m1 = np.concatenate([np.concatenate([mr, -mi], axis=2), np.concatenate([mi, mr], axis=2)], axis=1)
    ang2 = -2.0 * np.pi * np.outer(k, k) / n1
    fr, fi = np.cos(ang2), np.sin(ang2)
    f2 = np.block([[fr, -fi], [fi, fr]])
    f2c = np.block([[fr, fi], [-fi, fr]])
    no = (np.arange(half, dtype=np.float64) + n1 // 4)[None, :, None]
    kk = k[None, None, :]
    ang3 = 2.0 * np.pi * (n2 * kk / FFT_N + no * kk / n1)
    ir, ii = np.cos(ang3) / FFT_N, np.sin(ang3) / FFT_N
    m3 = np.concatenate([np.concatenate([ir, -ii], axis=2), np.concatenate([ii, ir], axis=2)], axis=1)
    return tuple(np.asarray(m, dtype=np.float32) for m in (m1, f2, f2c, m3))


FFT_NB = 16
FFT_HALF = FFT_N1 // 2


def _strided_rows(ref2d, start, n):
    return ref2d[pl.ds(start, n, stride=FFT_NB), :]


def _tok_flat(ref):
    return ref.reshape(ref.shape[0] * 2 * FFT_HALF * FFT_NB, LANE)


def _tok_row_slice(s, hf, t):
    return pl.ds((2 * s + hf) * FFT_HALF * FFT_NB + t, FFT_HALF, stride=FFT_NB)


def _tok_rows(ref, s, t):
    flat = _tok_flat(ref)
    return jnp.concatenate([flat[_tok_row_slice(s, hf, t), :] for hf in range(2)], axis=1)


def _stage_f32(dst, src_ref):
    v = src_ref[0].astype(F32).reshape(FFT_N1 * FFT_NB, HY_WIDTH)
    dst[0] = v[:, :LANE]
    dst[1] = v[:, LANE:]


def _staged_rows(scr, t):
    return jnp.concatenate([scr[hf, pl.ds(t, FFT_N1, stride=FFT_NB), :] for hf in range(2)], axis=1)


def _fft1_kernel(u_ref, m_ref, a_ref, *, nsig):
    for t in range(FFT_NB):
        rhs = _bf(jnp.concatenate([_tok_rows(u_ref, s, t) for s in range(nsig)], axis=0))
        a_ref[0, t] = _bf(_dot(m_ref[t], rhs))


def _fft1_call(u, m1, nsig):
    npair = u.shape[0] // nsig
    return pl.pallas_call(
        functools.partial(_fft1_kernel, nsig=nsig),
        grid=(npair, FFT_N1 // FFT_NB),
        in_specs=[pl.BlockSpec((nsig, 2, FFT_HALF, FFT_NB, LANE), lambda p, j: (p, 0, 0, j, 0)),
                  pl.BlockSpec((FFT_NB, 2 * FFT_N1, FFT_HALF * nsig), lambda p, j: (j, 0, 0))],
        out_specs=pl.BlockSpec((1, FFT_NB, 2 * FFT_N1, HY_WIDTH), lambda p, j: (p, j, 0, 0)),
        out_shape=jax.ShapeDtypeStruct((npair, FFT_N1, 2 * FFT_N1, HY_WIDTH), BF16),
        compiler_params=_cp("arbitrary", "arbitrary"),
        name="hy_fft1",
    )(u, m1)


def _fft2_kernel(*refs, spectrum):
    if spectrum:
        ar_ref, ai_ref, f_ref, o_ref, sr, si = refs
    else:
        ar_ref, ai_ref, f_ref, fc_ref, h_ref, o_ref, sr, si = refs
    _stage_f32(sr, ar_ref)
    _stage_f32(si, ai_ref)
    cols = lambda t: slice(t * HY_WIDTH, (t + 1) * HY_WIDTH)
    rhs = jnp.concatenate([_bf(jnp.concatenate([_staged_rows(sr, t), _staged_rows(si, t)], axis=0))
                           for t in range(FFT_NB)], axis=1)
    x = _dot(f_ref[...], rhs)
    if spectrum:
        for t in range(FFT_NB):
            o_ref[t] = x[:, cols(t)]
    else:
        ys = []
        for t in range(FFT_NB):
            xr, xi = x[:FFT_N1, cols(t)], x[FFT_N1:, cols(t)]
            hr, hi = h_ref[t, :FFT_N1], h_ref[t, FFT_N1:]
            ys.append(_bf(jnp.concatenate([xr * hr - xi * hi, xr * hi + xi * hr], axis=0)))
        out = _dot(fc_ref[...], jnp.concatenate(ys, axis=1))
        for t in range(FFT_NB):
            o_ref[0, t] = _bf(out[:, cols(t)])


def _fft2_call(a, f2, f2c, hspec):
    npair = a.shape[0]
    av = a
    nj = FFT_N1 // FFT_NB
    c2 = lambda p, j: (0, 0)
    in_specs = [pl.BlockSpec((1, FFT_N1, FFT_NB, HY_WIDTH), lambda p, j: (p, 0, j, 0)),
                pl.BlockSpec((1, FFT_N1, FFT_NB, HY_WIDTH), lambda p, j: (p, 0, nj + j, 0)),
                pl.BlockSpec((2 * FFT_N1, 2 * FFT_N1), c2)]
    staging = [pltpu.VMEM((2, FFT_N1 * FFT_NB, LANE), F32), pltpu.VMEM((2, FFT_N1 * FFT_NB, LANE), F32)]
    if hspec is None:
        return pl.pallas_call(
            functools.partial(_fft2_kernel, spectrum=True),
            grid=(1, nj),
            in_specs=in_specs,
            out_specs=pl.BlockSpec((FFT_NB, 2 * FFT_N1, HY_WIDTH), lambda p, j: (j, 0, 0)),
            out_shape=jax.ShapeDtypeStruct((FFT_N1, 2 * FFT_N1, HY_WIDTH), F32),
            scratch_shapes=staging,
            compiler_params=_cp("arbitrary", "arbitrary"),
            name="hy_fft2_spec",
        )(av, av, f2)
    in_specs += [pl.BlockSpec((2 * FFT_N1, 2 * FFT_N1), c2),
                 pl.BlockSpec((FFT_NB, 2 * FFT_N1, HY_WIDTH), lambda p, j: (j, 0, 0))]
    return pl.pallas_call(
        functools.partial(_fft2_kernel, spectrum=False),
        grid=(npair, nj),
        in_specs=in_specs,
        out_specs=pl.BlockSpec((1, FFT_NB, 2 * FFT_N1, HY_WIDTH), lambda p, j: (p, j, 0, 0)),
        out_shape=jax.ShapeDtypeStruct((npair, FFT_N1, 2 * FFT_N1, HY_WIDTH), BF16),
        scratch_shapes=staging,
        compiler_params=_cp("arbitrary", "arbitrary"),
        name="hy_fft2",
    )(av, av, f2, f2c, hspec)


def _fft3_kernel(br_ref, bi_ref, m_ref, vg_ref, x0_ref, bias_ref, y_ref, sr, si):
    _stage_f32(sr, br_ref)
    _stage_f32(si, bi_ref)
    for t in range(FFT_NB):
        rhs = _bf(jnp.concatenate([_staged_rows(sr, t), _staged_rows(si, t)], axis=0))
        out = _dot(m_ref[t], rhs)
        for s in range(2):
            conv = out[s * FFT_HALF:(s + 1) * FFT_HALF]
            y = (conv + _tok_rows(vg_ref, s, t) * bias_ref[...]) * _tok_rows(x0_ref, s, t)
            for hf in range(2):
                _tok_flat(y_ref)[_tok_row_slice(s, hf, t), :] = y[:, hf * LANE:(hf + 1) * LANE]


def _fft3_call(bmat, m3, vg, x0, bias, layer):
    npair = bmat.shape[0]
    nj = FFT_N1 // FFT_NB
    tok_spec = pl.BlockSpec((2, 2, FFT_HALF, FFT_NB, LANE), lambda p, j: (p, 0, 0, j, 0))
    return pl.pallas_call(
        _fft3_kernel,
        grid=(npair, nj),
        in_specs=[pl.BlockSpec((1, FFT_N1, FFT_NB, HY_WIDTH), lambda p, j: (p, 0, j, 0)),
                  pl.BlockSpec((1, FFT_N1, FFT_NB, HY_WIDTH), lambda p, j: (p, 0, nj + j, 0)),
                  pl.BlockSpec((FFT_NB, FFT_N1, 2 * FFT_N1), lambda p, j: (j, 0, 0)),
                  tok_spec, tok_spec,
                  _lspec((1, HY_WIDTH), layer)],
        out_specs=tok_spec,
        out_shape=jax.ShapeDtypeStruct(vg.shape, F32),
        scratch_shapes=[pltpu.VMEM((2, FFT_N1 * FFT_NB, LANE), F32), pltpu.VMEM((2, FFT_N1 * FFT_NB, LANE), F32)],
        compiler_params=_cp("arbitrary", "arbitrary"),
        name="hy_fft3",
    )(bmat, bmat, m3, vg, x0, bias)


def _hy_direct_kernel(vg_ref, x0_ref, h_ref, bias_ref, y_ref, pad_scr, sh_scr):
    L = vg_ref.shape[2]
    u = jnp.concatenate([vg_ref[0, 0], vg_ref[0, 1]], axis=1)
    pad_scr[...] = jnp.zeros_like(pad_scr)
    pad_scr[L:2 * L, :] = u
    top = L + L // 2
    acc = jnp.zeros((L, HY_WIDTH), F32)
    for r in range(8):
        sh_scr[...] = pad_scr[r:r + 3 * L - 8, :]
        a_lo = -(-(top - L + 1 - r) // 8)
        a_hi = (top - r) // 8

        def body(a, acc, r=r):
            m = top - (a * 8 + r)
            tap = jnp.concatenate([h_ref[0, pl.ds(m, 1), :], h_ref[1, pl.ds(m, 1), :]], axis=1)
            return acc + tap * sh_scr[pl.ds(pl.multiple_of(a * 8, 8), L), :]

        acc = lax.fori_loop(a_lo, a_hi + 1, body, acc, unroll=4)
    y = (acc + u * bias_ref[...]) * jnp.concatenate([x0_ref[0, 0], x0_ref[0, 1]], axis=1)
    y_ref[0, 0] = y[:, :LANE]
    y_ref[0, 1] = y[:, LANE:]


def _hy_direct_call(vg, x0, h, bias, layer):
    bsz, _, L, _ = vg.shape
    tok = pl.BlockSpec((1, 2, L, LANE), lambda b: (b, 0, 0, 0))
    return pl.pallas_call(
        _hy_direct_kernel,
        grid=(bsz,),
        in_specs=[tok, tok, pl.BlockSpec((2, L, LANE), lambda b: (0, 0, 0)),
                  _lspec((1, HY_WIDTH), layer)],
        out_specs=tok,
        out_shape=jax.ShapeDtypeStruct(vg.shape, F32),
        scratch_shapes=[pltpu.VMEM((3 * L, HY_WIDTH), F32), pltpu.VMEM((3 * L - 8, HY_WIDTH), F32)],
        compiler_params=_cp("arbitrary"),
        name="hy_direct",
    )(vg, x0, h, bias)


def _out_kernel(x_ref, yg_ref, yh_ref, ys_ref, mod_ref, g2_ref, gf_ref, wo_ref, w13_ref, w2_ref,
                o_ref, *scr, final, col_major):
    m = mod_ref[0]
    if col_major:
        ys_scr, = scr
        for c in range(GRID_W):
            for k in range(SSD_INNER // LANE):
                ys_scr[k, pl.ds(c, ROWS_PER_TILE, stride=GRID_W), :] = ys_ref[0, c, :, k * LANE:(k + 1) * LANE]
        ys = jnp.concatenate([ys_scr[k] for k in range(SSD_INNER // LANE)], axis=1)
    else:
        ys = ys_ref[...]
    yh = jnp.concatenate([yh_ref[0, 0], yh_ref[0, 1]], axis=1)
    tm = x_ref.shape[0]
    nrg = ROW_GROUPS if tm % (8 * ROW_GROUPS) == 0 else 1
    rgs = [slice(r * (tm // nrg), (r + 1) * (tm // nrg)) for r in range(nrg)]
    ycat = [jnp.concatenate([_bf(yg_ref[r, :]), _bf(yh[r]), _bf(ys[r])], axis=1) for r in rgs]
    mix = [_dot(t, wo_ref[...]) for t in ycat]
    x1 = [x_ref[r, :] + m[2:3] * mx for r, mx in zip(rgs, mix)]
    ms = [jnp.mean(t * t, axis=-1, keepdims=True) for t in x1]
    h = [t * lax.rsqrt(s + EPS) * g2_ref[...] for t, s in zip(x1, ms)]
    h = [_bf(t * (1.0 + m[4:5]) + m[3:4]) for t in h]
    ab = [_dot(t, w13_ref[...]) for t in h]
    ffn = [_dot(_bf(_silu(t[:, :D_FF]) * t[:, D_FF:]), w2_ref[...]) for t in ab]
    for r, t, f in zip(rgs, x1, ffn):
        x2 = t + m[5:6] * f
        if final:
            ms2 = jnp.mean(x2 * x2, axis=-1, keepdims=True)
            x2 = x2 * lax.rsqrt(ms2 + EPS) * gf_ref[...]
        o_ref[r, :] = x2


def _out_call(x2, yg, yh, ys, mods, mod_row, g2, gf, wo, w13, w2, layer, final, col_major):
    t = x2.shape[0]
    seq = yh.shape[2]
    tm = min(TOK_TILE, seq)
    tiles = seq // tm
    c2 = lambda i: (0, 0)
    if col_major:
        assert tm == TOK_TILE
        ys_spec = pl.BlockSpec((1, GRID_W, ROWS_PER_TILE, SSD_INNER), lambda i: (i // tiles, 0, i % tiles, 0))
        extra = [pltpu.VMEM((SSD_INNER // LANE, tm, LANE), F32)]
    else:
        ys = ys.reshape(t, SSD_INNER)
        ys_spec = pl.BlockSpec((tm, SSD_INNER), lambda i: (i, 0))
        extra = []
    resident = lambda shape: pl.BlockSpec((None,) + shape, lambda i: (layer, 0, 0), pipeline_mode=pl.Buffered(1))
    return pl.pallas_call(
        functools.partial(_out_kernel, final=final, col_major=col_major),
        grid=(t // tm,),
        in_specs=[pl.BlockSpec((tm, D_MODEL), lambda i: (i, 0)),
                  pl.BlockSpec((tm, GLA_V), lambda i: (i, 0)),
                  pl.BlockSpec((1, 2, tm, LANE), lambda i: (i // tiles, 0, i % tiles, 0)),
                  ys_spec,
                  pl.BlockSpec((None, 1, 6, D_MODEL), lambda i: (layer, mod_row(i), 0, 0)),
                  _lspec((1, D_MODEL), layer),
                  pl.BlockSpec((1, D_MODEL), c2),
                  resident((D_MODEL, D_MODEL)),
                  resident((D_MODEL, 2 * D_FF)),
                  resident((D_FF, D_MODEL))],
        out_specs=pl.BlockSpec((tm, D_MODEL), lambda i: (i, 0)),
        out_shape=jax.ShapeDtypeStruct((t, D_MODEL), F32),
        scratch_shapes=extra,
        compiler_params=_cp("arbitrary"),
        name="out_ffn",
    )(x2, yg, yh, ys, mods, g2, gf.reshape(1, D_MODEL), wo, w13, w2)


W_IN_MOVES = ((0, 192, 0), (192, 384, 256), (384, 1184, 512), (1184, 3244, PG_W))
D_IN = 3244


def _pack_kernel(w_ref, o_ref):
    o_ref[...] = jnp.zeros(o_ref.shape, BF16)
    for src0, src1, dst in W_IN_MOVES:
        o_ref[0, :, dst:dst + src1 - src0] = w_ref[0, :, src0:src1].astype(BF16)


def _pack_w_in(w_in):
    rows = 128
    wtot = PG_W + PH_W + PS_W
    return pl.pallas_call(
        _pack_kernel,
        grid=(DEPTH, D_MODEL // rows),
        in_specs=[pl.BlockSpec((1, rows, D_IN), lambda l, i: (l, i, 0))],
        out_specs=pl.BlockSpec((1, rows, wtot), lambda l, i: (l, i, 0)),
        out_shape=jax.ShapeDtypeStruct((DEPTH, D_MODEL, wtot), BF16),
        compiler_params=_cp("arbitrary", "arbitrary"),
        name="pack_w_in",
    )(w_in)


def _pad_to(a, shape):
    return jnp.pad(a, [(0, s - d) for d, s in zip(a.shape, shape)])


def _mixers(pg, vg, x0, ps, lw, l, states):
    g_f0, g_b0, m_f0, m_b0 = states
    o_f, g_f = _gla_call(pg, None, lw['wgk_f'], lw['bgk_f'], None, l, g_f0, False)
    gla_y, g_b = _gla_call(pg, o_f, lw['wgk_b'], lw['bgk_b'], lw['gla_nw'], l, g_b0, True)

    L = vg.shape[2]
    h = _hy_filter_call(L, lw['hy_w1'], lw['hy_b1'], lw['hy_w2'], lw['hy_b2'], lw['hy_w3'], lw['hy_freq'],
                        lw['hy_decay'], l)
    if L == FFT_N // 2:
        m1, m1_real, f2, f2c, m3 = lw['fft']
        tok5 = lambda t: t.reshape(t.shape[0], 2, FFT_HALF, FFT_N1, LANE)
        hspec = _fft2_call(_fft1_call(tok5(h[None]), m1_real, 1), f2, f2c, None)
        vg5 = tok5(vg)
        bmat = _fft2_call(_fft1_call(vg5, m1, 2), f2, f2c, hspec)
        hy_y = _fft3_call(bmat, m3, vg5, tok5(x0), lw['hy_bias'], l).reshape(vg.shape)
    else:
        hy_y = _hy_direct_call(vg, x0, h, lw['hy_bias'], l)

    y_f, act, dta, m_f = _ssd_fwd_call(ps, lw['ssd_cw'], lw['ssd_cb'], lw['ssd_dtb'], lw['ssd_alog'], l, m_f0)
    ssd_y, m_b = _ssd_bwd_call(ps, act, dta, y_f, lw['ssd_dx'], lw['ssd_nw'], l, m_b0)
    return (gla_y, hy_y, ssd_y), (g_f, g_b, m_f, m_b)


def kernel(x, c, ctx, c_ctx, mod_w, mod_b, norm1_g, norm2_g, w_in, gla_gk_w_f, gla_gk_b_f, gla_gk_w_b, gla_gk_b_b, gla_norm_w, hy_short_w, hy_short_b, hy_w1, hy_b1, hy_w2, hy_b2, hy_w3, hy_freq, hy_decay, hy_bias, ssd_conv_w, ssd_conv_b, ssd_dt_bias_f, ssd_dt_bias_b, ssd_a_log_f, ssd_a_log_b, ssd_d, ssd_norm_w, w_out, ffn_w1, ffn_w3, ffn_w2, final_g):
    bsz, seq, _ = x.shape
    lc = ctx.shape[1]
    c8 = jnp.concatenate([c, c_ctx[None], jnp.zeros((8 - bsz - 1, D_MODEL), F32)], axis=0)
    mods_all = _mod_call(c8, mod_w, mod_b).reshape(DEPTH, 8, 6, D_MODEL)

    w_in_p = _pack_w_in(w_in)
    m1_c, f2_c, f2c_c, m3_c = _fft_consts()
    fft_mats = tuple(_bf(jnp.asarray(m)) for m in (m1_c, np.ascontiguousarray(m1_c[:, :, :FFT_HALF]), f2_c, f2c_c, m3_c))
    w_out_b, w13_b, w2_b = _bf(w_out), _bf(jnp.concatenate([ffn_w1, ffn_w3], axis=-1)), _bf(ffn_w2)

    xt = x.reshape(bsz * seq, D_MODEL)
    ct = ctx.reshape(bsz * lc, D_MODEL)
    tiles_per_seq = seq // TOK_TILE
    row_x = lambda i: i // tiles_per_seq
    row_c = lambda i: bsz

    zeros_states = (jnp.zeros((bsz, GLA_DV, 256), F32), jnp.zeros((bsz, GLA_DV, 256), F32),
                    jnp.zeros((bsz, 2 * SSD_STATE, SSD_INNER), F32),
                    jnp.zeros((bsz, 2 * SSD_STATE, SSD_INNER), F32))

    def gkw(w, off):
        return _bf(_pad_to(jnp.pad(w, ((0, 0), (off, 0), (0, 0))), (DEPTH, LANE, 256)))

    def bdiag(w):
        z = jnp.zeros_like(w)
        return jnp.concatenate([jnp.concatenate([w, z], axis=2), jnp.concatenate([z, w], axis=2)], axis=1)

    row = lambda v: v[:, None, :]
    lw = {
        'wgk_f': gkw(gla_gk_w_f, 0), 'wgk_b': gkw(gla_gk_w_b, GLA_LOWRANK),
        'bgk_f': _pad_to(row(gla_gk_b_f), (DEPTH, 1, 256)), 'bgk_b': _pad_to(row(gla_gk_b_b), (DEPTH, 1, 256)),
        'gla_nw': row(jnp.tile(gla_norm_w, (1, GLA_HEADS))),
        'hy_sw': _pad_to(hy_short_w, (DEPTH, 8, PH_W)), 'hy_sb': row(hy_short_b),
        'hy_w1': bdiag(_pad_to(hy_w1, (DEPTH, HY_SLOT, HY_SLOT))), 'hy_b1': row(jnp.tile(hy_b1, (1, 2))),
        'hy_w2': bdiag(hy_w2), 'hy_b2': row(jnp.tile(hy_b2, (1, 2))),
        'hy_w3': jnp.stack([jnp.pad(hy_w3, ((0, 0), (0, HY_SLOT), (0, 0))),
                            jnp.pad(hy_w3, ((0, 0), (HY_SLOT, 0), (0, 0)))], axis=1),
        'hy_freq': row(jnp.tile(hy_freq, (1, 2))),
        'hy_decay': row(hy_decay), 'hy_bias': row(hy_bias),
        'ssd_cw': _pad_to(ssd_conv_w, (DEPTH, 8, SSD_CONV_DIM)), 'ssd_cb': row(ssd_conv_b),
        'ssd_dtb': _pad_to(row(jnp.concatenate([ssd_dt_bias_f, ssd_dt_bias_b], axis=1)), (DEPTH, 1, LANE)),
        'ssd_alog': _pad_to(row(jnp.concatenate([ssd_a_log_f, ssd_a_log_b], axis=1)), (DEPTH, 1, LANE)),
        'ssd_dx': row(jnp.repeat(ssd_d, SSD_P, axis=1)), 'ssd_nw': row(ssd_norm_w),
        'fft': fft_mats,
    }
    g1, g2 = row(norm1_g), row(norm2_g)

    for l in range(DEPTH):
        in_args = (g1, w_in_p, l, lw['hy_sw'], lw['hy_sb'], bsz)
        yc, ctx_states = _mixers(*_in_call(ct, mods_all, row_c, *in_args, False), lw, l, zeros_states)
        yx, _ = _mixers(*_in_call(xt, mods_all, row_x, *in_args, True), lw, l, ctx_states)
        last = l == DEPTH - 1
        ffn = (w_out_b, w13_b, w2_b, l)
        xt = _out_call(xt, yx[0].reshape(-1, GLA_V), yx[1], yx[2],
                       mods_all, row_x, g2, final_g, *ffn, last, True)
        if not last:
            ct = _out_call(ct, yc[0].reshape(-1, GLA_V), yc[1], yc[2],
                           mods_all, row_c, g2, final_g, *ffn, False, False)
    return xt.reshape(bsz, seq, D_MODEL)
```
